```python
import jax, jax.numpy as jnp
from jax import lax
import numpy as np

D_MODEL = 2048
BATCH = 8
SEQ = 4096
DEPTH = 2

D_MIX = D_MODEL
W_ATTN = D_MIX // 4
W_CONV = D_MIX // 4
W_HGRN = D_MIX // 4
W_RET = D_MIX - W_ATTN - W_CONV - W_HGRN

HEAD_DIM = 64
N_ATTN_HEADS = W_ATTN // HEAD_DIM
N_KV_HEADS = 2
WINDOW = 128
ATTN_BLOCK = 128

CONV_WIDTH = 3

HGRN_HEADS = 4
HGRN_DK = W_HGRN // HGRN_HEADS
HGRN_DV = W_HGRN // HGRN_HEADS
HGRN_CHUNK = 64

RET_HEADS = 4
RET_DK = W_RET // RET_HEADS
RET_DV = W_RET // RET_HEADS
RET_CHUNK = 128

N_EXPERTS = 16
N_GROUPS = 4
EXPERTS_PER_GROUP = N_EXPERTS // N_GROUPS
TOP_K = 2
D_EXPERT = 1024

DEEPNORM_ALPHA = (2.0 * DEPTH) ** 0.25
DEEPNORM_BETA = (8.0 * DEPTH) ** -0.25
LN_EPS = 1e-5
HEAD_NORM_EPS = 1e-6

SPLIT_SIZES = (
    N_ATTN_HEADS * HEAD_DIM,
    N_KV_HEADS * HEAD_DIM,
    N_KV_HEADS * HEAD_DIM,
    W_CONV,
    W_CONV,
    W_CONV,
    HGRN_HEADS * HGRN_DK,
    HGRN_HEADS * HGRN_DK,
    HGRN_HEADS * HGRN_DK,
    HGRN_HEADS * HGRN_DV,
    W_HGRN,
    RET_HEADS * RET_DK,
    RET_HEADS * RET_DK,
    RET_HEADS * RET_DV,
    W_RET,
)
D_IN_PROJ = sum(SPLIT_SIZES)
SPLIT_POINTS = tuple(sum(SPLIT_SIZES[:i + 1]) for i in range(len(SPLIT_SIZES) - 1))

kernel_name = "hybrid_parallel_groups_deepnorm_grouped_moe"


def layer_norm(x, g, b):
    xf = x.astype(jnp.float32)
    mu = xf.mean(-1, keepdims=True)
    var = jnp.square(xf - mu).mean(-1, keepdims=True)
    return ((xf - mu) * lax.rsqrt(var + LN_EPS)).astype(x.dtype) * g + b


def rms_norm_heads(o, g):
    o = o * lax.rsqrt(jnp.square(o).mean(-1, keepdims=True) + HEAD_NORM_EPS)
    return o.reshape(o.shape[0], o.shape[1], -1) * g.astype(jnp.float32)


def group_norm_heads(o, g):
    o = o - o.mean(-1, keepdims=True)
    o = o * lax.rsqrt(jnp.square(o).mean(-1, keepdims=True) + HEAD_NORM_EPS)
    return o.reshape(o.shape[0], o.shape[1], -1) * g.astype(jnp.float32)


def flip_seq(t):
    return jnp.flip(t, axis=1)


def windowed_gqa_attention(q, k, v, sink):
    B, S = q.shape[0], q.shape[1]
    L = ATTN_BLOCK
    nb = S // L
    G = N_ATTN_HEADS // N_KV_HEADS
    qb = q.reshape(B, nb, L, N_KV_HEADS, G, HEAD_DIM)

    def neighbours(t):
        tb = t.reshape(B, nb, L, N_KV_HEADS, HEAD_DIM)
        tp = jnp.pad(tb, ((0, 0), (1, 1), (0, 0), (0, 0), (0, 0)))
        return jnp.concatenate([tp[:, :-2], tp[:, 1:-1], tp[:, 2:]], axis=2)

    kb, vb = neighbours(k), neighbours(v)
    scores = jnp.einsum('bnqhgd,bnkhd->bnhgqk', qb, kb,
                        preferred_element_type=jnp.float32) * (HEAD_DIM ** -0.5)

    k_rel = jnp.arange(3 * L) - L
    dist = k_rel[None, :] - jnp.arange(L)[:, None]
    k_abs = jnp.arange(nb)[:, None] * L + k_rel[None, :]
    valid = (k_abs >= 0) & (k_abs < S)
    mask = (jnp.abs(dist) <= WINDOW)[None] & valid[:, None, :]

    slopes = 2.0 ** (-8.0 * jnp.arange(1, N_ATTN_HEADS + 1, dtype=jnp.float32) / N_ATTN_HEADS)
    alibi = -slopes.reshape(N_KV_HEADS, G)[:, :, None, None] * jnp.abs(dist).astype(jnp.float32)
    scores = jnp.where(mask[None, :, None, None], scores + alibi, -jnp.inf)

    sink_l = sink.astype(jnp.float32).reshape(N_KV_HEADS, G)[None, None, :, :, None, None]
    m = jnp.maximum(scores.max(-1, keepdims=True), sink_l)
    p = jnp.exp(scores - m)
    probs = p / (p.sum(-1, keepdims=True) + jnp.exp(sink_l - m))
    out = jnp.einsum('bnhgqk,bnkhd->bnqhgd', probs.astype(vb.dtype), vb)
    return out.reshape(B, S, N_ATTN_HEADS * HEAD_DIM)


def short_conv_mixer(b_gate, c_gate, h, conv_w):
    u = c_gate * h
    up = jnp.pad(u, ((0, 0), (1, 1), (0, 0)))
    y = conv_w[0] * up[:, :-2] + conv_w[1] * up[:, 1:-1] + conv_w[2] * up[:, 2:]
    return b_gate * y


def hgrn2_chunk_scan(q, k, log_f, v):
    B, S, H, dk = q.shape
    dv = v.shape[-1]
    L = HGRN_CHUNK
    nc = S // L

    def chunks(t):
        return t.reshape(B, nc, L, H, t.shape[-1]).transpose(1, 0, 3, 2, 4)

    causal = jnp.tril(jnp.ones((L, L), dtype=bool))[..., None]

    def step(state, inp):
        qi, ki, gi, vi = inp
        b = jnp.cumsum(gi.astype(jnp.float32), axis=2)
        diff = b[:, :, :, None, :] - b[:, :, None, :, :]
        decay = jnp.exp(jnp.where(causal, diff, -jnp.inf))
        a = jnp.einsum('bhtd,bhsd,bhtsd->bhts', qi, ki, decay)
        o_intra = jnp.einsum('bhts,bhsv->bhtv', a, vi)
        o_inter = jnp.einsum('bhtd,bhdv->bhtv', qi * jnp.exp(b), state)
        b_last = b[:, :, -1:, :]
        k_dec = ki * jnp.exp(b_last - b)
        new_state = jnp.exp(b_last[:, :, 0, :])[..., None] * state + \
            jnp.einsum('bhsd,bhsv->bhdv', k_dec, vi)
        return new_state, o_intra + o_inter

    state0 = jnp.zeros((B, H, dk, dv), jnp.float32)
    _, o = lax.scan(step, state0, (chunks(q), chunks(k), chunks(log_f), chunks(v)))
    return o.transpose(1, 0, 3, 2, 4).reshape(B, S, H, dv)


def hgrn2_mixer(q, z_fwd, z_bwd, i, g, lb, norm_g):
    lb = lb.reshape(HGRN_HEADS, HGRN_DK)

    def gates(z):
        z = z.astype(jnp.float32)
        log_f = jnp.logaddexp(jnp.log(lb), jnp.log1p(-lb) + jax.nn.log_sigmoid(z))
        k = (1.0 - lb) * jax.nn.sigmoid(-z)
        return k, log_f

    k_f, lf_f = gates(z_fwd)
    k_b, lf_b = gates(z_bwd)
    o = hgrn2_chunk_scan(q, k_f, lf_f, i) + flip_seq(
        hgrn2_chunk_scan(flip_seq(q), flip_seq(k_b), flip_seq(lf_b), flip_seq(i)))
    return rms_norm_heads(o, norm_g) * jax.nn.silu(g.astype(jnp.float32))


def retention_chunkwise(q, k, v, log_gamma):
    B, S, H, dk = q.shape
    dv = v.shape[-1]
    L = RET_CHUNK
    nc = S // L
    qc = q.reshape(B, nc, L, H, dk)
    kc = k.reshape(B, nc, L, H, dk)
    vc = v.reshape(B, nc, L, H, dv)
    pos = jnp.arange(L, dtype=jnp.float32)
    rel = pos[:, None] - pos[None, :]
    D = jnp.exp(jnp.where(rel[None] >= 0, log_gamma[:, None, None] * rel[None], -jnp.inf))
    a = jnp.einsum('bnthd,bnshd->bnhts', qc, kc) * D
    o_intra = jnp.einsum('bnhts,bnshv->bnthv', a, vc)
    k_dec = kc * jnp.exp(log_gamma[None, :] * (L - 1 - pos)[:, None])[:, :, None]
    kv = jnp.einsum('bnshd,bnshv->nbhdv', k_dec, vc)
    chunk_decay = jnp.exp(log_gamma * L)[None, :, None, None]

    def step(state, kv_c):
        return chunk_decay * state + kv_c, state

    _, prev = lax.scan(step, jnp.zeros((B, H, dk, dv), jnp.float32), kv)
    q_dec = qc * jnp.exp(log_gamma[None, :] * (pos + 1.0)[:, None])[:, :, None]
    o_inter = jnp.einsum('bnthd,nbhdv->bnthv', q_dec, prev)
    return (o_intra + o_inter).reshape(B, S, H, dv)


def retention_mixer(q, k, v, g, decay_logit, norm_g):
    log_gamma = jax.nn.log_sigmoid(decay_logit.astype(jnp.float32))
    k = k * (RET_DK ** -0.5)
    o = retention_chunkwise(q, k, v, log_gamma[0]) + flip_seq(
        retention_chunkwise(flip_seq(q), flip_seq(k), flip_seq(v), log_gamma[1]))
    return group_norm_heads(o, norm_g) * jax.nn.silu(g.astype(jnp.float32))


def token_mixing(h, w_in, w_out, attn_sink, conv_w, lb, hgrn_norm_g, ret_decay_logit, ret_norm_g):
    B, S, _ = h.shape
    proj = h @ w_in
    (a_q, a_k, a_v, c_b, c_c, c_h, g_q, g_zf, g_zb, g_i, g_o,
     r_q, r_k, r_v, r_g) = jnp.split(proj, SPLIT_POINTS, axis=-1)

    def heads(t, n):
        return t.reshape(B, S, n, -1)

    y_attn = windowed_gqa_attention(heads(a_q, N_ATTN_HEADS), heads(a_k, N_KV_HEADS),
                                    heads(a_v, N_KV_HEADS), attn_sink)
    y_conv = short_conv_mixer(c_b, c_c, c_h, conv_w)
    y_hgrn = hgrn2_mixer(heads(g_q, HGRN_HEADS), heads(g_zf, HGRN_HEADS), heads(g_zb, HGRN_HEADS),
                         heads(g_i, HGRN_HEADS), g_o, lb, hgrn_norm_g)
    y_ret = retention_mixer(heads(r_q, RET_HEADS), heads(r_k, RET_HEADS), heads(r_v, RET_HEADS),
                            r_g, ret_decay_logit, ret_norm_g)
    y = jnp.concatenate([y_attn.astype(h.dtype), y_conv.astype(h.dtype),
                         y_hgrn.astype(h.dtype), y_ret.astype(h.dtype)], axis=-1)
    return y @ w_out


def grouped_moe(h, router_w, router_b, w_gate, w_up, w_down):
    B, S, D = h.shape
    t = h.reshape(-1, D)
    logits = (t @ router_w).astype(jnp.float32) + router_b.astype(jnp.float32)
    probs = jax.nn.softmax(logits, axis=-1)
    grouped = probs.reshape(-1, N_GROUPS, EXPERTS_PER_GROUP)
    group_score = lax.top_k(grouped, TOP_K)[0].sum(-1)
    g_sel = jnp.argmax(group_score, axis=-1)
    in_group = jnp.take_along_axis(grouped, g_sel[:, None, None], axis=1)[:, 0]
    top_p, top_i = lax.top_k(in_group, TOP_K)
    weights = top_p / top_p.sum(-1, keepdims=True)
    expert_ids = g_sel[:, None] * EXPERTS_PER_GROUP + top_i
    combine = (jax.nn.one_hot(expert_ids, N_EXPERTS, dtype=jnp.float32)
               * weights[..., None]).sum(1).astype(t.dtype)
    out = jnp.zeros_like(t)
    for e in range(N_EXPERTS):
        hid = jax.nn.silu(t @ w_gate[e]) * (t @ w_up[e])
        out = out + combine[:, e:e + 1] * (hid @ w_down[e])
    return out.reshape(B, S, D)


def setup_inputs(seed: int = 0) -> dict:
    key = jax.random.key(seed)
    ks = jax.random.split(key, 21)
    f32 = jnp.float32

    def nrm(k, shape, scale):
        return jax.random.normal(k, shape, f32) * scale

    gamma0 = 1.0 - 2.0 ** (-5.0 - np.arange(RET_HEADS, dtype=np.float32))
    ret_logit0 = jnp.asarray(np.log(gamma0) - np.log1p(-gamma0), dtype=f32)
    return {
        "x": nrm(ks[0], (BATCH, SEQ, D_MODEL), 1.0),
        "emb_ln_g": 1.0 + nrm(ks[1], (D_MODEL,), 0.02),
        "emb_ln_b": nrm(ks[2], (D_MODEL,), 0.02),
        "w_in": nrm(ks[3], (DEPTH, D_MODEL, D_IN_PROJ), D_MODEL ** -0.5),
        "attn_sink": nrm(ks[4], (DEPTH, N_ATTN_HEADS), 0.5),
        "conv_w": nrm(ks[5], (DEPTH, CONV_WIDTH, W_CONV), CONV_WIDTH ** -0.5),
        "hgrn_lb": nrm(ks[6], (DEPTH, HGRN_HEADS * HGRN_DK), 1.0),
        "hgrn_norm_g": 1.0 + nrm(ks[7], (DEPTH, W_HGRN), 0.02),
        "ret_decay_logit": ret_logit0[None, None, :] + nrm(ks[8], (DEPTH, 2, RET_HEADS), 0.01),
        "ret_norm_g": 1.0 + nrm(ks[9], (DEPTH, W_RET), 0.02),
        "w_out": nrm(ks[10], (DEPTH, D_MIX, D_MODEL), D_MIX ** -0.5 * DEEPNORM_BETA),
        "ln1_g": 1.0 + nrm(ks[11], (DEPTH, D_MODEL), 0.02),
        "ln1_b": nrm(ks[12], (DEPTH, D_MODEL), 0.02),
        "router_w": nrm(ks[13], (D_MODEL, N_EXPERTS), D_MODEL ** -0.5),
        "router_b": nrm(ks[14], (N_EXPERTS,), 0.01),
        "w_gate": nrm(ks[15], (DEPTH, N_EXPERTS, D_MODEL, D_EXPERT), D_MODEL ** -0.5),
        "w_up": nrm(ks[16], (DEPTH, N_EXPERTS, D_MODEL, D_EXPERT), D_MODEL ** -0.5),
        "w_down": nrm(ks[17], (DEPTH, N_EXPERTS, D_EXPERT, D_MODEL), D_EXPERT ** -0.5 * DEEPNORM_BETA),
        "ln2_g": 1.0 + nrm(ks[18], (DEPTH, D_MODEL), 0.02),
        "ln2_b": nrm(ks[19], (DEPTH, D_MODEL), 0.02),
    }


def reference(x, emb_ln_g, emb_ln_b, w_in, attn_sink, conv_w, hgrn_lb, hgrn_norm_g,
              ret_decay_logit, ret_norm_g, w_out, ln1_g, ln1_b, router_w, router_b,
              w_gate, w_up, w_down, ln2_g, ln2_b):
    lb_all = jnp.cumsum(jax.nn.softmax(hgrn_lb.astype(jnp.float32), axis=0), axis=0)
    lb_all = lb_all - lb_all[0:1]
    h = layer_norm(x, emb_ln_g, emb_ln_b)
    for l in range(DEPTH):
        mix = token_mixing(h, w_in[l], w_out[l], attn_sink[l], conv_w[l], lb_all[l],
                           hgrn_norm_g[l], ret_decay_logit[l], ret_norm_g[l])
        h = layer_norm(DEEPNORM_ALPHA * h + mix, ln1_g[l], ln1_b[l])
        ffn = grouped_moe(h, router_w, router_b, w_gate[l], w_up[l], w_down[l])
        h = layer_norm(DEEPNORM_ALPHA * h + ffn, ln2_g[l], ln2_b[l])
    return h
```

```python
import functools

import jax
import jax.numpy as jnp
from jax import lax
from jax.experimental import pallas as pl
from jax.experimental.pallas import tpu as pltpu

F32 = jnp.float32
BF16 = jnp.bfloat16

D_MODEL = 2048
DEPTH = 2
W_GROUP = 512
HEAD_DIM = 64
N_ATTN_HEADS = 8
N_KV_HEADS = 2
ATTN_GROUP = N_ATTN_HEADS // N_KV_HEADS
WINDOW = 128
ATTN_BLOCK = 128
REC_HEADS = 4
REC_DIM = 128
N_EXPERTS = 16
N_GROUPS = 4
EXPERTS_PER_GROUP = 4
D_EXPERT = 1024
D_IN_PROJ = 6912
DEEPNORM_ALPHA = (2.0 * DEPTH) ** 0.25
LN_EPS = 1e-5
HEAD_NORM_EPS = 1e-6
NEG_BIG = -1e30

LANES = 128
COL_AQ, COL_AK, COL_AV = 0, 4, 5
COL_CB, COL_CC, COL_CH = 6, 10, 14
COL_GQ, COL_GZF, COL_GZB, COL_GI, COL_GO = 18, 22, 26, 30, 34
COL_RQ, COL_RK, COL_RV, COL_RG = 38, 42, 46, 50

CHUNK = 128
VMEM_LIMIT = 56 * 1024 * 1024

_NT = (((1,), (1,)), ((), ()))
_TN = (((0,), (0,)), ((), ()))


def _params(sem, vmem=VMEM_LIMIT):
    return pltpu.CompilerParams(dimension_semantics=sem, vmem_limit_bytes=vmem)


def _layer_norm(x, g, b):
    mu = jnp.mean(x, axis=-1, keepdims=True)
    xc = x - mu
    var = jnp.mean(xc * xc, axis=-1, keepdims=True)
    return xc * lax.rsqrt(var + LN_EPS) * g + b


def _silu(x):
    return x * (1.0 / (1.0 + jnp.exp(-x)))


def _embed_ln_kernel(x_ref, g_ref, b_ref, h_ref, hb_ref):
    h = _layer_norm(x_ref[...], g_ref[...], b_ref[...])
    h_ref[...] = h
    hb_ref[...] = h.astype(BF16)


def embed_ln(x2, g, b, tm=512):
    T, D = x2.shape
    return pl.pallas_call(
        _embed_ln_kernel,
        grid=(T // tm,),
        in_specs=[pl.BlockSpec((tm, D), lambda i: (i, 0)),
                  pl.BlockSpec((1, D), lambda i: (0, 0)),
                  pl.BlockSpec((1, D), lambda i: (0, 0))],
        out_specs=[pl.BlockSpec((tm, D), lambda i: (i, 0)),
                   pl.BlockSpec((tm, D), lambda i: (i, 0))],
        out_shape=[jax.ShapeDtypeStruct((T, D), F32), jax.ShapeDtypeStruct((T, D), BF16)],
        compiler_params=_params(("parallel",)),
        name="embed_ln",
    )(x2, g.reshape(1, D), b.reshape(1, D))


def _in_proj_kernel(x_ref, w_ref, o_ref):
    o_ref[...] = jnp.dot(x_ref[...], w_ref[...], preferred_element_type=F32)


def in_proj(hb, w, tm=1024, tn=768):
    T, K = hb.shape
    N = w.shape[1]
    tm = min(tm, T)
    return pl.pallas_call(
        _in_proj_kernel,
        grid=(N // tn, T // tm),
        in_specs=[pl.BlockSpec((tm, K), lambda n, m: (m, 0)),
                  pl.BlockSpec((K, tn), lambda n, m: (0, n))],
        out_specs=pl.BlockSpec((tm, tn), lambda n, m: (m, n)),
        out_shape=jax.ShapeDtypeStruct((T, N), F32),
        compiler_params=_params(("parallel", "parallel")),
        name="in_proj",
    )(hb, w)


def _attn_kernel(sink_ref, q_ref, kp_ref, kc_ref, kn_ref, vp_ref, vc_ref, vn_ref, bias_ref, o_ref):
    L = ATTN_BLOCK
    n = pl.program_id(1)
    nb = pl.num_programs(1)
    col = lax.broadcasted_iota(jnp.int32, (1, 3 * L), 1)
    valid = ((col >= L) | (n > 0)) & ((col < 2 * L) | (n < nb - 1))
    edge = jnp.where(valid, 0.0, NEG_BIG)
    q = q_ref[0]
    k3 = jnp.concatenate([kp_ref[0], kc_ref[0], kn_ref[0]], axis=0)
    v3 = jnp.concatenate([vp_ref[0], vc_ref[0], vn_ref[0]], axis=0)
    outs = []
    for h in range(N_KV_HEADS):
        kh = k3[:, h * HEAD_DIM:(h + 1) * HEAD_DIM].astype(BF16)
        vh = v3[:, h * HEAD_DIM:(h + 1) * HEAD_DIM].astype(BF16)
        for g in range(ATTN_GROUP):
            hd = h * ATTN_GROUP + g
            qh = (q[:, hd * HEAD_DIM:(hd + 1) * HEAD_DIM] * (HEAD_DIM ** -0.5)).astype(BF16)
            s = lax.dot_general(qh, kh, _NT, preferred_element_type=F32)
            s = s + bias_ref[hd] + edge
            sk = sink_ref[hd]
            m = jnp.maximum(jnp.max(s, axis=-1, keepdims=True), sk)
            p = jnp.exp(s - m)
            den = jnp.sum(p, axis=-1, keepdims=True) + jnp.exp(sk - m)
            o = jnp.dot(p.astype(BF16), vh, preferred_element_type=F32)
            outs.append(o / den)
    o_ref[0] = jnp.concatenate(outs, axis=-1).astype(o_ref.dtype)


def _attn_bias():
    L = ATTN_BLOCK
    k_rel = jnp.arange(3 * L) - L
    dist = jnp.abs(k_rel[None, :] - jnp.arange(L)[:, None]).astype(F32)
    slopes = 2.0 ** (-8.0 * jnp.arange(1, N_ATTN_HEADS + 1, dtype=F32) / N_ATTN_HEADS)
    bias = -slopes[:, None, None] * dist[None]
    return jnp.where(dist[None] <= WINDOW, bias, NEG_BIG)


def attention(proj, sink):
    B, S, _ = proj.shape
    L = ATTN_BLOCK
    nb = S // L
    kv = lambda col, shift: pl.BlockSpec(
        (1, L, LANES), lambda b, n: (b, jnp.clip(n + shift, 0, nb - 1), col))
    return pl.pallas_call(
        _attn_kernel,
        grid=(B, nb),
        in_specs=[pl.BlockSpec(memory_space=pltpu.SMEM),
                  pl.BlockSpec((1, L, W_GROUP), lambda b, n: (b, n, COL_AQ // 4)),
                  kv(COL_AK, -1), kv(COL_AK, 0), kv(COL_AK, 1),
                  kv(COL_AV, -1), kv(COL_AV, 0), kv(COL_AV, 1),
                  pl.BlockSpec((N_ATTN_HEADS, L, 3 * L), lambda b, n: (0, 0, 0))],
        out_specs=pl.BlockSpec((1, L, W_GROUP), lambda b, n: (b, n, 0)),
        out_shape=jax.ShapeDtypeStruct((B, S, W_GROUP), BF16),
        compiler_params=_params(("parallel", "arbitrary")),
        name="attention",
    )(sink.astype(F32), proj, proj, proj, proj, proj, proj, proj, _attn_bias())


CONV_ROWS = 512
HALO = 8


def _conv_kernel(b_ref, c_ref, h_ref, w_ref, o_ref, u_ref):
    S = b_ref.shape[1]
    R = min(CONV_ROWS, S)
    u_ref[0:HALO, :] = jnp.zeros((HALO, LANES), F32)
    u_ref[S + HALO:S + 2 * HALO, :] = jnp.zeros((HALO, LANES), F32)

    def gate(i, carry):
        r = pl.multiple_of(i * R, R)
        u_ref[pl.ds(r + HALO, R), :] = c_ref[0, pl.ds(r, R), :] * h_ref[0, pl.ds(r, R), :]
        return carry

    lax.fori_loop(0, S // R, gate, 0)
    w0, w1, w2 = w_ref[0:1, :], w_ref[1:2, :], w_ref[2:3, :]

    def conv(i, carry):
        r = pl.multiple_of(i * R, R)
        a = u_ref[pl.ds(r, R + 2 * HALO), :]
        prev = pltpu.roll(a, 1, 0)[HALO:HALO + R]
        nxt = pltpu.roll(a, R + 2 * HALO - 1, 0)[HALO:HALO + R]
        y = w0 * prev + w1 * a[HALO:HALO + R] + w2 * nxt
        o_ref[0, pl.ds(r, R), :] = (b_ref[0, pl.ds(r, R), :] * y).astype(o_ref.dtype)
        return carry

    lax.fori_loop(0, S // R, conv, 0)


def short_conv(proj, conv_w):
    B, S, _ = proj.shape
    nj = W_GROUP // LANES
    col = lambda c: pl.BlockSpec((1, S, LANES), lambda b, j: (b, 0, c + j))
    return pl.pallas_call(
        _conv_kernel,
        grid=(B, nj),
        in_specs=[col(COL_CB), col(COL_CC), col(COL_CH),
                  pl.BlockSpec((3, LANES), lambda b, j: (0, j))],
        out_specs=pl.BlockSpec((1, S, LANES), lambda b, j: (b, 0, j)),
        out_shape=jax.ShapeDtypeStruct((B, S, W_GROUP), BF16),
        scratch_shapes=[pltpu.VMEM((S + 2 * HALO, LANES), F32)],
        compiler_params=_params(("parallel", "parallel")),
        name="short_conv",
    )(proj, proj, proj, conv_w.astype(F32))


def _level_codes():
    t = lax.broadcasted_iota(jnp.int32, (CHUNK, CHUNK), 0)
    s = lax.broadcasted_iota(jnp.int32, (CHUNK, CHUNK), 1)
    x = t ^ s
    hb = jnp.zeros((CHUNK, CHUNK), jnp.int32)
    c = 1
    while c < CHUNK:
        hb = jnp.where((x & c) != 0, c, hb)
        c *= 2
    diag = jnp.where(t == s, 0, -1)
    return jnp.where(t > s, hb, diag), jnp.where(t < s, hb, diag)


def _hgrn_chunk(q, v, z, lb, code, reverse):
    e = jnp.exp(-jnp.abs(z))
    r = 1.0 / (1.0 + e)
    er = e * r
    nonneg = z >= 0
    f = lb + (1.0 - lb) * jnp.where(nonneg, r, er)
    k = (1.0 - lb) * jnp.where(nonneg, er, r)
    row = lax.broadcasted_iota(jnp.int32, (CHUNK, REC_DIM), 0)
    qs = f
    ks = jnp.ones_like(f)
    blk = f
    a = jnp.where(code == 0,
                  lax.dot_general(q.astype(BF16), k.astype(BF16), _NT, preferred_element_type=F32),
                  0.0)
    c = 1
    while c < CHUNK:
        p = lax.dot_general((q * qs).astype(BF16), (k * ks).astype(BF16), _NT,
                            preferred_element_type=F32)
        a = jnp.where(code == c, p, a)
        upper = (row & c) != 0
        sib = jnp.where(upper, pltpu.roll(blk, c, 0), pltpu.roll(blk, CHUNK - c, 0))
        grow_q = jnp.logical_not(upper) if reverse else upper
        qs = qs * jnp.where(grow_q, sib, 1.0)
        ks = ks * jnp.where(grow_q, 1.0, sib)
        blk = blk * sib
        c *= 2
    return a, q * qs, k * ks, blk[0:1, :]


def _hgrn_kernel(q_ref, zf_ref, zb_ref, i_ref, g_ref, lb_ref, ng_ref, o_ref, acc_ref, code_ref):
    S = q_ref.shape[1]
    nc = S // CHUNK
    lb = lb_ref[0]
    cf, cb = _level_codes()
    code_ref[0] = cf
    code_ref[1] = cb

    def sweep(z_ref, reverse, finish):
        def body(j, st):
            ci = (nc - 1 - j) if reverse else j
            r0 = pl.multiple_of(ci * CHUNK, CHUNK)
            rows = pl.ds(r0, CHUNK)
            q = q_ref[0, rows, :]
            v = i_ref[0, rows, :]
            vb = v.astype(BF16)
            a, qd, kd, dec = _hgrn_chunk(q, v, z_ref[0, rows, :], lb,
                                         code_ref[1 if reverse else 0], reverse)
            o = jnp.dot(a.astype(BF16), vb, preferred_element_type=F32)
            o = o + lax.dot_general(qd.astype(BF16), st.astype(BF16), _NT,
                                    preferred_element_type=F32)
            finish(rows, o)
            return st * dec + lax.dot_general(vb, kd.astype(BF16), _TN, preferred_element_type=F32)

        lax.fori_loop(0, nc, body, jnp.zeros((REC_DIM, REC_DIM), F32))

    def store_fwd(rows, o):
        acc_ref[rows, :] = o

    def store_out(rows, o):
        o = o + acc_ref[rows, :]
        y = o * lax.rsqrt(jnp.mean(o * o, axis=-1, keepdims=True) + HEAD_NORM_EPS)
        y = y * ng_ref[0] * _silu(g_ref[0, rows, :])
        o_ref[0, rows, :] = y.astype(o_ref.dtype)

    sweep(zf_ref, False, store_fwd)
    sweep(zb_ref, True, store_out)


def hgrn2(proj, lb, norm_g):
    B, S, _ = proj.shape
    col = lambda c: pl.BlockSpec((1, S, LANES), lambda b, h: (b, 0, c + h))
    vec = pl.BlockSpec((1, 1, LANES), lambda b, h: (h, 0, 0))
    return pl.pallas_call(
        _hgrn_kernel,
        grid=(B, REC_HEADS),
        in_specs=[col(COL_GQ), col(COL_GZF), col(COL_GZB), col(COL_GI), col(COL_GO), vec, vec],
        out_specs=pl.BlockSpec((1, S, LANES), lambda b, h: (b, 0, h)),
        out_shape=jax.ShapeDtypeStruct((B, S, W_GROUP), BF16),
        scratch_shapes=[pltpu.VMEM((S, REC_DIM), F32),
                        pltpu.VMEM((2, CHUNK, CHUNK), jnp.int32)],
        compiler_params=_params(("parallel", "parallel")),
        name="hgrn2",
    )(proj, proj, proj, proj, proj,
      lb.astype(F32).reshape(REC_HEADS, 1, LANES), norm_g.astype(F32).reshape(REC_HEADS, 1, LANES))


def _retention_kernel(dl_ref, q_ref, k_ref, v_ref, g_ref, ng_ref, o_ref, acc_ref):
    S = q_ref.shape[1]
    nc = S // CHUNK
    h = pl.program_id(1)
    scale = REC_DIM ** -0.5

    def log_gamma(d):
        x = jnp.full((1, LANES), dl_ref[d, h], F32)
        return jnp.minimum(x, 0.0) - jnp.log1p(jnp.exp(-jnp.abs(x)))

    lgf, lgb = log_gamma(0), log_gamma(1)
    t = lax.broadcasted_iota(jnp.int32, (CHUNK, CHUNK), 0)
    s = lax.broadcasted_iota(jnp.int32, (CHUNK, CHUNK), 1)
    rel = (t - s).astype(F32)
    dsum = (jnp.where(t >= s, jnp.exp(lgf * rel), 0.0)
            + jnp.where(s >= t, jnp.exp(-lgb * rel), 0.0)) * scale
    pos = lax.broadcasted_iota(jnp.int32, (CHUNK, REC_DIM), 0).astype(F32)
    qdec_f = jnp.exp(lgf * (pos + 1.0))
    kdec_f = jnp.exp(lgf * (CHUNK - 1.0 - pos)) * scale
    qdec_b = jnp.exp(lgb * (CHUNK - pos))
    kdec_b = jnp.exp(lgb * pos) * scale
    cd_f = jnp.exp(lgf * CHUNK)
    cd_b = jnp.exp(lgb * CHUNK)

    def fwd(ci, st):
        rows = pl.ds(pl.multiple_of(ci * CHUNK, CHUNK), CHUNK)
        q, k = q_ref[0, rows, :], k_ref[0, rows, :]
        vb = v_ref[0, rows, :].astype(BF16)
        a = lax.dot_general(q.astype(BF16), k.astype(BF16), _NT, preferred_element_type=F32)
        o = jnp.dot((a * dsum).astype(BF16), vb, preferred_element_type=F32)
        o = o + lax.dot_general((q * qdec_f).astype(BF16), st.astype(BF16), _NT,
                                preferred_element_type=F32)
        acc_ref[rows, :] = o
        return st * cd_f + lax.dot_general(vb, (k * kdec_f).astype(BF16), _TN,
                                           preferred_element_type=F32)

    lax.fori_loop(0, nc, fwd, jnp.zeros((REC_DIM, REC_DIM), F32))

    def bwd(j, st):
        rows = pl.ds(pl.multiple_of((nc - 1 - j) * CHUNK, CHUNK), CHUNK)
        q, k = q_ref[0, rows, :], k_ref[0, rows, :]
        vb = v_ref[0, rows, :].astype(BF16)
        o = acc_ref[rows, :] + lax.dot_general((q * qdec_b).astype(BF16), st.astype(BF16), _NT,
                                               preferred_element_type=F32)
        oc = o - jnp.mean(o, axis=-1, keepdims=True)
        y = oc * lax.rsqrt(jnp.mean(oc * oc, axis=-1, keepdims=True) + HEAD_NORM_EPS)
        o_ref[0, rows, :] = (y * ng_ref[0] * _silu(g_ref[0, rows, :])).astype(o_ref.dtype)
        return st * cd_b + lax.dot_general(vb, (k * kdec_b).astype(BF16), _TN,
                                           preferred_element_type=F32)

    lax.fori_loop(0, nc, bwd, jnp.zeros((REC_DIM, REC_DIM), F32))


def retention(proj, decay_logit, norm_g):
    B, S, _ = proj.shape
    col = lambda c: pl.BlockSpec((1, S, LANES), lambda b, h: (b, 0, c + h))
    return pl.pallas_call(
        _retention_kernel,
        grid=(B, REC_HEADS),
        in_specs=[pl.BlockSpec(memory_space=pltpu.SMEM),
                  col(COL_RQ), col(COL_RK), col(COL_RV), col(COL_RG),
                  pl.BlockSpec((1, 1, LANES), lambda b, h: (h, 0, 0))],
        out_specs=pl.BlockSpec((1, S, LANES), lambda b, h: (b, 0, h)),
        out_shape=jax.ShapeDtypeStruct((B, S, W_GROUP), BF16),
        scratch_shapes=[pltpu.VMEM((S, REC_DIM), F32)],
        compiler_params=_params(("parallel", "parallel")),
        name="retention",
    )(decay_logit.astype(F32), proj, proj, proj, proj,
      norm_g.astype(F32).reshape(REC_HEADS, 1, LANES))


ROUTE_ROWS = 8


def _first_max(vals):
    best, idx = vals[0], jnp.zeros(vals[0].shape, jnp.int32)
    for i in range(1, len(vals)):
        take = vals[i] > best
        best = jnp.where(take, vals[i], best)
        idx = jnp.where(take, i, idx)
    return best, idx


def _pick(idx, vals):
    out = vals[-1]
    for i in range(len(vals) - 2, -1, -1):
        out = jnp.where(idx == i, vals[i], out)
    return out


def _out_proj_kernel(ya_ref, yc_ref, yg_ref, yr_ref, w_ref, h_ref, g_ref, b_ref, rw_ref, rb_ref,
                     h1_ref, ri_ref, rwt_ref, cnt_ref, carry_ref):
    tm = h_ref.shape[0]

    @pl.when(pl.program_id(0) == 0)
    def _():
        carry_ref[...] = jnp.zeros_like(carry_ref)

    mix = jnp.dot(ya_ref[...], w_ref[0:W_GROUP, :], preferred_element_type=F32)
    mix += jnp.dot(yc_ref[...], w_ref[W_GROUP:2 * W_GROUP, :], preferred_element_type=F32)
    mix += jnp.dot(yg_ref[...], w_ref[2 * W_GROUP:3 * W_GROUP, :], preferred_element_type=F32)
    mix += jnp.dot(yr_ref[...], w_ref[3 * W_GROUP:4 * W_GROUP, :], preferred_element_type=F32)
    h1 = _layer_norm(DEEPNORM_ALPHA * h_ref[...] + mix, g_ref[...], b_ref[...])
    h1_ref[...] = h1

    logits = lax.dot_general(rw_ref[...], h1, _NT, preferred_element_type=F32,
                             precision=lax.Precision.HIGHEST) + rb_ref[...]
    rows = [logits[e:e + 1, :] for e in range(N_EXPERTS)]
    mx = functools.reduce(jnp.maximum, rows)
    ex = [jnp.exp(r - mx) for r in rows]
    den = functools.reduce(jnp.add, ex)
    pr = [x / den for x in ex]
    scores = []
    for g in range(N_GROUPS):
        a, b, c, d = pr[4 * g:4 * g + 4]
        hi1, lo1, hi2, lo2 = jnp.maximum(a, b), jnp.minimum(a, b), jnp.maximum(c, d), jnp.minimum(c, d)
        scores.append(jnp.maximum(hi1, hi2) + jnp.maximum(jnp.minimum(hi1, hi2), jnp.maximum(lo1, lo2)))
    _, gsel = _first_max(scores)
    cand = [_pick(gsel, [pr[4 * g + i] for g in range(N_GROUPS)]) for i in range(EXPERTS_PER_GROUP)]
    p0, i0 = _first_max(cand)
    p1, i1 = _first_max([jnp.where(i0 == i, -1.0, cand[i]) for i in range(EXPERTS_PER_GROUP)])
    e0 = gsel * EXPERTS_PER_GROUP + i0
    e1 = gsel * EXPERTS_PER_GROUP + i1
    tot = p0 + p1
    w0, w1 = p0 / tot, p1 / tot

    ind = jnp.concatenate([((e0 == e) | (e1 == e)).astype(F32) for e in range(N_EXPERTS)], axis=0)
    before = (lax.broadcasted_iota(jnp.int32, (tm, tm), 0)
              < lax.broadcasted_iota(jnp.int32, (tm, tm), 1)).astype(BF16)
    rank = jnp.dot(ind.astype(BF16), before, preferred_element_type=F32) + carry_ref[...]
    carry_ref[...] = carry_ref[...] + jnp.sum(ind, axis=-1, keepdims=True)
    rk = [rank[e:e + 1, :] for e in range(N_EXPERTS)]
    r0 = _pick(e0, rk).astype(jnp.int32)
    r1 = _pick(e1, rk).astype(jnp.int32)
    zi = jnp.zeros((ROUTE_ROWS - 4, tm), jnp.int32)
    ri_ref[...] = jnp.concatenate([e0, e1, r0, r1, zi], axis=0)
    rwt_ref[...] = jnp.concatenate([w0, w1, jnp.zeros((ROUTE_ROWS - 2, tm), F32)], axis=0)
    cnt_ref[...] = jnp.broadcast_to(carry_ref[...], cnt_ref.shape)


def out_proj_ln_route(ys, w_out, h, g, b, router_w, router_b, tm=256):
    T, D = h.shape
    tm = min(tm, T)
    part = pl.BlockSpec((tm, W_GROUP), lambda i: (i, 0))
    row = pl.BlockSpec((tm, D), lambda i: (i, 0))
    vec = pl.BlockSpec((1, D), lambda i: (0, 0))
    route = pl.BlockSpec((ROUTE_ROWS, tm), lambda i: (0, i))
    return pl.pallas_call(
        _out_proj_kernel,
        grid=(T // tm,),
        in_specs=[part, part, part, part,
                  pl.BlockSpec((D, D), lambda i: (0, 0)),
                  row, vec, vec,
                  pl.BlockSpec((N_EXPERTS, D), lambda i: (0, 0)),
                  pl.BlockSpec((N_EXPERTS, 1), lambda i: (0, 0))],
        out_specs=[row, route, route, pl.BlockSpec((N_EXPERTS, LANES), lambda i: (0, 0))],
        out_shape=[jax.ShapeDtypeStruct((T, D), F32),
                   jax.ShapeDtypeStruct((ROUTE_ROWS, T), jnp.int32),
                   jax.ShapeDtypeStruct((ROUTE_ROWS, T), F32),
                   jax.ShapeDtypeStruct((N_EXPERTS, LANES), F32)],
        scratch_shapes=[pltpu.VMEM((N_EXPERTS, 1), F32)],
        compiler_params=_params(("arbitrary",)),
        name="out_proj_ln_route",
    )(*[y.reshape(T, W_GROUP) for y in ys], w_out, h, g.reshape(1, D), b.reshape(1, D),
      router_w.astype(F32).T, router_b.astype(F32).reshape(N_EXPERTS, 1))


def _dispatch_kernel(pad_start_ref, pad_len_ref, nv_ref, pos_ref, h_ref, x_hbm, zero_ref, sem, zsem):
    tm = h_ref.shape[0]
    tile = zero_ref.shape[0]
    n_tiles = x_hbm.shape[0] // tile

    def row_copy(r, p):
        return pltpu.make_async_copy(h_ref.at[pl.ds(r, 1), :], x_hbm.at[pl.ds(p, 1), :], sem)

    def issue(r, carry):
        row_copy(r, pos_ref[0, 0, r]).start()
        row_copy(r, pos_ref[0, 0, tm + r]).start()
        return carry

    lax.fori_loop(0, tm, issue, 0)

    @pl.when(pl.program_id(0) == 0)
    def _():
        zero_ref[...] = jnp.zeros_like(zero_ref)

        def tcopy(i):
            return pltpu.make_async_copy(
                zero_ref, x_hbm.at[pl.ds(pl.multiple_of(i * tile, tile), tile), :], zsem)

        def tissue(i, carry):
            tcopy(i).start()
            return carry

        def twait(i, carry):
            tcopy(i).wait()
            return carry

        lax.fori_loop(nv_ref[0], n_tiles, tissue, 0)
        lax.fori_loop(nv_ref[0], n_tiles, twait, 0)
        for e in range(N_EXPERTS):
            def zcopy(i):
                return pltpu.make_async_copy(
                    zero_ref.at[pl.ds(0, 1), :], x_hbm.at[pl.ds(pad_start_ref[e] + i, 1), :], zsem)

            def zissue(i, carry):
                zcopy(i).start()
                return carry

            def zwait(i, carry):
                zcopy(i).wait()
                return carry

            lax.fori_loop(0, pad_len_ref[e], zissue, 0)
            lax.fori_loop(0, pad_len_ref[e], zwait, 0)

    def wait(r, carry):
        row_copy(r, 0).wait()
        row_copy(r, 0).wait()
        return carry

    lax.fori_loop(0, tm, wait, 0)


def dispatch(h1, pos, pad_start, pad_len, n_valid, n_rows, tile, tm=256):
    T, D = h1.shape
    tm = min(tm, T)
    return pl.pallas_call(
        _dispatch_kernel,
        grid_spec=pltpu.PrefetchScalarGridSpec(
            num_scalar_prefetch=3,
            grid=(T // tm,),
            in_specs=[pl.BlockSpec((1, 1, 2 * tm), lambda i, *_: (i, 0, 0), memory_space=pltpu.SMEM),
                      pl.BlockSpec((tm, D), lambda i, *_: (i, 0))],
            out_specs=pl.BlockSpec(memory_space=pl.ANY),
            scratch_shapes=[pltpu.VMEM((tile, D), F32), pltpu.SemaphoreType.DMA,
                            pltpu.SemaphoreType.DMA]),
        out_shape=jax.ShapeDtypeStruct((n_rows, D), F32),
        compiler_params=_params(("arbitrary",)),
        name="dispatch",
    )(pad_start, pad_len, n_valid, pos, h1)


def _expert_kernel(te_ref, nv_ref, x_ref, wg_ref, wu_ref, wd_ref, y_ref):
    @pl.when(pl.program_id(0) < nv_ref[0])
    def _():
        x = x_ref[...].astype(BF16)
        gate = jnp.dot(x, wg_ref[0], preferred_element_type=F32)
        up = jnp.dot(x, wu_ref[0], preferred_element_type=F32)
        hid = (_silu(gate) * up).astype(BF16)
        y_ref[...] = jnp.dot(hid, wd_ref[0], preferred_element_type=F32)

    @pl.when(pl.program_id(0) >= nv_ref[0])
    def _():
        y_ref[...] = jnp.zeros_like(y_ref)


def expert_ffn(x_sorted, tile_expert, n_valid, w_gate, w_up, w_down, tm):
    A, D = x_sorted.shape
    n_tiles = A // tm
    tile = lambda i, te, nv: (jnp.minimum(i, nv[0] - 1), 0)
    out_tile = lambda i, te, nv: (i, 0)
    wmap = lambda i, te, nv: (te[jnp.minimum(i, nv[0] - 1)], 0, 0)
    return pl.pallas_call(
        _expert_kernel,
        grid_spec=pltpu.PrefetchScalarGridSpec(
            num_scalar_prefetch=2,
            grid=(n_tiles,),
            in_specs=[pl.BlockSpec((tm, D), tile),
                      pl.BlockSpec((1, D, D_EXPERT), wmap),
                      pl.BlockSpec((1, D, D_EXPERT), wmap),
                      pl.BlockSpec((1, D_EXPERT, D), wmap)],
            out_specs=pl.BlockSpec((tm, D), out_tile)),
        out_shape=jax.ShapeDtypeStruct((A, D), F32),
        compiler_params=_params(("arbitrary",)),
        name="expert_ffn",
    )(tile_expert, n_valid, x_sorted, w_gate, w_up, w_down)


def _combine_kernel(pos_ref, h_ref, w_ref, g_ref, b_ref, y_hbm, o_ref, ob_ref, buf_ref, sem):
    tm = h_ref.shape[0]

    def row_copy(r, p):
        return pltpu.make_async_copy(y_hbm.at[pl.ds(p, 1), :], buf_ref.at[pl.ds(r, 1), :], sem)

    def issue(r, carry):
        row_copy(r, pos_ref[0, 0, r]).start()
        row_copy(tm + r, pos_ref[0, 0, tm + r]).start()
        return carry

    lax.fori_loop(0, tm, issue, 0)

    def wait(r, carry):
        row_copy(0, 0).wait()
        row_copy(0, 0).wait()
        return carry

    lax.fori_loop(0, tm, wait, 0)
    w = w_ref[...]
    ffn = w[:, 0:1] * buf_ref[0:tm, :] + w[:, 1:2] * buf_ref[tm:2 * tm, :]
    h2 = _layer_norm(DEEPNORM_ALPHA * h_ref[...] + ffn, g_ref[...], b_ref[...])
    o_ref[...] = h2
    ob_ref[...] = h2.astype(BF16)


def combine_ln(h1, y_sorted, pos, wcol, g, b, tm=256):
    T, D = h1.shape
    tm = min(tm, T)
    row = pl.BlockSpec((tm, D), lambda i: (i, 0))
    vec = pl.BlockSpec((1, D), lambda i: (0, 0))
    return pl.pallas_call(
        _combine_kernel,
        grid=(T // tm,),
        in_specs=[pl.BlockSpec((1, 1, 2 * tm), lambda i: (i, 0, 0), memory_space=pltpu.SMEM),
                  row,
                  pl.BlockSpec((tm, 2), lambda i: (i, 0)),
                  vec, vec,
                  pl.BlockSpec(memory_space=pl.ANY)],
        out_specs=[row, row],
        out_shape=[jax.ShapeDtypeStruct((T, D), F32), jax.ShapeDtypeStruct((T, D), BF16)],
        scratch_shapes=[pltpu.VMEM((2 * tm, D), F32), pltpu.SemaphoreType.DMA],
        compiler_params=_params(("arbitrary",)),
        name="combine_ln",
    )(pos, h1, wcol, g.reshape(1, D), b.reshape(1, D), y_sorted)


EXPERT_TILE = 256
ROUTE_TILE = 256


def _routing_plan(route_i, counts, T, tile, route_tile):
    cnt = counts[:, 0].astype(jnp.int32)
    padded = ((cnt + tile - 1) // tile) * tile
    ends = jnp.cumsum(padded)
    offs = ends - padded
    e0, e1, r0, r1 = route_i[0], route_i[1], route_i[2], route_i[3]
    pos0 = offs[e0] + r0
    pos1 = offs[e1] + r1
    nrt = T // route_tile
    pos = jnp.concatenate([pos0.reshape(nrt, 1, route_tile), pos1.reshape(nrt, 1, route_tile)], axis=-1)
    n_tiles = (2 * T) // tile + N_EXPERTS
    tile_expert = jnp.minimum(
        jnp.searchsorted(ends // tile, jnp.arange(n_tiles, dtype=jnp.int32), side="right"),
        N_EXPERTS - 1).astype(jnp.int32)
    n_valid = (ends[-1] // tile).astype(jnp.int32).reshape(1)
    return pos, tile_expert, n_valid, (offs + cnt).astype(jnp.int32), (padded - cnt).astype(jnp.int32)


def _hgrn_lower_bounds(hgrn_lb):
    lb = jnp.cumsum(jax.nn.softmax(hgrn_lb.astype(F32), axis=0), axis=0)
    return lb - lb[0:1]


def kernel(x, emb_ln_g, emb_ln_b, w_in, attn_sink, conv_w, hgrn_lb, hgrn_norm_g, ret_decay_logit,
           ret_norm_g, w_out, ln1_g, ln1_b, router_w, router_b, w_gate, w_up, w_down, ln2_g, ln2_b):
    B, S, D = x.shape
    T = B * S
    depth = w_in.shape[0]
    lb_all = _hgrn_lower_bounds(hgrn_lb)
    route_tile = min(ROUTE_TILE, T)
    n_rows = 2 * T + N_EXPERTS * EXPERT_TILE

    h, hb = embed_ln(x.reshape(T, D), emb_ln_g, emb_ln_b)
    for l in range(depth):
        proj = in_proj(hb, w_in[l].astype(BF16)).reshape(B, S, D_IN_PROJ)
        ys = [attention(proj, attn_sink[l]),
              short_conv(proj, conv_w[l]),
              hgrn2(proj, lb_all[l], hgrn_norm_g[l]),
              retention(proj, ret_decay_logit[l], ret_norm_g[l])]
        h1, route_i, route_w, counts = out_proj_ln_route(
            ys, w_out[l].astype(BF16), h, ln1_g[l], ln1_b[l], router_w, router_b, tm=route_tile)
        pos, tile_expert, n_valid, pad_start, pad_len = _routing_plan(
            route_i, counts, T, EXPERT_TILE, route_tile)
        x_sorted = dispatch(h1, pos, pad_start, pad_len, n_valid, n_rows, EXPERT_TILE, tm=route_tile)
        y_sorted = expert_ffn(x_sorted, tile_expert, n_valid, w_gate[l].astype(BF16),
                              w_up[l].astype(BF16), w_down[l].astype(BF16), EXPERT_TILE)
        h, hb = combine_ln(h1, y_sorted, pos, route_w[0:2].T, ln2_g[l], ln2_b[l], tm=route_tile)
    return h.reshape(B, S, D)
```

```python
import functools

import jax
import jax.numpy as jnp
from jax import lax
from jax.experimental import pallas as pl
from jax.experimental.pallas import tpu as pltpu

F32 = jnp.float32
BF16 = jnp.bfloat16

D_MODEL = 2048
DEPTH = 2
W_GROUP = 512
HEAD_DIM = 64
N_ATTN_HEADS = 8
N_KV_HEADS = 2
ATTN_GROUP = N_ATTN_HEADS // N_KV_HEADS
WINDOW = 128
ATTN_BLOCK = 128
REC_HEADS = 4
REC_DIM = 128
N_EXPERTS = 16
N_GROUPS = 4
EXPERTS_PER_GROUP = 4
D_EXPERT = 1024
D_IN_PROJ = 6912
DEEPNORM_ALPHA = (2.0 * DEPTH) ** 0.25
LN_EPS = 1e-5
HEAD_NORM_EPS = 1e-6
NEG_BIG = -1e30

LANES = 128
COL_AQ, COL_AK, COL_AV = 0, 4, 5
COL_CB, COL_CC, COL_CH = 6, 10, 14
COL_GQ, COL_GZF, COL_GZB, COL_GI, COL_GO = 18, 22, 26, 30, 34
COL_RQ, COL_RK, COL_RV, COL_RG = 38, 42, 46, 50

CHUNK = 128
VMEM_LIMIT = 56 * 1024 * 1024

_NT = (((1,), (1,)), ((), ()))
_TN = (((0,), (0,)), ((), ()))


def _params(sem, vmem=VMEM_LIMIT):
    return pltpu.CompilerParams(dimension_semantics=sem, vmem_limit_bytes=vmem)


def _layer_norm(x, g, b):
    mu = jnp.mean(x, axis=-1, keepdims=True)
    xc = x - mu
    var = jnp.mean(xc * xc, axis=-1, keepdims=True)
    return xc * lax.rsqrt(var + LN_EPS) * g + b


def _silu(x):
    return x * (1.0 / (1.0 + jnp.exp(-x)))


def _embed_ln_kernel(x_ref, g_ref, b_ref, h_ref, hb_ref):
    h = _layer_norm(x_ref[...], g_ref[...], b_ref[...])
    h_ref[...] = h
    hb_ref[...] = h.astype(BF16)


def embed_ln(x2, g, b, tm=512):
    T, D = x2.shape
    return pl.pallas_call(
        _embed_ln_kernel,
        grid=(T // tm,),
        in_specs=[pl.BlockSpec((tm, D), lambda i: (i, 0)),
                  pl.BlockSpec((1, D), lambda i: (0, 0)),
                  pl.BlockSpec((1, D), lambda i: (0, 0))],
        out_specs=[pl.BlockSpec((tm, D), lambda i: (i, 0)),
                   pl.BlockSpec((tm, D), lambda i: (i, 0))],
        out_shape=[jax.ShapeDtypeStruct((T, D), F32), jax.ShapeDtypeStruct((T, D), BF16)],
        compiler_params=_params(("parallel",)),
        name="embed_ln",
    )(x2, g.reshape(1, D), b.reshape(1, D))


def _in_proj_kernel(x_ref, w_ref, o_ref, wb_ref):
    @pl.when(pl.program_id(1) == 0)
    def _():
        wb_ref[...] = w_ref[...].astype(BF16)

    o_ref[...] = jnp.dot(x_ref[...], wb_ref[...], preferred_element_type=F32)


def in_proj(hb, w_in, layer, tm=1024, tn=768):
    T, K = hb.shape
    N = w_in.shape[2]
    tm = min(tm, T)
    return pl.pallas_call(
        _in_proj_kernel,
        grid=(N // tn, T // tm),
        in_specs=[pl.BlockSpec((tm, K), lambda n, m: (m, 0)),
                  pl.BlockSpec((None, K, tn), lambda n, m: (layer, 0, n))],
        out_specs=pl.BlockSpec((tm, tn), lambda n, m: (m, n)),
        out_shape=jax.ShapeDtypeStruct((T, N), F32),
        scratch_shapes=[pltpu.VMEM((K, tn), BF16)],
        compiler_params=_params(("arbitrary", "arbitrary")),
        name="in_proj",
    )(hb, w_in)


def _attn_kernel(sink_ref, q_ref, kp_ref, kc_ref, kn_ref, vp_ref, vc_ref, vn_ref, bias_ref, o_ref):
    L = ATTN_BLOCK
    n = pl.program_id(1)
    nb = pl.num_programs(1)
    col = lax.broadcasted_iota(jnp.int32, (1, 3 * L), 1)
    valid = ((col >= L) | (n > 0)) & ((col < 2 * L) | (n < nb - 1))
    edge = jnp.where(valid, 0.0, NEG_BIG)
    q = q_ref[0]
    k3 = jnp.concatenate([kp_ref[0], kc_ref[0], kn_ref[0]], axis=0)
    v3 = jnp.concatenate([vp_ref[0], vc_ref[0], vn_ref[0]], axis=0)
    outs = []
    for h in range(N_KV_HEADS):
        kh = k3[:, h * HEAD_DIM:(h + 1) * HEAD_DIM].astype(BF16)
        vh = v3[:, h * HEAD_DIM:(h + 1) * HEAD_DIM].astype(BF16)
        for g in range(ATTN_GROUP):
            hd = h * ATTN_GROUP + g
            qh = (q[:, hd * HEAD_DIM:(hd + 1) * HEAD_DIM] * (HEAD_DIM ** -0.5)).astype(BF16)
            s = lax.dot_general(qh, kh, _NT, preferred_element_type=F32)
            s = s + bias_ref[hd] + edge
            sk = sink_ref[hd]
            m = jnp.maximum(jnp.max(s, axis=-1, keepdims=True), sk)
            p = jnp.exp(s - m)
            den = jnp.sum(p, axis=-1, keepdims=True) + jnp.exp(sk - m)
            o = jnp.dot(p.astype(BF16), vh, preferred_element_type=F32)
            outs.append(o / den)
    o_ref[0] = jnp.concatenate(outs, axis=-1).astype(o_ref.dtype)


def _attn_bias():
    L = ATTN_BLOCK
    k_rel = jnp.arange(3 * L) - L
    dist = jnp.abs(k_rel[None, :] - jnp.arange(L)[:, None]).astype(F32)
    slopes = 2.0 ** (-8.0 * jnp.arange(1, N_ATTN_HEADS + 1, dtype=F32) / N_ATTN_HEADS)
    bias = -slopes[:, None, None] * dist[None]
    return jnp.where(dist[None] <= WINDOW, bias, NEG_BIG)


def attention(proj, sink):
    B, S, _ = proj.shape
    L = ATTN_BLOCK
    nb = S // L
    kv = lambda col, shift: pl.BlockSpec(
        (1, L, LANES), lambda b, n: (b, jnp.clip(n + shift, 0, nb - 1), col))
    return pl.pallas_call(
        _attn_kernel,
        grid=(B, nb),
        in_specs=[pl.BlockSpec(memory_space=pltpu.SMEM),
                  pl.BlockSpec((1, L, W_GROUP), lambda b, n: (b, n, COL_AQ // 4)),
                  kv(COL_AK, -1), kv(COL_AK, 0), kv(COL_AK, 1),
                  kv(COL_AV, -1), kv(COL_AV, 0), kv(COL_AV, 1),
                  pl.BlockSpec((N_ATTN_HEADS, L, 3 * L), lambda b, n: (0, 0, 0))],
        out_specs=pl.BlockSpec((1, L, W_GROUP), lambda b, n: (b, n, 0)),
        out_shape=jax.ShapeDtypeStruct((B, S, W_GROUP), BF16),
        compiler_params=_params(("parallel", "arbitrary")),
        name="attention",
    )(sink.astype(F32), proj, proj, proj, proj, proj, proj, proj, _attn_bias())


CONV_ROWS = 512
HALO = 8


def _conv_kernel(b_ref, c_ref, h_ref, w_ref, o_ref, u_ref):
    S = b_ref.shape[1]
    R = min(CONV_ROWS, S)
    u_ref[0:HALO, :] = jnp.zeros((HALO, LANES), F32)
    u_ref[S + HALO:S + 2 * HALO, :] = jnp.zeros((HALO, LANES), F32)

    def gate(i, carry):
        r = pl.multiple_of(i * R, R)
        u_ref[pl.ds(r + HALO, R), :] = c_ref[0, pl.ds(r, R), :] * h_ref[0, pl.ds(r, R), :]
        return carry

    lax.fori_loop(0, S // R, gate, 0)
    w0, w1, w2 = w_ref[0:1, :], w_ref[1:2, :], w_ref[2:3, :]

    def conv(i, carry):
        r = pl.multiple_of(i * R, R)
        a = u_ref[pl.ds(r, R + 2 * HALO), :]
        prev = pltpu.roll(a, 1, 0)[HALO:HALO + R]
        nxt = pltpu.roll(a, R + 2 * HALO - 1, 0)[HALO:HALO + R]
        y = w0 * prev + w1 * a[HALO:HALO + R] + w2 * nxt
        o_ref[0, pl.ds(r, R), :] = (b_ref[0, pl.ds(r, R), :] * y).astype(o_ref.dtype)
        return carry

    lax.fori_loop(0, S // R, conv, 0)


def short_conv(proj, conv_w):
    B, S, _ = proj.shape
    nj = W_GROUP // LANES
    col = lambda c: pl.BlockSpec((1, S, LANES), lambda b, j: (b, 0, c + j))
    return pl.pallas_call(
        _conv_kernel,
        grid=(B, nj),
        in_specs=[col(COL_CB), col(COL_CC), col(COL_CH),
                  pl.BlockSpec((3, LANES), lambda b, j: (0, j))],
        out_specs=pl.BlockSpec((1, S, LANES), lambda b, j: (b, 0, j)),
        out_shape=jax.ShapeDtypeStruct((B, S, W_GROUP), BF16),
        scratch_shapes=[pltpu.VMEM((S + 2 * HALO, LANES), F32)],
        compiler_params=_params(("parallel", "parallel")),
        name="short_conv",
    )(proj, proj, proj, conv_w.astype(F32))


def _level_codes():
    t = lax.broadcasted_iota(jnp.int32, (CHUNK, CHUNK), 0)
    s = lax.broadcasted_iota(jnp.int32, (CHUNK, CHUNK), 1)
    x = t ^ s
    hb = jnp.zeros((CHUNK, CHUNK), jnp.int32)
    c = 1
    while c < CHUNK:
        hb = jnp.where((x & c) != 0, c, hb)
        c *= 2
    diag = jnp.where(t == s, 0, -1)
    return jnp.where(t > s, hb, diag), jnp.where(t < s, hb, diag)


def _hgrn_chunk(q, v, z, lb, code, reverse):
    e = jnp.exp(-jnp.abs(z))
    r = 1.0 / (1.0 + e)
    er = e * r
    nonneg = z >= 0
    f = lb + (1.0 - lb) * jnp.where(nonneg, r, er)
    k = (1.0 - lb) * jnp.where(nonneg, er, r)
    row = lax.broadcasted_iota(jnp.int32, (CHUNK, REC_DIM), 0)
    qs = f
    ks = jnp.ones_like(f)
    blk = f
    a = jnp.where(code == 0,
                  lax.dot_general(q.astype(BF16), k.astype(BF16), _NT, preferred_element_type=F32),
                  0.0)
    c = 1
    while c < CHUNK:
        p = lax.dot_general((q * qs).astype(BF16), (k * ks).astype(BF16), _NT,
                            preferred_element_type=F32)
        a = jnp.where(code == c, p, a)
        upper = (row & c) != 0
        sib = jnp.where(upper, pltpu.roll(blk, c, 0), pltpu.roll(blk, CHUNK - c, 0))
        grow_q = jnp.logical_not(upper) if reverse else upper
        qs = qs * jnp.where(grow_q, sib, 1.0)
        ks = ks * jnp.where(grow_q, 1.0, sib)
        blk = blk * sib
        c *= 2
    return a, q * qs, k * ks, blk[0:1, :]


def _hgrn_kernel(q_ref, zf_ref, zb_ref, i_ref, g_ref, lb_ref, ng_ref, o_ref, acc_ref, code_ref,
                 stf_ref, stb_ref):
    S = q_ref.shape[1]
    nc = S // CHUNK
    lb = lb_ref[0]
    cf, cb = _level_codes()
    code_ref[0] = cf
    code_ref[1] = cb
    stf_ref[...] = jnp.zeros_like(stf_ref)
    stb_ref[...] = jnp.zeros_like(stb_ref)

    def part(ci, reverse):
        z_ref, st = (zb_ref, stb_ref) if reverse else (zf_ref, stf_ref)
        rows = pl.ds(pl.multiple_of(ci * CHUNK, CHUNK), CHUNK)
        q = q_ref[0, rows, :]
        v = i_ref[0, rows, :]
        vb = v.astype(BF16)
        a, qd, kd, dec = _hgrn_chunk(q, v, z_ref[0, rows, :], lb,
                                     code_ref[1 if reverse else 0], reverse)
        o = jnp.dot(a.astype(BF16), vb, preferred_element_type=F32)
        o = o + lax.dot_general(qd.astype(BF16), st[...].astype(BF16), _NT,
                                preferred_element_type=F32)
        st[...] = st[...] * dec + lax.dot_general(vb, kd.astype(BF16), _TN,
                                                  preferred_element_type=F32)
        return rows, o

    def finish(rows, o):
        o = o + acc_ref[rows, :]
        y = o * lax.rsqrt(jnp.mean(o * o, axis=-1, keepdims=True) + HEAD_NORM_EPS)
        y = y * ng_ref[0] * _silu(g_ref[0, rows, :])
        o_ref[0, rows, :] = y.astype(o_ref.dtype)

    def first_half(j, carry):
        rows, o = part(j, False)
        acc_ref[rows, :] = o
        rows, o = part(nc - 1 - j, True)
        acc_ref[rows, :] = o
        return carry

    def second_half(j, carry):
        finish(*part(j, False))
        finish(*part(nc - 1 - j, True))
        return carry

    lax.fori_loop(0, nc // 2, first_half, 0)
    lax.fori_loop(nc // 2, nc, second_half, 0)


def hgrn2(proj, lb, norm_g):
    B, S, _ = proj.shape
    assert (S // CHUNK) % 2 == 0
    col = lambda c: pl.BlockSpec((1, S, LANES), lambda b, h: (b, 0, c + h))
    vec = pl.BlockSpec((1, 1, LANES), lambda b, h: (h, 0, 0))
    return pl.pallas_call(
        _hgrn_kernel,
        grid=(B, REC_HEADS),
        in_specs=[col(COL_GQ), col(COL_GZF), col(COL_GZB), col(COL_GI), col(COL_GO), vec, vec],
        out_specs=pl.BlockSpec((1, S, LANES), lambda b, h: (b, 0, h)),
        out_shape=jax.ShapeDtypeStruct((B, S, W_GROUP), BF16),
        scratch_shapes=[pltpu.VMEM((S, REC_DIM), F32),
                        pltpu.VMEM((2, CHUNK, CHUNK), jnp.int32),
                        pltpu.VMEM((REC_DIM, REC_DIM), F32),
                        pltpu.VMEM((REC_DIM, REC_DIM), F32)],
        compiler_params=_params(("parallel", "parallel")),
        name="hgrn2",
    )(proj, proj, proj, proj, proj,
      lb.astype(F32).reshape(REC_HEADS, 1, LANES), norm_g.astype(F32).reshape(REC_HEADS, 1, LANES))


RET_HEADS_PER_STEP = 2


def _retention_kernel(dl_ref, q_ref, k_ref, v_ref, g_ref, ng_ref, o_ref, const_ref, *scratch):
    S = q_ref.shape[1]
    nc = S // CHUNK
    nh = RET_HEADS_PER_STEP
    acc_refs, st_refs = scratch[:nh], scratch[nh:]
    scale = REC_DIM ** -0.5
    t = lax.broadcasted_iota(jnp.int32, (CHUNK, CHUNK), 0)
    s = lax.broadcasted_iota(jnp.int32, (CHUNK, CHUNK), 1)
    rel = (t - s).astype(F32)
    pos = lax.broadcasted_iota(jnp.int32, (CHUNK, REC_DIM), 0).astype(F32)
    chunk_decay = []
    for hh in range(nh):
        head = pl.program_id(1) * nh + hh

        def log_gamma(d):
            x = jnp.full((1, LANES), dl_ref[d, head], F32)
            return jnp.minimum(x, 0.0) - jnp.log1p(jnp.exp(-jnp.abs(x)))

        lgf, lgb = log_gamma(0), log_gamma(1)
        const_ref[hh, 0] = (jnp.where(t >= s, jnp.exp(lgf * rel), 0.0)
                            + jnp.where(s >= t, jnp.exp(-lgb * rel), 0.0)) * scale
        const_ref[hh, 1] = jnp.exp(lgf * (pos + 1.0))
        const_ref[hh, 2] = jnp.exp(lgf * (CHUNK - 1.0 - pos)) * scale
        const_ref[hh, 3] = jnp.exp(lgb * (CHUNK - pos))
        const_ref[hh, 4] = jnp.exp(lgb * pos) * scale
        chunk_decay.append((jnp.exp(lgf * CHUNK), jnp.exp(lgb * CHUNK)))
    for st in st_refs:
        st[...] = jnp.zeros_like(st)

    def load(ci, hh):
        rows = pl.ds(pl.multiple_of(ci * CHUNK, CHUNK), CHUNK)
        lanes = slice(hh * LANES, (hh + 1) * LANES)
        return rows, lanes, q_ref[0, rows, lanes], k_ref[0, rows, lanes], v_ref[0, rows, lanes].astype(BF16)

    def fwd_part(ci, hh):
        rows, lanes, q, k, vb = load(ci, hh)
        st = st_refs[2 * hh]
        a = lax.dot_general(q.astype(BF16), k.astype(BF16), _NT, preferred_element_type=F32)
        o = jnp.dot((a * const_ref[hh, 0]).astype(BF16), vb, preferred_element_type=F32)
        o = o + lax.dot_general((q * const_ref[hh, 1]).astype(BF16), st[...].astype(BF16), _NT,
                                preferred_element_type=F32)
        st[...] = st[...] * chunk_decay[hh][0] + lax.dot_general(
            vb, (k * const_ref[hh, 2]).astype(BF16), _TN, preferred_element_type=F32)
        return rows, lanes, o

    def bwd_part(ci, hh):
        rows, lanes, q, k, vb = load(ci, hh)
        st = st_refs[2 * hh + 1]
        o = lax.dot_general((q * const_ref[hh, 3]).astype(BF16), st[...].astype(BF16), _NT,
                            preferred_element_type=F32)
        st[...] = st[...] * chunk_decay[hh][1] + lax.dot_general(
            vb, (k * const_ref[hh, 4]).astype(BF16), _TN, preferred_element_type=F32)
        return rows, lanes, o

    def finish(rows, lanes, hh, o):
        o = o + acc_refs[hh][rows, :]
        oc = o - jnp.mean(o, axis=-1, keepdims=True)
        y = oc * lax.rsqrt(jnp.mean(oc * oc, axis=-1, keepdims=True) + HEAD_NORM_EPS)
        o_ref[0, rows, lanes] = (y * ng_ref[hh] * _silu(g_ref[0, rows, lanes])).astype(o_ref.dtype)

    def first_half(j, carry):
        for hh in range(nh):
            rows, _, o = fwd_part(j, hh)
            acc_refs[hh][rows, :] = o
            rows, _, o = bwd_part(nc - 1 - j, hh)
            acc_refs[hh][rows, :] = o
        return carry

    def second_half(j, carry):
        for hh in range(nh):
            rows, lanes, o = fwd_part(j, hh)
            finish(rows, lanes, hh, o)
            rows, lanes, o = bwd_part(nc - 1 - j, hh)
            finish(rows, lanes, hh, o)
        return carry

    lax.fori_loop(0, nc // 2, first_half, 0)
    lax.fori_loop(nc // 2, nc, second_half, 0)


def retention(proj, decay_logit, norm_g):
    B, S, _ = proj.shape
    nh = RET_HEADS_PER_STEP
    assert (S // CHUNK) % 2 == 0 and REC_HEADS % nh == 0
    col = lambda c: pl.BlockSpec((1, S, nh * LANES), lambda b, h: (b, 0, c // nh + h))
    return pl.pallas_call(
        _retention_kernel,
        grid=(B, REC_HEADS // nh),
        in_specs=[pl.BlockSpec(memory_space=pltpu.SMEM),
                  col(COL_RQ), col(COL_RK), col(COL_RV), col(COL_RG),
                  pl.BlockSpec((nh, 1, LANES), lambda b, h: (h, 0, 0))],
        out_specs=pl.BlockSpec((1, S, nh * LANES), lambda b, h: (b, 0, h)),
        out_shape=jax.ShapeDtypeStruct((B, S, W_GROUP), BF16),
        scratch_shapes=([pltpu.VMEM((nh, 5, CHUNK, REC_DIM), F32)]
                        + [pltpu.VMEM((S, REC_DIM), F32)] * nh
                        + [pltpu.VMEM((REC_DIM, REC_DIM), F32)] * (2 * nh)),
        compiler_params=_params(("parallel", "parallel")),
        name="retention",
    )(decay_logit.astype(F32), proj, proj, proj, proj,
      norm_g.astype(F32).reshape(REC_HEADS, 1, LANES))


ROUTE_ROWS = 8


def _first_max(vals):
    best, idx = vals[0], jnp.zeros(vals[0].shape, jnp.int32)
    for i in range(1, len(vals)):
        take = vals[i] > best
        best = jnp.where(take, vals[i], best)
        idx = jnp.where(take, i, idx)
    return best, idx


def _pick(idx, vals):
    out = vals[-1]
    for i in range(len(vals) - 2, -1, -1):
        out = jnp.where(idx == i, vals[i], out)
    return out


def _out_proj_kernel(ya_ref, yc_ref, yg_ref, yr_ref, w_ref, h_ref, g_ref, b_ref, rwh_ref, rwl_ref,
                     rb_ref, h1_ref, ri_ref, rwt_ref, cnt_ref, carry_ref, ycat_ref):
    tm = h_ref.shape[0]

    @pl.when(pl.program_id(0) == 0)
    def _():
        carry_ref[...] = jnp.zeros_like(carry_ref)

    ycat_ref[:, 0:W_GROUP] = ya_ref[...]
    ycat_ref[:, W_GROUP:2 * W_GROUP] = yc_ref[...]
    ycat_ref[:, 2 * W_GROUP:3 * W_GROUP] = yg_ref[...]
    ycat_ref[:, 3 * W_GROUP:4 * W_GROUP] = yr_ref[...]
    mix = jnp.dot(ycat_ref[...], w_ref[...], preferred_element_type=F32)
    h1 = _layer_norm(DEEPNORM_ALPHA * h_ref[...] + mix, g_ref[...], b_ref[...])
    h1_ref[...] = h1

    h1_hi = h1.astype(BF16)
    h1_lo = (h1 - h1_hi.astype(F32)).astype(BF16)
    logits = (lax.dot_general(rwh_ref[...], h1_hi, _NT, preferred_element_type=F32)
              + lax.dot_general(rwh_ref[...], h1_lo, _NT, preferred_element_type=F32)
              + lax.dot_general(rwl_ref[...], h1_hi, _NT, preferred_element_type=F32)
              + rb_ref[...])
    rows = [logits[e:e + 1, :] for e in range(N_EXPERTS)]
    mx = functools.reduce(jnp.maximum, rows)
    ex = [jnp.exp(r - mx) for r in rows]
    den = functools.reduce(jnp.add, ex)
    pr = [x / den for x in ex]
    scores = []
    for g in range(N_GROUPS):
        a, b, c, d = pr[4 * g:4 * g + 4]
        hi1, lo1, hi2, lo2 = jnp.maximum(a, b), jnp.minimum(a, b), jnp.maximum(c, d), jnp.minimum(c, d)
        scores.append(jnp.maximum(hi1, hi2) + jnp.maximum(jnp.minimum(hi1, hi2), jnp.maximum(lo1, lo2)))
    _, gsel = _first_max(scores)
    cand = [_pick(gsel, [pr[4 * g + i] for g in range(N_GROUPS)]) for i in range(EXPERTS_PER_GROUP)]
    p0, i0 = _first_max(cand)
    p1, i1 = _first_max([jnp.where(i0 == i, -1.0, cand[i]) for i in range(EXPERTS_PER_GROUP)])
    e0 = gsel * EXPERTS_PER_GROUP + i0
    e1 = gsel * EXPERTS_PER_GROUP + i1
    tot = p0 + p1
    w0, w1 = p0 / tot, p1 / tot

    ind = jnp.concatenate([((e0 == e) | (e1 == e)).astype(F32) for e in range(N_EXPERTS)], axis=0)
    before = (lax.broadcasted_iota(jnp.int32, (tm, tm), 0)
              < lax.broadcasted_iota(jnp.int32, (tm, tm), 1)).astype(BF16)
    rank = jnp.dot(ind.astype(BF16), before, preferred_element_type=F32) + carry_ref[...]
    carry_ref[...] = carry_ref[...] + jnp.sum(ind, axis=-1, keepdims=True)
    rk = [rank[e:e + 1, :] for e in range(N_EXPERTS)]
    r0 = _pick(e0, rk).astype(jnp.int32)
    r1 = _pick(e1, rk).astype(jnp.int32)
    zi = jnp.zeros((ROUTE_ROWS - 4, tm), jnp.int32)
    ri_ref[...] = jnp.concatenate([e0, e1, r0, r1, zi], axis=0)
    rwt_ref[...] = jnp.concatenate([w0, w1, jnp.zeros((ROUTE_ROWS - 2, tm), F32)], axis=0)
    cnt_ref[...] = jnp.broadcast_to(carry_ref[...], cnt_ref.shape)


def out_proj_ln_route(ys, w_out_bf16, layer, h, g, b, router_w, router_b, tm=512):
    T, D = h.shape
    tm = min(tm, T)
    part = pl.BlockSpec((tm, W_GROUP), lambda i: (i, 0))
    row = pl.BlockSpec((tm, D), lambda i: (i, 0))
    vec = pl.BlockSpec((1, D), lambda i: (0, 0))
    route = pl.BlockSpec((ROUTE_ROWS, tm), lambda i: (0, i))
    once = pl.Buffered(1)
    rw_t = router_w.astype(F32).T
    rw_hi = rw_t.astype(BF16)
    rw_lo = (rw_t - rw_hi.astype(F32)).astype(BF16)
    return pl.pallas_call(
        _out_proj_kernel,
        grid=(T // tm,),
        in_specs=[part, part, part, part,
                  pl.BlockSpec((None, D, D), lambda i: (layer, 0, 0), pipeline_mode=once),
                  row, vec, vec,
                  pl.BlockSpec((N_EXPERTS, D), lambda i: (0, 0), pipeline_mode=once),
                  pl.BlockSpec((N_EXPERTS, D), lambda i: (0, 0), pipeline_mode=once),
                  pl.BlockSpec((N_EXPERTS, 1), lambda i: (0, 0))],
        out_specs=[row, route, route, pl.BlockSpec((N_EXPERTS, LANES), lambda i: (0, 0))],
        out_shape=[jax.ShapeDtypeStruct((T, D), F32),
                   jax.ShapeDtypeStruct((ROUTE_ROWS, T), jnp.int32),
                   jax.ShapeDtypeStruct((ROUTE_ROWS, T), F32),
                   jax.ShapeDtypeStruct((N_EXPERTS, LANES), F32)],
        scratch_shapes=[pltpu.VMEM((N_EXPERTS, 1), F32), pltpu.VMEM((tm, D), BF16)],
        compiler_params=_params(("arbitrary",)),
        name="out_proj_ln_route",
    )(*[y.reshape(T, W_GROUP) for y in ys], w_out_bf16, h, g.reshape(1, D), b.reshape(1, D),
      rw_hi, rw_lo, router_b.astype(F32).reshape(N_EXPERTS, 1))


def _dispatch_kernel(pad_start_ref, pad_len_ref, nv_ref, pos_ref, h_ref, x_hbm, zero_ref, sem, zsem):
    tm = h_ref.shape[0]
    tile = zero_ref.shape[0]
    n_tiles = x_hbm.shape[0] // tile

    def row_copy(r, p):
        return pltpu.make_async_copy(h_ref.at[pl.ds(r, 1), :], x_hbm.at[pl.ds(p, 1), :], sem)

    def issue(r, carry):
        row_copy(r, pos_ref[0, 0, r]).start()
        row_copy(r, pos_ref[0, 0, tm + r]).start()
        return carry

    lax.fori_loop(0, tm, issue, 0)

    @pl.when(pl.program_id(0) == 0)
    def _():
        zero_ref[...] = jnp.zeros_like(zero_ref)

        def tcopy(i):
            return pltpu.make_async_copy(
                zero_ref, x_hbm.at[pl.ds(pl.multiple_of(i * tile, tile), tile), :], zsem)

        def tissue(i, carry):
            tcopy(i).start()
            return carry

        def twait(i, carry):
            tcopy(i).wait()
            return carry

        lax.fori_loop(nv_ref[0], n_tiles, tissue, 0)
        lax.fori_loop(nv_ref[0], n_tiles, twait, 0)
        for e in range(N_EXPERTS):
            def zcopy(i):
                return pltpu.make_async_copy(
                    zero_ref.at[pl.ds(0, 1), :], x_hbm.at[pl.ds(pad_start_ref[e] + i, 1), :], zsem)

            def zissue(i, carry):
                zcopy(i).start()
                return carry

            def zwait(i, carry):
                zcopy(i).wait()
                return carry

            lax.fori_loop(0, pad_len_ref[e], zissue, 0)
            lax.fori_loop(0, pad_len_ref[e], zwait, 0)

    def wait(r, carry):
        row_copy(r, 0).wait()
        row_copy(r, 0).wait()
        return carry

    lax.fori_loop(0, tm, wait, 0)


def dispatch(h1, pos, pad_start, pad_len, n_valid, n_rows, tile, tm=256):
    T, D = h1.shape
    tm = min(tm, T)
    return pl.pallas_call(
        _dispatch_kernel,
        grid_spec=pltpu.PrefetchScalarGridSpec(
            num_scalar_prefetch=3,
            grid=(T // tm,),
            in_specs=[pl.BlockSpec((1, 1, 2 * tm), lambda i, *_: (i, 0, 0), memory_space=pltpu.SMEM),
                      pl.BlockSpec((tm, D), lambda i, *_: (i, 0))],
            out_specs=pl.BlockSpec(memory_space=pl.ANY),
            scratch_shapes=[pltpu.VMEM((tile, D), F32), pltpu.SemaphoreType.DMA,
                            pltpu.SemaphoreType.DMA]),
        out_shape=jax.ShapeDtypeStruct((n_rows, D), F32),
        compiler_params=_params(("arbitrary",)),
        name="dispatch",
    )(pad_start, pad_len, n_valid, pos, h1)


def _expert_kernel(te_ref, nv_ref, x_ref, wg_ref, wu_ref, wd_ref, y_ref):
    @pl.when(pl.program_id(0) < nv_ref[0])
    def _():
        x = x_ref[...].astype(BF16)
        gate = jnp.dot(x, wg_ref[...], preferred_element_type=F32)
        up = jnp.dot(x, wu_ref[...], preferred_element_type=F32)
        hid = (_silu(gate) * up).astype(BF16)
        y_ref[...] = jnp.dot(hid, wd_ref[...], preferred_element_type=F32)

    @pl.when(pl.program_id(0) >= nv_ref[0])
    def _():
        y_ref[...] = jnp.zeros_like(y_ref)


def expert_ffn(x_sorted, tile_expert, n_valid, w_gate, w_up, w_down, layer, tm):
    A, D = x_sorted.shape
    n_tiles = A // tm
    tile = lambda i, te, nv: (jnp.minimum(i, nv[0] - 1), 0)
    out_tile = lambda i, te, nv: (i, 0)
    wmap = lambda i, te, nv: (layer, te[jnp.minimum(i, nv[0] - 1)], 0, 0)
    return pl.pallas_call(
        _expert_kernel,
        grid_spec=pltpu.PrefetchScalarGridSpec(
            num_scalar_prefetch=2,
            grid=(n_tiles,),
            in_specs=[pl.BlockSpec((tm, D), tile),
                      pl.BlockSpec((None, None, D, D_EXPERT), wmap),
                      pl.BlockSpec((None, None, D, D_EXPERT), wmap),
                      pl.BlockSpec((None, None, D_EXPERT, D), wmap)],
            out_specs=pl.BlockSpec((tm, D), out_tile)),
        out_shape=jax.ShapeDtypeStruct((A, D), F32),
        compiler_params=_params(("arbitrary",)),
        name="expert_ffn",
    )(tile_expert, n_valid, x_sorted, w_gate, w_up, w_down)


def _combine_kernel(pos_ref, h_ref, w_ref, g_ref, b_ref, y_hbm, o_ref, ob_ref, buf_ref, sem):
    tm = h_ref.shape[0]

    def row_copy(r, p):
        return pltpu.make_async_copy(y_hbm.at[pl.ds(p, 1), :], buf_ref.at[pl.ds(r, 1), :], sem)

    def issue(r, carry):
        row_copy(r, pos_ref[0, 0, r]).start()
        row_copy(tm + r, pos_ref[0, 0, tm + r]).start()
        return carry

    lax.fori_loop(0, tm, issue, 0)

    def wait(r, carry):
        row_copy(0, 0).wait()
        row_copy(0, 0).wait()
        return carry

    lax.fori_loop(0, tm, wait, 0)
    w = w_ref[...]
    ffn = w[:, 0:1] * buf_ref[0:tm, :] + w[:, 1:2] * buf_ref[tm:2 * tm, :]
    h2 = _layer_norm(DEEPNORM_ALPHA * h_ref[...] + ffn, g_ref[...], b_ref[...])
    o_ref[...] = h2
    ob_ref[...] = h2.astype(BF16)


def combine_ln(h1, y_sorted, pos, wcol, g, b, tm=256):
    T, D = h1.shape
    tm = min(tm, T)
    row = pl.BlockSpec((tm, D), lambda i: (i, 0))
    vec = pl.BlockSpec((1, D), lambda i: (0, 0))
    return pl.pallas_call(
        _combine_kernel,
        grid=(T // tm,),
        in_specs=[pl.BlockSpec((1, 1, 2 * tm), lambda i: (i, 0, 0), memory_space=pltpu.SMEM),
                  row,
                  pl.BlockSpec((tm, 2), lambda i: (i, 0)),
                  vec, vec,
                  pl.BlockSpec(memory_space=pl.ANY)],
        out_specs=[row, row],
        out_shape=[jax.ShapeDtypeStruct((T, D), F32), jax.ShapeDtypeStruct((T, D), BF16)],
        scratch_shapes=[pltpu.VMEM((2 * tm, D), F32), pltpu.SemaphoreType.DMA],
        compiler_params=_params(("arbitrary",)),
        name="combine_ln",
    )(pos, h1, wcol, g.reshape(1, D), b.reshape(1, D), y_sorted)


EXPERT_TILE = 256
ROUTE_TILE = 256


def _routing_plan(route_i, counts, T, tile, route_tile):
    cnt = counts[:, 0].astype(jnp.int32)
    padded = ((cnt + tile - 1) // tile) * tile
    ends = jnp.cumsum(padded)
    offs = ends - padded
    e0, e1, r0, r1 = route_i[0], route_i[1], route_i[2], route_i[3]
    pos0 = offs[e0] + r0
    pos1 = offs[e1] + r1
    nrt = T // route_tile
    pos = jnp.concatenate([pos0.reshape(nrt, 1, route_tile), pos1.reshape(nrt, 1, route_tile)], axis=-1)
    n_tiles = (2 * T) // tile + N_EXPERTS
    tile_ids = jnp.arange(n_tiles, dtype=jnp.int32)
    tile_expert = jnp.minimum(
        jnp.sum((ends[None, :] // tile <= tile_ids[:, None]).astype(jnp.int32), axis=1),
        N_EXPERTS - 1).astype(jnp.int32)
    n_valid = (ends[-1] // tile).astype(jnp.int32).reshape(1)
    return pos, tile_expert, n_valid, (offs + cnt).astype(jnp.int32), (padded - cnt).astype(jnp.int32)


def _hgrn_lower_bounds(hgrn_lb):
    lb = jnp.cumsum(jax.nn.softmax(hgrn_lb.astype(F32), axis=0), axis=0)
    return lb - lb[0:1]


def kernel(x, emb_ln_g, emb_ln_b, w_in, attn_sink, conv_w, hgrn_lb, hgrn_norm_g, ret_decay_logit,
           ret_norm_g, w_out, ln1_g, ln1_b, router_w, router_b, w_gate, w_up, w_down, ln2_g, ln2_b):
    B, S, D = x.shape
    T = B * S
    depth = w_in.shape[0]
    lb_all = _hgrn_lower_bounds(hgrn_lb)
    route_tile = min(ROUTE_TILE, T)
    n_rows = 2 * T + N_EXPERTS * EXPERT_TILE

    w_out_b, w_gate_b, w_up_b, w_down_b = (w.astype(BF16) for w in (w_out, w_gate, w_up, w_down))
    h, hb = embed_ln(x.reshape(T, D), emb_ln_g, emb_ln_b)
    for l in range(depth):
        proj = in_proj(hb, w_in, l).reshape(B, S, D_IN_PROJ)
        ys = [attention(proj, attn_sink[l]),
              short_conv(proj, conv_w[l]),
              hgrn2(proj, lb_all[l], hgrn_norm_g[l]),
              retention(proj, ret_decay_logit[l], ret_norm_g[l])]
        h1, route_i, route_w, counts = out_proj_ln_route(
            ys, w_out_b, l, h, ln1_g[l], ln1_b[l], router_w, router_b)
        pos, tile_expert, n_valid, pad_start, pad_len = _routing_plan(
            route_i, counts, T, EXPERT_TILE, route_tile)
        x_sorted = dispatch(h1, pos, pad_start, pad_len, n_valid, n_rows, EXPERT_TILE, tm=route_tile)
        y_sorted = expert_ffn(x_sorted, tile_expert, n_valid, w_gate_b, w_up_b, w_down_b, l,
                              EXPERT_TILE)
        h, hb = combine_ln(h1, y_sorted, pos, route_w[0:2].T, ln2_g[l], ln2_b[l], tm=route_tile)
    return h.reshape(B, S, D)
```

```python
import functools

import jax
import jax.numpy as jnp
from jax import lax
from jax.experimental import pallas as pl
from jax.experimental.pallas import tpu as pltpu

F32 = jnp.float32
BF16 = jnp.bfloat16

D_MODEL = 2048
DEPTH = 2
W_GROUP = 512
HEAD_DIM = 64
N_ATTN_HEADS = 8
N_KV_HEADS = 2
ATTN_GROUP = N_ATTN_HEADS // N_KV_HEADS
WINDOW = 128
ATTN_BLOCK = 128
REC_HEADS = 4
REC_DIM = 128
N_EXPERTS = 16
N_GROUPS = 4
EXPERTS_PER_GROUP = 4
D_EXPERT = 1024
D_IN_PROJ = 6912
DEEPNORM_ALPHA = (2.0 * DEPTH) ** 0.25
LN_EPS = 1e-5
HEAD_NORM_EPS = 1e-6
NEG_BIG = -1e30

LANES = 128
SUBLANES = 8
COL_AQ, COL_AK, COL_AV = 0, 4, 5
COL_CB, COL_CC, COL_CH = 6, 10, 14
COL_GQ, COL_GZF, COL_GZB, COL_GI, COL_GO = 18, 22, 26, 30, 34
COL_RQ, COL_RK, COL_RV, COL_RG = 38, 42, 46, 50

CHUNK = 128
VMEM_LIMIT = 56 * 1024 * 1024

_NT = (((1,), (1,)), ((), ()))
_TN = (((0,), (0,)), ((), ()))


def _params(sem, vmem=VMEM_LIMIT):
    return pltpu.CompilerParams(dimension_semantics=sem, vmem_limit_bytes=vmem)


def _layer_norm(x, g, b):
    mu = jnp.mean(x, axis=-1, keepdims=True)
    xc = x - mu
    var = jnp.mean(xc * xc, axis=-1, keepdims=True)
    return xc * lax.rsqrt(var + LN_EPS) * g + b


def _silu(x):
    return x * (1.0 / (1.0 + jnp.exp(-x)))


def _pack_rows(x):
    n = x.shape[1] // 2
    hi = lax.bitcast_convert_type(x[:, :n].astype(BF16).astype(F32), jnp.uint32)
    lo = lax.bitcast_convert_type(x[:, n:].astype(BF16).astype(F32), jnp.uint32)
    return hi | (lo >> 16)


def _unpack_rows(w):
    hi = lax.bitcast_convert_type(w & jnp.uint32(0xFFFF0000), F32)
    lo = lax.bitcast_convert_type(w << 16, F32)
    return jnp.concatenate([hi, lo], axis=-1)


def _embed_ln_kernel(x_ref, g_ref, b_ref, h_ref, hb_ref):
    h = _layer_norm(x_ref[...], g_ref[...], b_ref[...])
    h_ref[...] = h
    hb_ref[...] = h.astype(BF16)


def embed_ln(x2, g, b, tm=512):
    T, D = x2.shape
    return pl.pallas_call(
        _embed_ln_kernel,
        grid=(T // tm,),
        in_specs=[pl.BlockSpec((tm, D), lambda i: (i, 0)),
                  pl.BlockSpec((1, D), lambda i: (0, 0)),
                  pl.BlockSpec((1, D), lambda i: (0, 0))],
        out_specs=[pl.BlockSpec((tm, D), lambda i: (i, 0)),
                   pl.BlockSpec((tm, D), lambda i: (i, 0))],
        out_shape=[jax.ShapeDtypeStruct((T, D), F32), jax.ShapeDtypeStruct((T, D), BF16)],
        compiler_params=_params(("parallel",)),
        name="embed_ln",
    )(x2, g.reshape(1, D), b.reshape(1, D))


def _in_proj_kernel(x_ref, w_ref, o_ref, wb_ref):
    @pl.when(pl.program_id(1) == 0)
    def _():
        wb_ref[...] = w_ref[...].astype(BF16)

    o_ref[...] = jnp.dot(x_ref[...], wb_ref[...], preferred_element_type=F32)


def in_proj(hb, w_in, layer, tm=1024, tn=768):
    T, K = hb.shape
    N = w_in.shape[2]
    tm = min(tm, T)
    return pl.pallas_call(
        _in_proj_kernel,
        grid=(N // tn, T // tm),
        in_specs=[pl.BlockSpec((tm, K), lambda n, m: (m, 0)),
                  pl.BlockSpec((None, K, tn), lambda n, m: (layer, 0, n))],
        out_specs=pl.BlockSpec((tm, tn), lambda n, m: (m, n)),
        out_shape=jax.ShapeDtypeStruct((T, N), F32),
        scratch_shapes=[pltpu.VMEM((K, tn), BF16)],
        compiler_params=_params(("arbitrary", "arbitrary")),
        name="in_proj",
    )(hb, w_in)


def _attn_kernel(sink_ref, q_ref, kp_ref, kc_ref, kn_ref, vp_ref, vc_ref, vn_ref, bias_ref, o_ref):
    L = ATTN_BLOCK
    n = pl.program_id(1)
    nb = pl.num_programs(1)
    col = lax.broadcasted_iota(jnp.int32, (1, 3 * L), 1)
    valid = ((col >= L) | (n > 0)) & ((col < 2 * L) | (n < nb - 1))
    edge = jnp.where(valid, 0.0, NEG_BIG)
    q = q_ref[0]
    k3 = jnp.concatenate([kp_ref[0], kc_ref[0], kn_ref[0]], axis=0)
    v3 = jnp.concatenate([vp_ref[0], vc_ref[0], vn_ref[0]], axis=0)
    outs = []
    for h in range(N_KV_HEADS):
        kh = k3[:, h * HEAD_DIM:(h + 1) * HEAD_DIM].astype(BF16)
        vh = v3[:, h * HEAD_DIM:(h + 1) * HEAD_DIM].astype(BF16)
        for g in range(ATTN_GROUP):
            hd = h * ATTN_GROUP + g
            qh = (q[:, hd * HEAD_DIM:(hd + 1) * HEAD_DIM] * (HEAD_DIM ** -0.5)).astype(BF16)
            s = lax.dot_general(qh, kh, _NT, preferred_element_type=F32)
            s = s + bias_ref[hd] + edge
            sk = sink_ref[hd]
            m = jnp.maximum(jnp.max(s, axis=-1, keepdims=True), sk)
            p = jnp.exp(s - m)
            den = jnp.sum(p, axis=-1, keepdims=True) + jnp.exp(sk - m)
            o = jnp.dot(p.astype(BF16), vh, preferred_element_type=F32)
            outs.append(o / den)
    o_ref[0] = jnp.concatenate(outs, axis=-1).astype(o_ref.dtype)


def _attn_bias():
    L = ATTN_BLOCK
    k_rel = jnp.arange(3 * L) - L
    dist = jnp.abs(k_rel[None, :] - jnp.arange(L)[:, None]).astype(F32)
    slopes = 2.0 ** (-8.0 * jnp.arange(1, N_ATTN_HEADS + 1, dtype=F32) / N_ATTN_HEADS)
    bias = -slopes[:, None, None] * dist[None]
    return jnp.where(dist[None] <= WINDOW, bias, NEG_BIG)


def attention(proj, sink):
    B, S, _ = proj.shape
    L = ATTN_BLOCK
    nb = S // L
    kv = lambda col, shift: pl.BlockSpec(
        (1, L, LANES), lambda b, n: (b, jnp.clip(n + shift, 0, nb - 1), col))
    return pl.pallas_call(
        _attn_kernel,
        grid=(B, nb),
        in_specs=[pl.BlockSpec(memory_space=pltpu.SMEM),
                  pl.BlockSpec((1, L, W_GROUP), lambda b, n: (b, n, COL_AQ // 4)),
                  kv(COL_AK, -1), kv(COL_AK, 0), kv(COL_AK, 1),
                  kv(COL_AV, -1), kv(COL_AV, 0), kv(COL_AV, 1),
                  pl.BlockSpec((N_ATTN_HEADS, L, 3 * L), lambda b, n: (0, 0, 0))],
        out_specs=pl.BlockSpec((1, L, W_GROUP), lambda b, n: (b, n, 0)),
        out_shape=jax.ShapeDtypeStruct((B, S, W_GROUP), BF16),
        compiler_params=_params(("parallel", "arbitrary")),
        name="attention",
    )(sink.astype(F32), proj, proj, proj, proj, proj, proj, proj, _attn_bias())


CONV_ROWS = 512
HALO = 8


def _conv_kernel(b_ref, c_ref, h_ref, w_ref, o_ref, u_ref):
    S = b_ref.shape[1]
    R = min(CONV_ROWS, S)
    u_ref[0:HALO, :] = jnp.zeros((HALO, LANES), F32)
    u_ref[S + HALO:S + 2 * HALO, :] = jnp.zeros((HALO, LANES), F32)

    def gate(i, carry):
        r = pl.multiple_of(i * R, R)
        u_ref[pl.ds(r + HALO, R), :] = c_ref[0, pl.ds(r, R), :] * h_ref[0, pl.ds(r, R), :]
        return carry

    lax.fori_loop(0, S // R, gate, 0)
    w0, w1, w2 = w_ref[0:1, :], w_ref[1:2, :], w_ref[2:3, :]

    def conv(i, carry):
        r = pl.multiple_of(i * R, R)
        a = u_ref[pl.ds(r, R + 2 * HALO), :]
        prev = pltpu.roll(a, 1, 0)[HALO:HALO + R]
        nxt = pltpu.roll(a, R + 2 * HALO - 1, 0)[HALO:HALO + R]
        y = w0 * prev + w1 * a[HALO:HALO + R] + w2 * nxt
        o_ref[0, pl.ds(r, R), :] = (b_ref[0, pl.ds(r, R), :] * y).astype(o_ref.dtype)
        return carry

    lax.fori_loop(0, S // R, conv, 0)


def short_conv(proj, conv_w):
    B, S, _ = proj.shape
    nj = W_GROUP // LANES
    col = lambda c: pl.BlockSpec((1, S, LANES), lambda b, j: (b, 0, c + j))
    return pl.pallas_call(
        _conv_kernel,
        grid=(B, nj),
        in_specs=[col(COL_CB), col(COL_CC), col(COL_CH),
                  pl.BlockSpec((3, LANES), lambda b, j: (0, j))],
        out_specs=pl.BlockSpec((1, S, LANES), lambda b, j: (b, 0, j)),
        out_shape=jax.ShapeDtypeStruct((B, S, W_GROUP), BF16),
        scratch_shapes=[pltpu.VMEM((S + 2 * HALO, LANES), F32)],
        compiler_params=_params(("parallel", "parallel")),
        name="short_conv",
    )(proj, proj, proj, conv_w.astype(F32))


def _level_codes():
    t = lax.broadcasted_iota(jnp.int32, (CHUNK, CHUNK), 0)
    s = lax.broadcasted_iota(jnp.int32, (CHUNK, CHUNK), 1)
    x = t ^ s
    hb = jnp.zeros((CHUNK, CHUNK), jnp.int32)
    c = 1
    while c < CHUNK:
        hb = jnp.where((x & c) != 0, c, hb)
        c *= 2
    diag = jnp.where(t == s, 0, -1)
    return jnp.where(t > s, hb, diag), jnp.where(t < s, hb, diag)


def _hgrn_chunk(q, v, z, lb, code, reverse):
    e = jnp.exp(-jnp.abs(z))
    r = 1.0 / (1.0 + e)
    er = e * r
    nonneg = z >= 0
    f = lb + (1.0 - lb) * jnp.where(nonneg, r, er)
    k = (1.0 - lb) * jnp.where(nonneg, er, r)
    row = lax.broadcasted_iota(jnp.int32, (CHUNK, REC_DIM), 0)
    qs = f
    ks = jnp.ones_like(f)
    blk = f
    a = jnp.where(code == 0,
                  lax.dot_general(q.astype(BF16), k.astype(BF16), _NT, preferred_element_type=F32),
                  0.0)
    c = 1
    while c < CHUNK:
        p = lax.dot_general((q * qs).astype(BF16), (k * ks).astype(BF16), _NT,
                            preferred_element_type=F32)
        a = jnp.where(code == c, p, a)
        if c < SUBLANES:
            upper = (row & c) != 0
            grouped = blk.reshape(CHUNK // SUBLANES, SUBLANES, REC_DIM)
            down = pltpu.roll(grouped, c, 1).reshape(CHUNK, REC_DIM)
            up = (down if 2 * c == SUBLANES
                  else pltpu.roll(grouped, SUBLANES - c, 1).reshape(CHUNK, REC_DIM))
            sib = jnp.where(upper, down, up)
            grow_q = jnp.logical_not(upper) if reverse else upper
            qs = qs * jnp.where(grow_q, sib, 1.0)
            ks = ks * jnp.where(grow_q, 1.0, sib)
            blk = blk * sib
        else:
            step = c // SUBLANES
            group = lambda x, j: x[j * SUBLANES:(j + 1) * SUBLANES]
            nq, nk, nb = [], [], []
            for j in range(CHUNK // SUBLANES):
                upper = (j & step) != 0
                sib = group(blk, j ^ step)
                grow_q = (not upper) if reverse else upper
                nq.append(group(qs, j) * sib if grow_q else group(qs, j))
                nk.append(group(ks, j) if grow_q else group(ks, j) * sib)
                nb.append(group(blk, j) * sib)
            qs, ks, blk = (jnp.concatenate(x, axis=0) for x in (nq, nk, nb))
        c *= 2
    return a, q * qs, k * ks, blk[0:1, :]


def _hgrn_kernel(q_ref, zf_ref, zb_ref, i_ref, g_ref, lb_ref, ng_ref, o_ref, acc_ref, code_ref,
                 stf_ref, stb_ref):
    S = q_ref.shape[1]
    nc = S // CHUNK
    lb = lb_ref[0]
    cf, cb = _level_codes()
    code_ref[0] = cf
    code_ref[1] = cb
    stf_ref[...] = jnp.zeros_like(stf_ref)
    stb_ref[...] = jnp.zeros_like(stb_ref)

    def part(ci, reverse):
        z_ref, st = (zb_ref, stb_ref) if reverse else (zf_ref, stf_ref)
        rows = pl.ds(pl.multiple_of(ci * CHUNK, CHUNK), CHUNK)
        q = q_ref[0, rows, :]
        v = i_ref[0, rows, :]
        vb = v.astype(BF16)
        a, qd, kd, dec = _hgrn_chunk(q, v, z_ref[0, rows, :], lb,
                                     code_ref[1 if reverse else 0], reverse)
        o = jnp.dot(a.astype(BF16), vb, preferred_element_type=F32)
        o = o + lax.dot_general(qd.astype(BF16), st[...].astype(BF16), _NT,
                                preferred_element_type=F32)
        st[...] = st[...] * dec + lax.dot_general(vb, kd.astype(BF16), _TN,
                                                  preferred_element_type=F32)
        return rows, o

    def finish(rows, o):
        o = o + acc_ref[rows, :]
        y = o * lax.rsqrt(jnp.mean(o * o, axis=-1, keepdims=True) + HEAD_NORM_EPS)
        y = y * ng_ref[0] * _silu(g_ref[0, rows, :])
        o_ref[0, rows, :] = y.astype(o_ref.dtype)

    def first_half(j, carry):
        rows, o = part(j, False)
        acc_ref[rows, :] = o
        rows, o = part(nc - 1 - j, True)
        acc_ref[rows, :] = o
        return carry

    def second_half(j, carry):
        finish(*part(j, False))
        finish(*part(nc - 1 - j, True))
        return carry

    lax.fori_loop(0, nc // 2, first_half, 0)
    lax.fori_loop(nc // 2, nc, second_half, 0)


def hgrn2(proj, lb, norm_g):
    B, S, _ = proj.shape
    assert (S // CHUNK) % 2 == 0
    col = lambda c: pl.BlockSpec((1, S, LANES), lambda b, h: (b, 0, c + h))
    vec = pl.BlockSpec((1, 1, LANES), lambda b, h: (h, 0, 0))
    return pl.pallas_call(
        _hgrn_kernel,
        grid=(B, REC_HEADS),
        in_specs=[col(COL_GQ), col(COL_GZF), col(COL_GZB), col(COL_GI), col(COL_GO), vec, vec],
        out_specs=pl.BlockSpec((1, S, LANES), lambda b, h: (b, 0, h)),
        out_shape=jax.ShapeDtypeStruct((B, S, W_GROUP), BF16),
        scratch_shapes=[pltpu.VMEM((S, REC_DIM), F32),
                        pltpu.VMEM((2, CHUNK, CHUNK), jnp.int32),
                        pltpu.VMEM((REC_DIM, REC_DIM), F32),
                        pltpu.VMEM((REC_DIM, REC_DIM), F32)],
        compiler_params=_params(("parallel", "parallel")),
        name="hgrn2",
    )(proj, proj, proj, proj, proj,
      lb.astype(F32).reshape(REC_HEADS, 1, LANES), norm_g.astype(F32).reshape(REC_HEADS, 1, LANES))


RET_HEADS_PER_STEP = 2


def _retention_kernel(dl_ref, q_ref, k_ref, v_ref, g_ref, ng_ref, o_ref, const_ref, *scratch):
    S = q_ref.shape[1]
    nc = S // CHUNK
    nh = RET_HEADS_PER_STEP
    acc_refs, st_refs = scratch[:nh], scratch[nh:]
    scale = REC_DIM ** -0.5
    t = lax.broadcasted_iota(jnp.int32, (CHUNK, CHUNK), 0)
    s = lax.broadcasted_iota(jnp.int32, (CHUNK, CHUNK), 1)
    rel = (t - s).astype(F32)
    pos = lax.broadcasted_iota(jnp.int32, (CHUNK, REC_DIM), 0).astype(F32)
    chunk_decay = []
    for hh in range(nh):
        head = pl.program_id(1) * nh + hh

        def log_gamma(d):
            x = jnp.full((1, LANES), dl_ref[d, head], F32)
            return jnp.minimum(x, 0.0) - jnp.log1p(jnp.exp(-jnp.abs(x)))

        lgf, lgb = log_gamma(0), log_gamma(1)
        const_ref[hh, 0] = (jnp.where(t >= s, jnp.exp(lgf * rel), 0.0)
                            + jnp.where(s >= t, jnp.exp(-lgb * rel), 0.0)) * scale
        const_ref[hh, 1] = jnp.exp(lgf * (pos + 1.0))
        const_ref[hh, 2] = jnp.exp(lgf * (CHUNK - 1.0 - pos)) * scale
        const_ref[hh, 3] = jnp.exp(lgb * (CHUNK - pos))
        const_ref[hh, 4] = jnp.exp(lgb * pos) * scale
        chunk_decay.append((jnp.exp(lgf * CHUNK), jnp.exp(lgb * CHUNK)))
    for st in st_refs:
        st[...] = jnp.zeros_like(st)

    def load(ci, hh):
        rows = pl.ds(pl.multiple_of(ci * CHUNK, CHUNK), CHUNK)
        lanes = slice(hh * LANES, (hh + 1) * LANES)
        return rows, lanes, q_ref[0, rows, lanes], k_ref[0, rows, lanes], v_ref[0, rows, lanes].astype(BF16)

    def fwd_part(ci, hh):
        rows, lanes, q, k, vb = load(ci, hh)
        st = st_refs[2 * hh]
        a = lax.dot_general(q.astype(BF16), k.astype(BF16), _NT, preferred_element_type=F32)
        o = jnp.dot((a * const_ref[hh, 0]).astype(BF16), vb, preferred_element_type=F32)
        o = o + lax.dot_general((q * const_ref[hh, 1]).astype(BF16), st[...].astype(BF16), _NT,
                                preferred_element_type=F32)
        st[...] = st[...] * chunk_decay[hh][0] + lax.dot_general(
            vb, (k * const_ref[hh, 2]).astype(BF16), _TN, preferred_element_type=F32)
        return rows, lanes, o

    def bwd_part(ci, hh):
        rows, lanes, q, k, vb = load(ci, hh)
        st = st_refs[2 * hh + 1]
        o = lax.dot_general((q * const_ref[hh, 3]).astype(BF16), st[...].astype(BF16), _NT,
                            preferred_element_type=F32)
        st[...] = st[...] * chunk_decay[hh][1] + lax.dot_general(
            vb, (k * const_ref[hh, 4]).astype(BF16), _TN, preferred_element_type=F32)
        return rows, lanes, o

    def finish(rows, lanes, hh, o):
        o = o + acc_refs[hh][rows, :]
        oc = o - jnp.mean(o, axis=-1, keepdims=True)
        y = oc * lax.rsqrt(jnp.mean(oc * oc, axis=-1, keepdims=True) + HEAD_NORM_EPS)
        o_ref[0, rows, lanes] = (y * ng_ref[hh] * _silu(g_ref[0, rows, lanes])).astype(o_ref.dtype)

    def first_half(j, carry):
        for hh in range(nh):
            rows, _, o = fwd_part(j, hh)
            acc_refs[hh][rows, :] = o
            rows, _, o = bwd_part(nc - 1 - j, hh)
            acc_refs[hh][rows, :] = o
        return carry

    def second_half(j, carry):
        for hh in range(nh):
            rows, lanes, o = fwd_part(j, hh)
            finish(rows, lanes, hh, o)
            rows, lanes, o = bwd_part(nc - 1 - j, hh)
            finish(rows, lanes, hh, o)
        return carry

    lax.fori_loop(0, nc // 2, first_half, 0)
    lax.fori_loop(nc // 2, nc, second_half, 0)


def retention(proj, decay_logit, norm_g):
    B, S, _ = proj.shape
    nh = RET_HEADS_PER_STEP
    assert (S // CHUNK) % 2 == 0 and REC_HEADS % nh == 0
    col = lambda c: pl.BlockSpec((1, S, nh * LANES), lambda b, h: (b, 0, c // nh + h))
    return pl.pallas_call(
        _retention_kernel,
        grid=(B, REC_HEADS // nh),
        in_specs=[pl.BlockSpec(memory_space=pltpu.SMEM),
                  col(COL_RQ), col(COL_RK), col(COL_RV), col(COL_RG),
                  pl.BlockSpec((nh, 1, LANES), lambda b, h: (h, 0, 0))],
        out_specs=pl.BlockSpec((1, S, nh * LANES), lambda b, h: (b, 0, h)),
        out_shape=jax.ShapeDtypeStruct((B, S, W_GROUP), BF16),
        scratch_shapes=([pltpu.VMEM((nh, 5, CHUNK, REC_DIM), F32)]
                        + [pltpu.VMEM((S, REC_DIM), F32)] * nh
                        + [pltpu.VMEM((REC_DIM, REC_DIM), F32)] * (2 * nh)),
        compiler_params=_params(("parallel", "parallel")),
        name="retention",
    )(decay_logit.astype(F32), proj, proj, proj, proj,
      norm_g.astype(F32).reshape(REC_HEADS, 1, LANES))


ROUTE_ROWS = 8


def _first_max(vals):
    best, idx = vals[0], jnp.zeros(vals[0].shape, jnp.int32)
    for i in range(1, len(vals)):
        take = vals[i] > best
        best = jnp.where(take, vals[i], best)
        idx = jnp.where(take, i, idx)
    return best, idx


def _pick(idx, vals):
    out = vals[-1]
    for i in range(len(vals) - 2, -1, -1):
        out = jnp.where(idx == i, vals[i], out)
    return out


def _out_proj_kernel(ya_ref, yc_ref, yg_ref, yr_ref, w_ref, h_ref, g_ref, b_ref, rwh_ref, rwl_ref,
                     rb_ref, h1_ref, h1p_ref, ri_ref, rwt_ref, cnt_ref, carry_ref, ycat_ref):
    tm = h_ref.shape[0]

    @pl.when(pl.program_id(0) == 0)
    def _():
        carry_ref[...] = jnp.zeros_like(carry_ref)

    ycat_ref[:, 0:W_GROUP] = ya_ref[...]
    ycat_ref[:, W_GROUP:2 * W_GROUP] = yc_ref[...]
    ycat_ref[:, 2 * W_GROUP:3 * W_GROUP] = yg_ref[...]
    ycat_ref[:, 3 * W_GROUP:4 * W_GROUP] = yr_ref[...]
    mix = jnp.dot(ycat_ref[...], w_ref[...], preferred_element_type=F32)
    h1 = _layer_norm(DEEPNORM_ALPHA * h_ref[...] + mix, g_ref[...], b_ref[...])
    h1_ref[...] = h1
    h1p_ref[...] = _pack_rows(h1)

    h1_hi = h1.astype(BF16)
    h1_lo = (h1 - h1_hi.astype(F32)).astype(BF16)
    logits = (lax.dot_general(rwh_ref[...], h1_hi, _NT, preferred_element_type=F32)
              + lax.dot_general(rwh_ref[...], h1_lo, _NT, preferred_element_type=F32)
              + lax.dot_general(rwl_ref[...], h1_hi, _NT, preferred_element_type=F32)
              + rb_ref[...])
    rows = [logits[e:e + 1, :] for e in range(N_EXPERTS)]
    mx = functools.reduce(jnp.maximum, rows)
    ex = [jnp.exp(r - mx) for r in rows]
    den = functools.reduce(jnp.add, ex)
    pr = [x / den for x in ex]
    scores = []
    for g in range(N_GROUPS):
        a, b, c, d = pr[4 * g:4 * g + 4]
        hi1, lo1, hi2, lo2 = jnp.maximum(a, b), jnp.minimum(a, b), jnp.maximum(c, d), jnp.minimum(c, d)
        scores.append(jnp.maximum(hi1, hi2) + jnp.maximum(jnp.minimum(hi1, hi2), jnp.maximum(lo1, lo2)))
    _, gsel = _first_max(scores)
    cand = [_pick(gsel, [pr[4 * g + i] for g in range(N_GROUPS)]) for i in range(EXPERTS_PER_GROUP)]
    p0, i0 = _first_max(cand)
    p1, i1 = _first_max([jnp.where(i0 == i, -1.0, cand[i]) for i in range(EXPERTS_PER_GROUP)])
    e0 = gsel * EXPERTS_PER_GROUP + i0
    e1 = gsel * EXPERTS_PER_GROUP + i1
    tot = p0 + p1
    w0, w1 = p0 / tot, p1 / tot

    ind = jnp.concatenate([((e0 == e) | (e1 == e)).astype(F32) for e in range(N_EXPERTS)], axis=0)
    before = (lax.broadcasted_iota(jnp.int32, (tm, tm), 0)
              < lax.broadcasted_iota(jnp.int32, (tm, tm), 1)).astype(BF16)
    rank = jnp.dot(ind.astype(BF16), before, preferred_element_type=F32) + carry_ref[...]
    carry_ref[...] = carry_ref[...] + jnp.sum(ind, axis=-1, keepdims=True)
    rk = [rank[e:e + 1, :] for e in range(N_EXPERTS)]
    r0 = _pick(e0, rk).astype(jnp.int32)
    r1 = _pick(e1, rk).astype(jnp.int32)
    zi = jnp.zeros((ROUTE_ROWS - 4, tm), jnp.int32)
    ri_ref[...] = jnp.concatenate([e0, e1, r0, r1, zi], axis=0)
    rwt_ref[...] = jnp.concatenate([w0, w1, jnp.zeros((ROUTE_ROWS - 2, tm), F32)], axis=0)
    cnt_ref[...] = jnp.broadcast_to(carry_ref[...], cnt_ref.shape)


def out_proj_ln_route(ys, w_out_bf16, layer, h, g, b, router_w, router_b, tm=512):
    T, D = h.shape
    tm = min(tm, T)
    part = pl.BlockSpec((tm, W_GROUP), lambda i: (i, 0))
    row = pl.BlockSpec((tm, D), lambda i: (i, 0))
    vec = pl.BlockSpec((1, D), lambda i: (0, 0))
    route = pl.BlockSpec((ROUTE_ROWS, tm), lambda i: (0, i))
    once = pl.Buffered(1)
    rw_t = router_w.astype(F32).T
    rw_hi = rw_t.astype(BF16)
    rw_lo = (rw_t - rw_hi.astype(F32)).astype(BF16)
    return pl.pallas_call(
        _out_proj_kernel,
        grid=(T // tm,),
        in_specs=[part, part, part, part,
                  pl.BlockSpec((None, D, D), lambda i: (layer, 0, 0), pipeline_mode=once),
                  row, vec, vec,
                  pl.BlockSpec((N_EXPERTS, D), lambda i: (0, 0), pipeline_mode=once),
                  pl.BlockSpec((N_EXPERTS, D), lambda i: (0, 0), pipeline_mode=once),
                  pl.BlockSpec((N_EXPERTS, 1), lambda i: (0, 0))],
        out_specs=[row, pl.BlockSpec((tm, D // 2), lambda i: (i, 0)), route, route,
                   pl.BlockSpec((N_EXPERTS, LANES), lambda i: (0, 0))],
        out_shape=[jax.ShapeDtypeStruct((T, D), F32),
                   jax.ShapeDtypeStruct((T, D // 2), jnp.uint32),
                   jax.ShapeDtypeStruct((ROUTE_ROWS, T), jnp.int32),
                   jax.ShapeDtypeStruct((ROUTE_ROWS, T), F32),
                   jax.ShapeDtypeStruct((N_EXPERTS, LANES), F32)],
        scratch_shapes=[pltpu.VMEM((N_EXPERTS, 1), F32), pltpu.VMEM((tm, D), BF16)],
        compiler_params=_params(("arbitrary",)),
        name="out_proj_ln_route",
    )(*[y.reshape(T, W_GROUP) for y in ys], w_out_bf16, h, g.reshape(1, D), b.reshape(1, D),
      rw_hi, rw_lo, router_b.astype(F32).reshape(N_EXPERTS, 1))


def _dispatch_kernel(pad_start_ref, pad_len_ref, nv_ref, pos_ref, h_ref, x_hbm, zero_ref, sem, zsem):
    tm = h_ref.shape[0]
    tile = zero_ref.shape[0]
    n_tiles = x_hbm.shape[0] // tile

    def row_copy(r, p):
        return pltpu.make_async_copy(h_ref.at[pl.ds(r, 1), :], x_hbm.at[pl.ds(p, 1), :], sem)

    def issue(r, carry):
        row_copy(r, pos_ref[0, 0, r]).start()
        row_copy(r, pos_ref[0, 0, tm + r]).start()
        return carry

    lax.fori_loop(0, tm, issue, 0)

    @pl.when(pl.program_id(0) == 0)
    def _():
        zero_ref[...] = jnp.zeros_like(zero_ref)

        def tcopy(i):
            return pltpu.make_async_copy(
                zero_ref, x_hbm.at[pl.ds(pl.multiple_of(i * tile, tile), tile), :], zsem)

        def tissue(i, carry):
            tcopy(i).start()
            return carry

        def twait(i, carry):
            tcopy(i).wait()
            return carry

        lax.fori_loop(nv_ref[0], n_tiles, tissue, 0)
        lax.fori_loop(nv_ref[0], n_tiles, twait, 0)
        for e in range(N_EXPERTS):
            def zcopy(i):
                return pltpu.make_async_copy(
                    zero_ref.at[pl.ds(0, 1), :], x_hbm.at[pl.ds(pad_start_ref[e] + i, 1), :], zsem)

            def zissue(i, carry):
                zcopy(i).start()
                return carry

            def zwait(i, carry):
                zcopy(i).wait()
                return carry

            lax.fori_loop(0, pad_len_ref[e], zissue, 0)
            lax.fori_loop(0, pad_len_ref[e], zwait, 0)

    def wait(r, carry):
        row_copy(r, 0).wait()
        row_copy(r, 0).wait()
        return carry

    lax.fori_loop(0, tm, wait, 0)


def dispatch(h1, pos, pad_start, pad_len, n_valid, n_rows, tile, tm=256):
    T, D = h1.shape
    tm = min(tm, T)
    return pl.pallas_call(
        _dispatch_kernel,
        grid_spec=pltpu.PrefetchScalarGridSpec(
            num_scalar_prefetch=3,
            grid=(T // tm,),
            in_specs=[pl.BlockSpec((1, 1, 2 * tm), lambda i, *_: (i, 0, 0), memory_space=pltpu.SMEM),
                      pl.BlockSpec((tm, D), lambda i, *_: (i, 0))],
            out_specs=pl.BlockSpec(memory_space=pl.ANY),
            scratch_shapes=[pltpu.VMEM((tile, D), h1.dtype), pltpu.SemaphoreType.DMA,
                            pltpu.SemaphoreType.DMA]),
        out_shape=jax.ShapeDtypeStruct((n_rows, D), h1.dtype),
        compiler_params=_params(("arbitrary",)),
        name="dispatch",
    )(pad_start, pad_len, n_valid, pos, h1)


def _expert_kernel(te_ref, nv_ref, x_ref, wg_ref, wu_ref, wd_ref, y_ref):
    @pl.when(pl.program_id(0) < nv_ref[0])
    def _():
        x = _unpack_rows(x_ref[...]).astype(BF16)
        gate = jnp.dot(x, wg_ref[...], preferred_element_type=F32)
        up = jnp.dot(x, wu_ref[...], preferred_element_type=F32)
        hid = (_silu(gate) * up).astype(BF16)
        y_ref[...] = _pack_rows(jnp.dot(hid, wd_ref[...], preferred_element_type=F32))

    @pl.when(pl.program_id(0) >= nv_ref[0])
    def _():
        y_ref[...] = jnp.zeros_like(y_ref)


def expert_ffn(x_sorted, tile_expert, n_valid, w_gate, w_up, w_down, layer, tm):
    A, DP = x_sorted.shape
    D = 2 * DP
    n_tiles = A // tm
    tile = lambda i, te, nv: (jnp.minimum(i, nv[0] - 1), 0)
    out_tile = lambda i, te, nv: (i, 0)
    wmap = lambda i, te, nv: (layer, te[jnp.minimum(i, nv[0] - 1)], 0, 0)
    return pl.pallas_call(
        _expert_kernel,
        grid_spec=pltpu.PrefetchScalarGridSpec(
            num_scalar_prefetch=2,
            grid=(n_tiles,),
            in_specs=[pl.BlockSpec((tm, DP), tile),
                      pl.BlockSpec((None, None, D, D_EXPERT), wmap),
                      pl.BlockSpec((None, None, D, D_EXPERT), wmap),
                      pl.BlockSpec((None, None, D_EXPERT, D), wmap)],
            out_specs=pl.BlockSpec((tm, DP), out_tile)),
        out_shape=jax.ShapeDtypeStruct((A, DP), jnp.uint32),
        compiler_params=_params(("arbitrary",)),
        name="expert_ffn",
    )(tile_expert, n_valid, x_sorted, w_gate, w_up, w_down)


def _combine_kernel(pos_ref, h_ref, w_ref, g_ref, b_ref, y_hbm, o_ref, ob_ref, buf_ref, sem):
    tm = h_ref.shape[0]

    def row_copy(r, p):
        return pltpu.make_async_copy(y_hbm.at[pl.ds(p, 1), :], buf_ref.at[pl.ds(r, 1), :], sem)

    def issue(r, carry):
        row_copy(r, pos_ref[0, 0, r]).start()
        row_copy(tm + r, pos_ref[0, 0, tm + r]).start()
        return carry

    lax.fori_loop(0, tm, issue, 0)

    def wait(r, carry):
        row_copy(0, 0).wait()
        row_copy(0, 0).wait()
        return carry

    lax.fori_loop(0, tm, wait, 0)
    w = w_ref[...]
    ffn = (w[:, 0:1] * _unpack_rows(buf_ref[0:tm, :])
           + w[:, 1:2] * _unpack_rows(buf_ref[tm:2 * tm, :]))
    h2 = _layer_norm(DEEPNORM_ALPHA * h_ref[...] + ffn, g_ref[...], b_ref[...])
    o_ref[...] = h2
    ob_ref[...] = h2.astype(BF16)


def combine_ln(h1, y_sorted, pos, wcol, g, b, tm=256):
    T, D = h1.shape
    tm = min(tm, T)
    row = pl.BlockSpec((tm, D), lambda i: (i, 0))
    vec = pl.BlockSpec((1, D), lambda i: (0, 0))
    return pl.pallas_call(
        _combine_kernel,
        grid=(T // tm,),
        in_specs=[pl.BlockSpec((1, 1, 2 * tm), lambda i: (i, 0, 0), memory_space=pltpu.SMEM),
                  row,
                  pl.BlockSpec((tm, 2), lambda i: (i, 0)),
                  vec, vec,
                  pl.BlockSpec(memory_space=pl.ANY)],
        out_specs=[row, row],
        out_shape=[jax.ShapeDtypeStruct((T, D), F32), jax.ShapeDtypeStruct((T, D), BF16)],
        scratch_shapes=[pltpu.VMEM((2 * tm, D // 2), jnp.uint32), pltpu.SemaphoreType.DMA],
        compiler_params=_params(("arbitrary",)),
        name="combine_ln",
    )(pos, h1, wcol, g.reshape(1, D), b.reshape(1, D), y_sorted)


EXPERT_TILE = 256
ROUTE_TILE = 256


def _routing_plan(route_i, counts, T, tile, route_tile):
    cnt = counts[:, 0].astype(jnp.int32)
    padded = ((cnt + tile - 1) // tile) * tile
    ends = jnp.cumsum(padded)
    offs = ends - padded
    e0, e1, r0, r1 = route_i[0], route_i[1], route_i[2], route_i[3]
    pos0 = offs[e0] + r0
    pos1 = offs[e1] + r1
    nrt = T // route_tile
    pos = jnp.concatenate([pos0.reshape(nrt, 1, route_tile), pos1.reshape(nrt, 1, route_tile)], axis=-1)
    n_tiles = (2 * T) // tile + N_EXPERTS
    tile_ids = jnp.arange(n_tiles, dtype=jnp.int32)
    tile_expert = jnp.minimum(
        jnp.sum((ends[None, :] // tile <= tile_ids[:, None]).astype(jnp.int32), axis=1),
        N_EXPERTS - 1).astype(jnp.int32)
    n_valid = (ends[-1] // tile).astype(jnp.int32).reshape(1)
    return pos, tile_expert, n_valid, (offs + cnt).astype(jnp.int32), (padded - cnt).astype(jnp.int32)


def _hgrn_lower_bounds(hgrn_lb):
    lb = jnp.cumsum(jax.nn.softmax(hgrn_lb.astype(F32), axis=0), axis=0)
    return lb - lb[0:1]


def kernel(x, emb_ln_g, emb_ln_b, w_in, attn_sink, conv_w, hgrn_lb, hgrn_norm_g, ret_decay_logit,
           ret_norm_g, w_out, ln1_g, ln1_b, router_w, router_b, w_gate, w_up, w_down, ln2_g, ln2_b):
    B, S, D = x.shape
    T = B * S
    depth = w_in.shape[0]
    lb_all = _hgrn_lower_bounds(hgrn_lb)
    route_tile = min(ROUTE_TILE, T)
    n_rows = 2 * T + N_EXPERTS * EXPERT_TILE

    w_out_b, w_gate_b, w_up_b, w_down_b = (w.astype(BF16) for w in (w_out, w_gate, w_up, w_down))
    h, hb = embed_ln(x.reshape(T, D), emb_ln_g, emb_ln_b)
    for l in range(depth):
        proj = in_proj(hb, w_in, l).reshape(B, S, D_IN_PROJ)
        ys = [attention(proj, attn_sink[l]),
              short_conv(proj, conv_w[l]),
              hgrn2(proj, lb_all[l], hgrn_norm_g[l]),
              retention(proj, ret_decay_logit[l], ret_norm_g[l])]
        h1, h1_packed, route_i, route_w, counts = out_proj_ln_route(
            ys, w_out_b, l, h, ln1_g[l], ln1_b[l], router_w, router_b)
        pos, tile_expert, n_valid, pad_start, pad_len = _routing_plan(
            route_i, counts, T, EXPERT_TILE, route_tile)
        x_sorted = dispatch(h1_packed, pos, pad_start, pad_len, n_valid, n_rows, EXPERT_TILE,
                            tm=route_tile)
        y_sorted = expert_ffn(x_sorted, tile_expert, n_valid, w_gate_b, w_up_b, w_down_b, l,
                              EXPERT_TILE)
        h, hb = combine_ln(h1, y_sorted, pos, route_w[0:2].T, ln2_g[l], ln2_b[l], tm=route_tile)
    return h.reshape(B, S, D)
```

```python
import functools

import jax
import jax.numpy as jnp
from jax import lax
from jax.experimental import pallas as pl
from jax.experimental.pallas import tpu as pltpu

F32 = jnp.float32
BF16 = jnp.bfloat16

D_MODEL = 2048
DEPTH = 2
W_GROUP = 512
HEAD_DIM = 64
N_ATTN_HEADS = 8
N_KV_HEADS = 2
ATTN_GROUP = N_ATTN_HEADS // N_KV_HEADS
WINDOW = 128
ATTN_BLOCK = 128
REC_HEADS = 4
REC_DIM = 128
N_EXPERTS = 16
N_GROUPS = 4
EXPERTS_PER_GROUP = 4
D_EXPERT = 1024
D_IN_PROJ = 6912
DEEPNORM_ALPHA = (2.0 * DEPTH) ** 0.25
LN_EPS = 1e-5
HEAD_NORM_EPS = 1e-6
NEG_BIG = -1e30

LANES = 128
SUBLANES = 8
COL_AQ, COL_AK, COL_AV = 0, 4, 5
COL_CB, COL_CC, COL_CH = 6, 10, 14
COL_GQ, COL_GZF, COL_GZB, COL_GI, COL_GO = 18, 22, 26, 30, 34
COL_RQ, COL_RK, COL_RV, COL_RG = 38, 42, 46, 50

CHUNK = 128
VMEM_LIMIT = 56 * 1024 * 1024

_NT = (((1,), (1,)), ((), ()))
_TN = (((0,), (0,)), ((), ()))


def _params(sem, vmem=VMEM_LIMIT):
    return pltpu.CompilerParams(dimension_semantics=sem, vmem_limit_bytes=vmem)


def _layer_norm(x, g, b):
    mu = jnp.mean(x, axis=-1, keepdims=True)
    xc = x - mu
    var = jnp.mean(xc * xc, axis=-1, keepdims=True)
    return xc * lax.rsqrt(var + LN_EPS) * g + b


def _silu(x):
    return x * (1.0 / (1.0 + jnp.exp(-x)))


def _pack_rows(x):
    n = x.shape[1] // 2
    hi = lax.bitcast_convert_type(x[:, :n].astype(BF16).astype(F32), jnp.uint32)
    lo = lax.bitcast_convert_type(x[:, n:].astype(BF16).astype(F32), jnp.uint32)
    return hi | (lo >> 16)


def _unpack_rows(w):
    hi = lax.bitcast_convert_type(w & jnp.uint32(0xFFFF0000), F32)
    lo = lax.bitcast_convert_type(w << 16, F32)
    return jnp.concatenate([hi, lo], axis=-1)


def _embed_ln_kernel(x_ref, g_ref, b_ref, h_ref, hb_ref):
    h = _layer_norm(x_ref[...], g_ref[...], b_ref[...])
    h_ref[...] = h
    hb_ref[...] = h.astype(BF16)


def embed_ln(x2, g, b, tm=512):
    T, D = x2.shape
    return pl.pallas_call(
        _embed_ln_kernel,
        grid=(T // tm,),
        in_specs=[pl.BlockSpec((tm, D), lambda i: (i, 0)),
                  pl.BlockSpec((1, D), lambda i: (0, 0)),
                  pl.BlockSpec((1, D), lambda i: (0, 0))],
        out_specs=[pl.BlockSpec((tm, D), lambda i: (i, 0)),
                   pl.BlockSpec((tm, D), lambda i: (i, 0))],
        out_shape=[jax.ShapeDtypeStruct((T, D), F32), jax.ShapeDtypeStruct((T, D), BF16)],
        compiler_params=_params(("parallel",)),
        name="embed_ln",
    )(x2, g.reshape(1, D), b.reshape(1, D))


def _in_proj_kernel(x_ref, w_ref, o_ref, wb_ref):
    @pl.when(pl.program_id(1) == 0)
    def _():
        wb_ref[...] = w_ref[...].astype(BF16)

    o_ref[...] = jnp.dot(x_ref[...], wb_ref[...], preferred_element_type=F32)


def in_proj(hb, w_in, layer, tm=1024, tn=768):
    T, K = hb.shape
    N = w_in.shape[2]
    tm = min(tm, T)
    return pl.pallas_call(
        _in_proj_kernel,
        grid=(N // tn, T // tm),
        in_specs=[pl.BlockSpec((tm, K), lambda n, m: (m, 0)),
                  pl.BlockSpec((None, K, tn), lambda n, m: (layer, 0, n))],
        out_specs=pl.BlockSpec((tm, tn), lambda n, m: (m, n)),
        out_shape=jax.ShapeDtypeStruct((T, N), F32),
        scratch_shapes=[pltpu.VMEM((K, tn), BF16)],
        compiler_params=_params(("arbitrary", "arbitrary")),
        name="in_proj",
    )(hb, w_in)


def _attn_kernel(sink_ref, q_ref, kp_ref, kc_ref, kn_ref, vp_ref, vc_ref, vn_ref, bias_ref, o_ref):
    L = ATTN_BLOCK
    n = pl.program_id(1)
    nb = pl.num_programs(1)
    col = lax.broadcasted_iota(jnp.int32, (1, 3 * L), 1)
    valid = ((col >= L) | (n > 0)) & ((col < 2 * L) | (n < nb - 1))
    edge = jnp.where(valid, 0.0, NEG_BIG)
    q = q_ref[0]
    k3 = jnp.concatenate([kp_ref[0], kc_ref[0], kn_ref[0]], axis=0)
    v3 = jnp.concatenate([vp_ref[0], vc_ref[0], vn_ref[0]], axis=0)
    outs = []
    for h in range(N_KV_HEADS):
        kh = k3[:, h * HEAD_DIM:(h + 1) * HEAD_DIM].astype(BF16)
        vh = v3[:, h * HEAD_DIM:(h + 1) * HEAD_DIM].astype(BF16)
        for g in range(ATTN_GROUP):
            hd = h * ATTN_GROUP + g
            qh = (q[:, hd * HEAD_DIM:(hd + 1) * HEAD_DIM] * (HEAD_DIM ** -0.5)).astype(BF16)
            s = lax.dot_general(qh, kh, _NT, preferred_element_type=F32)
            s = s + bias_ref[hd] + edge
            sk = sink_ref[hd]
            m = jnp.maximum(jnp.max(s, axis=-1, keepdims=True), sk)
            p = jnp.exp(s - m)
            den = jnp.sum(p, axis=-1, keepdims=True) + jnp.exp(sk - m)
            o = jnp.dot(p.astype(BF16), vh, preferred_element_type=F32)
            outs.append(o / den)
    o_ref[0] = jnp.concatenate(outs, axis=-1).astype(o_ref.dtype)


def _attn_bias():
    L = ATTN_BLOCK
    k_rel = jnp.arange(3 * L) - L
    dist = jnp.abs(k_rel[None, :] - jnp.arange(L)[:, None]).astype(F32)
    slopes = 2.0 ** (-8.0 * jnp.arange(1, N_ATTN_HEADS + 1, dtype=F32) / N_ATTN_HEADS)
    bias = -slopes[:, None, None] * dist[None]
    return jnp.where(dist[None] <= WINDOW, bias, NEG_BIG)


def attention(proj, sink):
    B, S, _ = proj.shape
    L = ATTN_BLOCK
    nb = S // L
    kv = lambda col, shift: pl.BlockSpec(
        (1, L, LANES), lambda b, n: (b, jnp.clip(n + shift, 0, nb - 1), col))
    return pl.pallas_call(
        _attn_kernel,
        grid=(B, nb),
        in_specs=[pl.BlockSpec(memory_space=pltpu.SMEM),
                  pl.BlockSpec((1, L, W_GROUP), lambda b, n: (b, n, COL_AQ // 4)),
                  kv(COL_AK, -1), kv(COL_AK, 0), kv(COL_AK, 1),
                  kv(COL_AV, -1), kv(COL_AV, 0), kv(COL_AV, 1),
                  pl.BlockSpec((N_ATTN_HEADS, L, 3 * L), lambda b, n: (0, 0, 0))],
        out_specs=pl.BlockSpec((1, L, W_GROUP), lambda b, n: (b, n, 0)),
        out_shape=jax.ShapeDtypeStruct((B, S, W_GROUP), BF16),
        compiler_params=_params(("parallel", "arbitrary")),
        name="attention",
    )(sink.astype(F32), proj, proj, proj, proj, proj, proj, proj, _attn_bias())


CONV_ROWS = 512
HALO = 8


def _conv_kernel(b_ref, c_ref, h_ref, w_ref, o_ref, u_ref):
    S = b_ref.shape[1]
    R = min(CONV_ROWS, S)
    u_ref[0:HALO, :] = jnp.zeros((HALO, LANES), F32)
    u_ref[S + HALO:S + 2 * HALO, :] = jnp.zeros((HALO, LANES), F32)

    def gate(i, carry):
        r = pl.multiple_of(i * R, R)
        u_ref[pl.ds(r + HALO, R), :] = c_ref[0, pl.ds(r, R), :] * h_ref[0, pl.ds(r, R), :]
        return carry

    lax.fori_loop(0, S // R, gate, 0)
    w0, w1, w2 = w_ref[0:1, :], w_ref[1:2, :], w_ref[2:3, :]

    def conv(i, carry):
        r = pl.multiple_of(i * R, R)
        a = u_ref[pl.ds(r, R + 2 * HALO), :]
        prev = pltpu.roll(a, 1, 0)[HALO:HALO + R]
        nxt = pltpu.roll(a, R + 2 * HALO - 1, 0)[HALO:HALO + R]
        y = w0 * prev + w1 * a[HALO:HALO + R] + w2 * nxt
        o_ref[0, pl.ds(r, R), :] = (b_ref[0, pl.ds(r, R), :] * y).astype(o_ref.dtype)
        return carry

    lax.fori_loop(0, S // R, conv, 0)


def short_conv(proj, conv_w):
    B, S, _ = proj.shape
    nj = W_GROUP // LANES
    col = lambda c: pl.BlockSpec((1, S, LANES), lambda b, j: (b, 0, c + j))
    return pl.pallas_call(
        _conv_kernel,
        grid=(B, nj),
        in_specs=[col(COL_CB), col(COL_CC), col(COL_CH),
                  pl.BlockSpec((3, LANES), lambda b, j: (0, j))],
        out_specs=pl.BlockSpec((1, S, LANES), lambda b, j: (b, 0, j)),
        out_shape=jax.ShapeDtypeStruct((B, S, W_GROUP), BF16),
        scratch_shapes=[pltpu.VMEM((S + 2 * HALO, LANES), F32)],
        compiler_params=_params(("parallel", "parallel")),
        name="short_conv",
    )(proj, proj, proj, conv_w.astype(F32))


def _level_codes():
    t = lax.broadcasted_iota(jnp.int32, (CHUNK, CHUNK), 0)
    s = lax.broadcasted_iota(jnp.int32, (CHUNK, CHUNK), 1)
    x = t ^ s
    hb = jnp.zeros((CHUNK, CHUNK), jnp.int32)
    c = 1
    while c < CHUNK:
        hb = jnp.where((x & c) != 0, c, hb)
        c *= 2
    diag = jnp.where(t == s, 0, -1)
    return jnp.where(t > s, hb, diag), jnp.where(t < s, hb, diag)


def _hgrn_chunk(q, v, z, lb, code, reverse):
    e = jnp.exp(-jnp.abs(z))
    r = 1.0 / (1.0 + e)
    er = e * r
    nonneg = z >= 0
    f = lb + (1.0 - lb) * jnp.where(nonneg, r, er)
    k = (1.0 - lb) * jnp.where(nonneg, er, r)
    row = lax.broadcasted_iota(jnp.int32, (CHUNK, REC_DIM), 0)
    qs = f
    ks = jnp.ones_like(f)
    blk = f
    a = jnp.where(code == 0,
                  lax.dot_general(q.astype(BF16), k.astype(BF16), _NT, preferred_element_type=F32),
                  0.0)
    c = 1
    while c < CHUNK:
        p = lax.dot_general((q * qs).astype(BF16), (k * ks).astype(BF16), _NT,
                            preferred_element_type=F32)
        a = jnp.where(code == c, p, a)
        if c < SUBLANES:
            upper = (row & c) != 0
            grouped = blk.reshape(CHUNK // SUBLANES, SUBLANES, REC_DIM)
            down = pltpu.roll(grouped, c, 1).reshape(CHUNK, REC_DIM)
            up = (down if 2 * c == SUBLANES
                  else pltpu.roll(grouped, SUBLANES - c, 1).reshape(CHUNK, REC_DIM))
            sib = jnp.where(upper, down, up)
            grow_q = jnp.logical_not(upper) if reverse else upper
            qs = qs * jnp.where(grow_q, sib, 1.0)
            ks = ks * jnp.where(grow_q, 1.0, sib)
            blk = blk * sib
        else:
            step = c // SUBLANES
            group = lambda x, j: x[j * SUBLANES:(j + 1) * SUBLANES]
            nq, nk, nb = [], [], []
            for j in range(CHUNK // SUBLANES):
                upper = (j & step) != 0
                sib = group(blk, j ^ step)
                grow_q = (not upper) if reverse else upper
                nq.append(group(qs, j) * sib if grow_q else group(qs, j))
                nk.append(group(ks, j) if grow_q else group(ks, j) * sib)
                nb.append(group(blk, j) * sib)
            qs, ks, blk = (jnp.concatenate(x, axis=0) for x in (nq, nk, nb))
        c *= 2
    return a, q * qs, k * ks, blk[0:1, :]


def _hgrn_kernel(q_ref, zf_ref, zb_ref, i_ref, g_ref, lb_ref, ng_ref, o_ref, acc_ref, code_ref,
                 stf_ref, stb_ref):
    S = q_ref.shape[1]
    nc = S // CHUNK
    lb = lb_ref[0]
    cf, cb = _level_codes()
    code_ref[0] = cf
    code_ref[1] = cb
    stf_ref[...] = jnp.zeros_like(stf_ref)
    stb_ref[...] = jnp.zeros_like(stb_ref)

    def part(ci, reverse):
        z_ref, st = (zb_ref, stb_ref) if reverse else (zf_ref, stf_ref)
        rows = pl.ds(pl.multiple_of(ci * CHUNK, CHUNK), CHUNK)
        q = q_ref[0, rows, :]
        v = i_ref[0, rows, :]
        vb = v.astype(BF16)
        a, qd, kd, dec = _hgrn_chunk(q, v, z_ref[0, rows, :], lb,
                                     code_ref[1 if reverse else 0], reverse)
        o = jnp.dot(a.astype(BF16), vb, preferred_element_type=F32)
        o = o + lax.dot_general(qd.astype(BF16), st[...].astype(BF16), _NT,
                                preferred_element_type=F32)
        st[...] = st[...] * dec + lax.dot_general(vb, kd.astype(BF16), _TN,
                                                  preferred_element_type=F32)
        return rows, o

    def finish(rows, o):
        o = o + acc_ref[rows, :]
        y = o * lax.rsqrt(jnp.mean(o * o, axis=-1, keepdims=True) + HEAD_NORM_EPS)
        y = y * ng_ref[0] * _silu(g_ref[0, rows, :])
        o_ref[0, rows, :] = y.astype(o_ref.dtype)

    def first_half(j, carry):
        rows, o = part(j, False)
        acc_ref[rows, :] = o
        rows, o = part(nc - 1 - j, True)
        acc_ref[rows, :] = o
        return carry

    def second_half(j, carry):
        finish(*part(j, False))
        finish(*part(nc - 1 - j, True))
        return carry

    lax.fori_loop(0, nc // 2, first_half, 0)
    lax.fori_loop(nc // 2, nc, second_half, 0)


def hgrn2(proj, lb, norm_g):
    B, S, _ = proj.shape
    assert (S // CHUNK) % 2 == 0
    col = lambda c: pl.BlockSpec((1, S, LANES), lambda b, h: (b, 0, c + h))
    vec = pl.BlockSpec((1, 1, LANES), lambda b, h: (h, 0, 0))
    return pl.pallas_call(
        _hgrn_kernel,
        grid=(B, REC_HEADS),
        in_specs=[col(COL_GQ), col(COL_GZF), col(COL_GZB), col(COL_GI), col(COL_GO), vec, vec],
        out_specs=pl.BlockSpec((1, S, LANES), lambda b, h: (b, 0, h)),
        out_shape=jax.ShapeDtypeStruct((B, S, W_GROUP), BF16),
        scratch_shapes=[pltpu.VMEM((S, REC_DIM), F32),
                        pltpu.VMEM((2, CHUNK, CHUNK), jnp.int32),
                        pltpu.VMEM((REC_DIM, REC_DIM), F32),
                        pltpu.VMEM((REC_DIM, REC_DIM), F32)],
        compiler_params=_params(("parallel", "parallel")),
        name="hgrn2",
    )(proj, proj, proj, proj, proj,
      lb.astype(F32).reshape(REC_HEADS, 1, LANES), norm_g.astype(F32).reshape(REC_HEADS, 1, LANES))


RET_HEADS_PER_STEP = 2


def _retention_kernel(dl_ref, q_ref, k_ref, v_ref, g_ref, ng_ref, o_ref, const_ref, *scratch):
    S = q_ref.shape[1]
    nc = S // CHUNK
    nh = RET_HEADS_PER_STEP
    acc_refs, st_refs = scratch[:nh], scratch[nh:]
    scale = REC_DIM ** -0.5
    t = lax.broadcasted_iota(jnp.int32, (CHUNK, CHUNK), 0)
    s = lax.broadcasted_iota(jnp.int32, (CHUNK, CHUNK), 1)
    rel = (t - s).astype(F32)
    pos = lax.broadcasted_iota(jnp.int32, (CHUNK, REC_DIM), 0).astype(F32)
    chunk_decay = []
    for hh in range(nh):
        head = pl.program_id(1) * nh + hh

        def log_gamma(d):
            x = jnp.full((1, LANES), dl_ref[d, head], F32)
            return jnp.minimum(x, 0.0) - jnp.log1p(jnp.exp(-jnp.abs(x)))

        lgf, lgb = log_gamma(0), log_gamma(1)
        const_ref[hh, 0] = (jnp.where(t >= s, jnp.exp(lgf * rel), 0.0)
                            + jnp.where(s >= t, jnp.exp(-lgb * rel), 0.0)) * scale
        const_ref[hh, 1] = jnp.exp(lgf * (pos + 1.0))
        const_ref[hh, 2] = jnp.exp(lgf * (CHUNK - 1.0 - pos)) * scale
        const_ref[hh, 3] = jnp.exp(lgb * (CHUNK - pos))
        const_ref[hh, 4] = jnp.exp(lgb * pos) * scale
        chunk_decay.append((jnp.exp(lgf * CHUNK), jnp.exp(lgb * CHUNK)))
    for st in st_refs:
        st[...] = jnp.zeros_like(st)

    def load(ci, hh):
        rows = pl.ds(pl.multiple_of(ci * CHUNK, CHUNK), CHUNK)
        lanes = slice(hh * LANES, (hh + 1) * LANES)
        return rows, lanes, q_ref[0, rows, lanes], k_ref[0, rows, lanes], v_ref[0, rows, lanes].astype(BF16)

    def fwd_part(ci, hh):
        rows, lanes, q, k, vb = load(ci, hh)
        st = st_refs[2 * hh]
        a = lax.dot_general(q.astype(BF16), k.astype(BF16), _NT, preferred_element_type=F32)
        o = jnp.dot((a * const_ref[hh, 0]).astype(BF16), vb, preferred_element_type=F32)
        o = o + lax.dot_general((q * const_ref[hh, 1]).astype(BF16), st[...].astype(BF16), _NT,
                                preferred_element_type=F32)
        st[...] = st[...] * chunk_decay[hh][0] + lax.dot_general(
            vb, (k * const_ref[hh, 2]).astype(BF16), _TN, preferred_element_type=F32)
        return rows, lanes, o

    def bwd_part(ci, hh):
        rows, lanes, q, k, vb = load(ci, hh)
        st = st_refs[2 * hh + 1]
        o = lax.dot_general((q * const_ref[hh, 3]).astype(BF16), st[...].astype(BF16), _NT,
                            preferred_element_type=F32)
        st[...] = st[...] * chunk_decay[hh][1] + lax.dot_general(
            vb, (k * const_ref[hh, 4]).astype(BF16), _TN, preferred_element_type=F32)
        return rows, lanes, o

    def finish(rows, lanes, hh, o):
        o = o + acc_refs[hh][rows, :]
        oc = o - jnp.mean(o, axis=-1, keepdims=True)
        y = oc * lax.rsqrt(jnp.mean(oc * oc, axis=-1, keepdims=True) + HEAD_NORM_EPS)
        o_ref[0, rows, lanes] = (y * ng_ref[hh] * _silu(g_ref[0, rows, lanes])).astype(o_ref.dtype)

    def first_half(j, carry):
        for hh in range(nh):
            rows, _, o = fwd_part(j, hh)
            acc_refs[hh][rows, :] = o
            rows, _, o = bwd_part(nc - 1 - j, hh)
            acc_refs[hh][rows, :] = o
        return carry

    def second_half(j, carry):
        for hh in range(nh):
            rows, lanes, o = fwd_part(j, hh)
            finish(rows, lanes, hh, o)
            rows, lanes, o = bwd_part(nc - 1 - j, hh)
            finish(rows, lanes, hh, o)
        return carry

    lax.fori_loop(0, nc // 2, first_half, 0)
    lax.fori_loop(nc // 2, nc, second_half, 0)


def retention(proj, decay_logit, norm_g):
    B, S, _ = proj.shape
    nh = RET_HEADS_PER_STEP
    assert (S // CHUNK) % 2 == 0 and REC_HEADS % nh == 0
    col = lambda c: pl.BlockSpec((1, S, nh * LANES), lambda b, h: (b, 0, c // nh + h))
    return pl.pallas_call(
        _retention_kernel,
        grid=(B, REC_HEADS // nh),
        in_specs=[pl.BlockSpec(memory_space=pltpu.SMEM),
                  col(COL_RQ), col(COL_RK), col(COL_RV), col(COL_RG),
                  pl.BlockSpec((nh, 1, LANES), lambda b, h: (h, 0, 0))],
        out_specs=pl.BlockSpec((1, S, nh * LANES), lambda b, h: (b, 0, h)),
        out_shape=jax.ShapeDtypeStruct((B, S, W_GROUP), BF16),
        scratch_shapes=([pltpu.VMEM((nh, 5, CHUNK, REC_DIM), F32)]
                        + [pltpu.VMEM((S, REC_DIM), F32)] * nh
                        + [pltpu.VMEM((REC_DIM, REC_DIM), F32)] * (2 * nh)),
        compiler_params=_params(("parallel", "parallel")),
        name="retention",
    )(decay_logit.astype(F32), proj, proj, proj, proj,
      norm_g.astype(F32).reshape(REC_HEADS, 1, LANES))


ROUTE_ROWS = 8


def _first_max(vals):
    best, idx = vals[0], jnp.zeros(vals[0].shape, jnp.int32)
    for i in range(1, len(vals)):
        take = vals[i] > best
        best = jnp.where(take, vals[i], best)
        idx = jnp.where(take, i, idx)
    return best, idx


def _pick(idx, vals):
    out = vals[-1]
    for i in range(len(vals) - 2, -1, -1):
        out = jnp.where(idx == i, vals[i], out)
    return out


def _out_proj_kernel(ya_ref, yc_ref, yg_ref, yr_ref, w_ref, h_ref, g_ref, b_ref, rwh_ref, rwl_ref,
                     rb_ref, h1_ref, h1p_ref, ri_ref, rwt_ref, cnt_ref, carry_ref, ycat_ref):
    tm = h_ref.shape[0]

    @pl.when(pl.program_id(0) == 0)
    def _():
        carry_ref[...] = jnp.zeros_like(carry_ref)

    ycat_ref[:, 0:W_GROUP] = ya_ref[...]
    ycat_ref[:, W_GROUP:2 * W_GROUP] = yc_ref[...]
    ycat_ref[:, 2 * W_GROUP:3 * W_GROUP] = yg_ref[...]
    ycat_ref[:, 3 * W_GROUP:4 * W_GROUP] = yr_ref[...]
    mix = jnp.dot(ycat_ref[...], w_ref[...], preferred_element_type=F32)
    h1 = _layer_norm(DEEPNORM_ALPHA * h_ref[...] + mix, g_ref[...], b_ref[...])
    h1_ref[...] = h1
    h1p_ref[...] = _pack_rows(h1)

    h1_hi = h1.astype(BF16)
    h1_lo = (h1 - h1_hi.astype(F32)).astype(BF16)
    logits = (lax.dot_general(rwh_ref[...], h1_hi, _NT, preferred_element_type=F32)
              + lax.dot_general(rwh_ref[...], h1_lo, _NT, preferred_element_type=F32)
              + lax.dot_general(rwl_ref[...], h1_hi, _NT, preferred_element_type=F32)
              + rb_ref[...])
    rows = [logits[e:e + 1, :] for e in range(N_EXPERTS)]
    mx = functools.reduce(jnp.maximum, rows)
    ex = [jnp.exp(r - mx) for r in rows]
    den = functools.reduce(jnp.add, ex)
    pr = [x / den for x in ex]
    scores = []
    for g in range(N_GROUPS):
        a, b, c, d = pr[4 * g:4 * g + 4]
        hi1, lo1, hi2, lo2 = jnp.maximum(a, b), jnp.minimum(a, b), jnp.maximum(c, d), jnp.minimum(c, d)
        scores.append(jnp.maximum(hi1, hi2) + jnp.maximum(jnp.minimum(hi1, hi2), jnp.maximum(lo1, lo2)))
    _, gsel = _first_max(scores)
    cand = [_pick(gsel, [pr[4 * g + i] for g in range(N_GROUPS)]) for i in range(EXPERTS_PER_GROUP)]
    p0, i0 = _first_max(cand)
    p1, i1 = _first_max([jnp.where(i0 == i, -1.0, cand[i]) for i in range(EXPERTS_PER_GROUP)])
    e0 = gsel * EXPERTS_PER_GROUP + i0
    e1 = gsel * EXPERTS_PER_GROUP + i1
    tot = p0 + p1
    w0, w1 = p0 / tot, p1 / tot

    ind = jnp.concatenate([((e0 == e) | (e1 == e)).astype(F32) for e in range(N_EXPERTS)], axis=0)
    before = (lax.broadcasted_iota(jnp.int32, (tm, tm), 0)
              < lax.broadcasted_iota(jnp.int32, (tm, tm), 1)).astype(BF16)
    rank = jnp.dot(ind.astype(BF16), before, preferred_element_type=F32) + carry_ref[...]
    carry_ref[...] = carry_ref[...] + jnp.sum(ind, axis=-1, keepdims=True)
    rk = [rank[e:e + 1, :] for e in range(N_EXPERTS)]
    r0 = _pick(e0, rk).astype(jnp.int32)
    r1 = _pick(e1, rk).astype(jnp.int32)
    zi = jnp.zeros((ROUTE_ROWS - 4, tm), jnp.int32)
    ri_ref[...] = jnp.concatenate([e0, e1, r0, r1, zi], axis=0)
    rwt_ref[...] = jnp.concatenate([w0, w1, jnp.zeros((ROUTE_ROWS - 2, tm), F32)], axis=0)
    cnt_ref[...] = jnp.broadcast_to(carry_ref[...], cnt_ref.shape)


def out_proj_ln_route(ys, w_out_bf16, layer, h, g, b, router_w, router_b, tm=512):
    T, D = h.shape
    tm = min(tm, T)
    part = pl.BlockSpec((tm, W_GROUP), lambda i: (i, 0))
    row = pl.BlockSpec((tm, D), lambda i: (i, 0))
    vec = pl.BlockSpec((1, D), lambda i: (0, 0))
    route = pl.BlockSpec((ROUTE_ROWS, tm), lambda i: (0, i))
    once = pl.Buffered(1)
    rw_t = router_w.astype(F32).T
    rw_hi = rw_t.astype(BF16)
    rw_lo = (rw_t - rw_hi.astype(F32)).astype(BF16)
    return pl.pallas_call(
        _out_proj_kernel,
        grid=(T // tm,),
        in_specs=[part, part, part, part,
                  pl.BlockSpec((None, D, D), lambda i: (layer, 0, 0), pipeline_mode=once),
                  row, vec, vec,
                  pl.BlockSpec((N_EXPERTS, D), lambda i: (0, 0), pipeline_mode=once),
                  pl.BlockSpec((N_EXPERTS, D), lambda i: (0, 0), pipeline_mode=once),
                  pl.BlockSpec((N_EXPERTS, 1), lambda i: (0, 0))],
        out_specs=[row, pl.BlockSpec((tm, D // 2), lambda i: (i, 0)), route, route,
                   pl.BlockSpec((N_EXPERTS, LANES), lambda i: (0, 0))],
        out_shape=[jax.ShapeDtypeStruct((T, D), F32),
                   jax.ShapeDtypeStruct((T, D // 2), jnp.uint32),
                   jax.ShapeDtypeStruct((ROUTE_ROWS, T), jnp.int32),
                   jax.ShapeDtypeStruct((ROUTE_ROWS, T), F32),
                   jax.ShapeDtypeStruct((N_EXPERTS, LANES), F32)],
        scratch_shapes=[pltpu.VMEM((N_EXPERTS, 1), F32), pltpu.VMEM((tm, D), BF16)],
        compiler_params=_params(("arbitrary",)),
        name="out_proj_ln_route",
    )(*[y.reshape(T, W_GROUP) for y in ys], w_out_bf16, h, g.reshape(1, D), b.reshape(1, D),
      rw_hi, rw_lo, router_b.astype(F32).reshape(N_EXPERTS, 1))


ROW_DMA_UNROLL = 4


def _dispatch_kernel(pad_start_ref, pad_len_ref, nv_ref, pos_ref, h_ref, x_hbm, zero_ref, sems, zsem):
    tm = h_ref.shape[0]
    tile = zero_ref.shape[0]
    n_tiles = x_hbm.shape[0] // tile

    def issue(i, carry):
        for u in range(ROW_DMA_UNROLL):
            r = i * ROW_DMA_UNROLL + u
            for slot in range(2):
                pltpu.make_async_copy(h_ref.at[pl.ds(r, 1), :],
                                      x_hbm.at[pl.ds(pos_ref[0, 0, slot * tm + r], 1), :],
                                      sems.at[slot]).start(priority=slot)
        return carry

    lax.fori_loop(0, tm // ROW_DMA_UNROLL, issue, 0)

    @pl.when(pl.program_id(0) == 0)
    def _():
        zero_ref[...] = jnp.zeros_like(zero_ref)

        def tcopy(i):
            return pltpu.make_async_copy(
                zero_ref, x_hbm.at[pl.ds(pl.multiple_of(i * tile, tile), tile), :], zsem)

        def tissue(i, carry):
            tcopy(i).start()
            return carry

        def twait(i, carry):
            tcopy(i).wait()
            return carry

        lax.fori_loop(nv_ref[0], n_tiles, tissue, 0)
        lax.fori_loop(nv_ref[0], n_tiles, twait, 0)
        for e in range(N_EXPERTS):
            def zcopy(i):
                return pltpu.make_async_copy(
                    zero_ref.at[pl.ds(0, 1), :], x_hbm.at[pl.ds(pad_start_ref[e] + i, 1), :], zsem)

            def zissue(i, carry):
                zcopy(i).start()
                return carry

            def zwait(i, carry):
                zcopy(i).wait()
                return carry

            lax.fori_loop(0, pad_len_ref[e], zissue, 0)
            lax.fori_loop(0, pad_len_ref[e], zwait, 0)

    for slot in range(2):
        pltpu.make_async_copy(h_ref, x_hbm.at[pl.ds(0, tm), :], sems.at[slot]).wait()


def dispatch(h1, pos, pad_start, pad_len, n_valid, n_rows, tile, tm=256):
    T, D = h1.shape
    tm = min(tm, T)
    return pl.pallas_call(
        _dispatch_kernel,
        grid_spec=pltpu.PrefetchScalarGridSpec(
            num_scalar_prefetch=3,
            grid=(T // tm,),
            in_specs=[pl.BlockSpec((1, 1, 2 * tm), lambda i, *_: (i, 0, 0), memory_space=pltpu.SMEM),
                      pl.BlockSpec((tm, D), lambda i, *_: (i, 0))],
            out_specs=pl.BlockSpec(memory_space=pl.ANY),
            scratch_shapes=[pltpu.VMEM((tile, D), h1.dtype), pltpu.SemaphoreType.DMA((2,)),
                            pltpu.SemaphoreType.DMA]),
        out_shape=jax.ShapeDtypeStruct((n_rows, D), h1.dtype),
        compiler_params=_params(("arbitrary",)),
        name="dispatch",
    )(pad_start, pad_len, n_valid, pos, h1)


def _expert_kernel(te_ref, nv_ref, x_ref, wg_ref, wu_ref, wd_ref, y_ref):
    @pl.when(pl.program_id(0) < nv_ref[0])
    def _():
        x = _unpack_rows(x_ref[...]).astype(BF16)
        gate = jnp.dot(x, wg_ref[...], preferred_element_type=F32)
        up = jnp.dot(x, wu_ref[...], preferred_element_type=F32)
        hid = (_silu(gate) * up).astype(BF16)
        y_ref[...] = _pack_rows(jnp.dot(hid, wd_ref[...], preferred_element_type=F32))

    @pl.when(pl.program_id(0) >= nv_ref[0])
    def _():
        y_ref[...] = jnp.zeros_like(y_ref)


def expert_ffn(x_sorted, tile_expert, n_valid, w_gate, w_up, w_down, layer, tm):
    A, DP = x_sorted.shape
    D = 2 * DP
    n_tiles = A // tm
    tile = lambda i, te, nv: (jnp.minimum(i, nv[0] - 1), 0)
    out_tile = lambda i, te, nv: (i, 0)
    wmap = lambda i, te, nv: (layer, te[jnp.minimum(i, nv[0] - 1)], 0, 0)
    return pl.pallas_call(
        _expert_kernel,
        grid_spec=pltpu.PrefetchScalarGridSpec(
            num_scalar_prefetch=2,
            grid=(n_tiles,),
            in_specs=[pl.BlockSpec((tm, DP), tile),
                      pl.BlockSpec((None, None, D, D_EXPERT), wmap),
                      pl.BlockSpec((None, None, D, D_EXPERT), wmap),
                      pl.BlockSpec((None, None, D_EXPERT, D), wmap)],
            out_specs=pl.BlockSpec((tm, DP), out_tile)),
        out_shape=jax.ShapeDtypeStruct((A, DP), jnp.uint32),
        compiler_params=_params(("arbitrary",)),
        name="expert_ffn",
    )(tile_expert, n_valid, x_sorted, w_gate, w_up, w_down)


def _combine_kernel(pos_ref, h_ref, w_ref, g_ref, b_ref, y_hbm, o_ref, ob_ref, buf_ref, sems):
    tm = h_ref.shape[0]

    def issue(i, carry):
        for u in range(ROW_DMA_UNROLL):
            r = i * ROW_DMA_UNROLL + u
            for slot in range(2):
                pltpu.make_async_copy(y_hbm.at[pl.ds(pos_ref[0, 0, slot * tm + r], 1), :],
                                      buf_ref.at[slot, pl.ds(r, 1), :],
                                      sems.at[slot]).start(priority=slot)
        return carry

    lax.fori_loop(0, tm // ROW_DMA_UNROLL, issue, 0)
    for slot in range(2):
        pltpu.make_async_copy(y_hbm.at[pl.ds(0, tm), :], buf_ref.at[slot], sems.at[slot]).wait()
    w = w_ref[...]
    ffn = (w[:, 0:1] * _unpack_rows(buf_ref[0]) + w[:, 1:2] * _unpack_rows(buf_ref[1]))
    h2 = _layer_norm(DEEPNORM_ALPHA * h_ref[...] + ffn, g_ref[...], b_ref[...])
    o_ref[...] = h2
    ob_ref[...] = h2.astype(BF16)


def combine_ln(h1, y_sorted, pos, wcol, g, b, tm=256):
    T, D = h1.shape
    tm = min(tm, T)
    row = pl.BlockSpec((tm, D), lambda i: (i, 0))
    vec = pl.BlockSpec((1, D), lambda i: (0, 0))
    return pl.pallas_call(
        _combine_kernel,
        grid=(T // tm,),
        in_specs=[pl.BlockSpec((1, 1, 2 * tm), lambda i: (i, 0, 0), memory_space=pltpu.SMEM),
                  row,
                  pl.BlockSpec((tm, 2), lambda i: (i, 0)),
                  vec, vec,
                  pl.BlockSpec(memory_space=pl.ANY)],
        out_specs=[row, row],
        out_shape=[jax.ShapeDtypeStruct((T, D), F32), jax.ShapeDtypeStruct((T, D), BF16)],
        scratch_shapes=[pltpu.VMEM((2, tm, D // 2), jnp.uint32), pltpu.SemaphoreType.DMA((2,))],
        compiler_params=_params(("arbitrary",)),
        name="combine_ln",
    )(pos, h1, wcol, g.reshape(1, D), b.reshape(1, D), y_sorted)


EXPERT_TILE = 256
ROUTE_TILE = 256


def _routing_plan(route_i, counts, T, tile, route_tile):
    cnt = counts[:, 0].astype(jnp.int32)
    padded = ((cnt + tile - 1) // tile) * tile
    ends = jnp.cumsum(padded)
    offs = ends - padded
    e0, e1, r0, r1 = route_i[0], route_i[1], route_i[2], route_i[3]
    pos0 = offs[e0] + r0
    pos1 = offs[e1] + r1
    nrt = T // route_tile
    pos = jnp.concatenate([pos0.reshape(nrt, 1, route_tile), pos1.reshape(nrt, 1, route_tile)], axis=-1)
    n_tiles = (2 * T) // tile + N_EXPERTS
    tile_ids = jnp.arange(n_tiles, dtype=jnp.int32)
    tile_expert = jnp.minimum(
        jnp.sum((ends[None, :] // tile <= tile_ids[:, None]).astype(jnp.int32), axis=1),
        N_EXPERTS - 1).astype(jnp.int32)
    n_valid = (ends[-1] // tile).astype(jnp.int32).reshape(1)
    return pos, tile_expert, n_valid, (offs + cnt).astype(jnp.int32), (padded - cnt).astype(jnp.int32)


def _hgrn_lower_bounds(hgrn_lb):
    lb = jnp.cumsum(jax.nn.softmax(hgrn_lb.astype(F32), axis=0), axis=0)
    return lb - lb[0:1]


def kernel(x, emb_ln_g, emb_ln_b, w_in, attn_sink, conv_w, hgrn_lb, hgrn_norm_g, ret_decay_logit,
           ret_norm_g, w_out, ln1_g, ln1_b, router_w, router_b, w_gate, w_up, w_down, ln2_g, ln2_b):
    B, S, D = x.shape
    T = B * S
    depth = w_in.shape[0]
    lb_all = _hgrn_lower_bounds(hgrn_lb)
    route_tile = min(ROUTE_TILE, T)
    n_rows = 2 * T + N_EXPERTS * EXPERT_TILE

    w_out_b, w_gate_b, w_up_b, w_down_b = (w.astype(BF16) for w in (w_out, w_gate, w_up, w_down))
    h, hb = embed_ln(x.reshape(T, D), emb_ln_g, emb_ln_b)
    for l in range(depth):
        proj = in_proj(hb, w_in, l).reshape(B, S, D_IN_PROJ)
        ys = [attention(proj, attn_sink[l]),
              short_conv(proj, conv_w[l]),
              hgrn2(proj, lb_all[l], hgrn_norm_g[l]),
              retention(proj, ret_decay_logit[l], ret_norm_g[l])]
        h1, h1_packed, route_i, route_w, counts = out_proj_ln_route(
            ys, w_out_b, l, h, ln1_g[l], ln1_b[l], router_w, router_b)
        pos, tile_expert, n_valid, pad_start, pad_len = _routing_plan(
            route_i, counts, T, EXPERT_TILE, route_tile)
        x_sorted = dispatch(h1_packed, pos, pad_start, pad_len, n_valid, n_rows, EXPERT_TILE,
                            tm=route_tile)
        y_sorted = expert_ffn(x_sorted, tile_expert, n_valid, w_gate_b, w_up_b, w_down_b, l,
                              EXPERT_TILE)
        h, hb = combine_ln(h1, y_sorted, pos, route_w[0:2].T, ln2_g[l], ln2_b[l], tm=route_tile)
    return h.reshape(B, S, D)
```

```python
import functools

import jax
import jax.numpy as jnp
from jax import lax
from jax.experimental import pallas as pl
from jax.experimental.pallas import tpu as pltpu

F32 = jnp.float32
BF16 = jnp.bfloat16

D_MODEL = 2048
DEPTH = 2
W_GROUP = 512
HEAD_DIM = 64
N_ATTN_HEADS = 8
N_KV_HEADS = 2
ATTN_GROUP = N_ATTN_HEADS // N_KV_HEADS
WINDOW = 128
ATTN_BLOCK = 128
ATTN_STACK = 4
REC_HEADS = 4
REC_DIM = 128
N_EXPERTS = 16
N_GROUPS = 4
EXPERTS_PER_GROUP = 4
D_EXPERT = 1024
D_IN_PROJ = 6912
DEEPNORM_ALPHA = (2.0 * DEPTH) ** 0.25
LN_EPS = 1e-5
HEAD_NORM_EPS = 1e-6
NEG_BIG = -1e30

LANES = 128
SUBLANES = 8
COL_AQ, COL_AK, COL_AV = 0, 4, 5
COL_CB, COL_CC, COL_CH = 6, 10, 14
COL_GQ, COL_GZF, COL_GZB, COL_GI, COL_GO = 18, 22, 26, 30, 34
COL_RQ, COL_RK, COL_RV, COL_RG = 38, 42, 46, 50

CHUNK = 128
VMEM_LIMIT = 56 * 1024 * 1024

_NT = (((1,), (1,)), ((), ()))
_TN = (((0,), (0,)), ((), ()))


def _params(sem, vmem=VMEM_LIMIT):
    return pltpu.CompilerParams(dimension_semantics=sem, vmem_limit_bytes=vmem)


def _layer_norm(x, g, b):
    mu = jnp.mean(x, axis=-1, keepdims=True)
    xc = x - mu
    var = jnp.mean(xc * xc, axis=-1, keepdims=True)
    return xc * lax.rsqrt(var + LN_EPS) * g + b


def _silu(x):
    return x * (1.0 / (1.0 + jnp.exp(-x)))


def _pack_rows(x):
    n = x.shape[1] // 2
    hi = lax.bitcast_convert_type(x[:, :n].astype(BF16).astype(F32), jnp.uint32)
    lo = lax.bitcast_convert_type(x[:, n:].astype(BF16).astype(F32), jnp.uint32)
    return hi | (lo >> 16)


def _unpack_rows(w):
    hi = lax.bitcast_convert_type(w & jnp.uint32(0xFFFF0000), F32)
    lo = lax.bitcast_convert_type(w << 16, F32)
    return jnp.concatenate([hi, lo], axis=-1)


def _embed_ln_kernel(x_ref, g_ref, b_ref, h_ref, hb_ref):
    h = _layer_norm(x_ref[...], g_ref[...], b_ref[...])
    h_ref[...] = h
    hb_ref[...] = h.astype(BF16)


def embed_ln(x2, g, b, tm=512):
    T, D = x2.shape
    return pl.pallas_call(
        _embed_ln_kernel,
        grid=(T // tm,),
        in_specs=[pl.BlockSpec((tm, D), lambda i: (i, 0)),
                  pl.BlockSpec((1, D), lambda i: (0, 0)),
                  pl.BlockSpec((1, D), lambda i: (0, 0))],
        out_specs=[pl.BlockSpec((tm, D), lambda i: (i, 0)),
                   pl.BlockSpec((tm, D), lambda i: (i, 0))],
        out_shape=[jax.ShapeDtypeStruct((T, D), F32), jax.ShapeDtypeStruct((T, D), BF16)],
        compiler_params=_params(("parallel",)),
        name="embed_ln",
    )(x2, g.reshape(1, D), b.reshape(1, D))


def _in_proj_kernel(x_ref, w_ref, o_ref, wb_ref):
    @pl.when(pl.program_id(1) == 0)
    def _():
        wb_ref[...] = w_ref[...].astype(BF16)

    o_ref[...] = jnp.dot(x_ref[...], wb_ref[...], preferred_element_type=F32)


def in_proj(hb, w_in, layer, tm=1024, tn=768):
    T, K = hb.shape
    N = w_in.shape[2]
    tm = min(tm, T)
    return pl.pallas_call(
        _in_proj_kernel,
        grid=(N // tn, T // tm),
        in_specs=[pl.BlockSpec((tm, K), lambda n, m: (m, 0)),
                  pl.BlockSpec((None, K, tn), lambda n, m: (layer, 0, n))],
        out_specs=pl.BlockSpec((tm, tn), lambda n, m: (m, n)),
        out_shape=jax.ShapeDtypeStruct((T, N), F32),
        scratch_shapes=[pltpu.VMEM((K, tn), BF16)],
        compiler_params=_params(("arbitrary", "arbitrary")),
        name="in_proj",
    )(hb, w_in)


def _attn_kernel(sink_ref, q_ref, kp_ref, kc_ref, kn_ref, vp_ref, vc_ref, vn_ref, bias_ref, o_ref):
    L = ATTN_BLOCK
    n = pl.program_id(1)
    nb = pl.num_programs(1)
    key = lax.broadcasted_iota(jnp.int32, (3 * L, 1), 0)
    valid = ((key >= L) | (n > 0)) & ((key < 2 * L) | (n < nb - 1))
    edge = jnp.where(valid, 0.0, NEG_BIG)
    q = q_ref[0]
    k3 = jnp.concatenate([kp_ref[0], kc_ref[0], kn_ref[0]], axis=0)
    v3 = jnp.concatenate([vp_ref[0], vc_ref[0], vn_ref[0]], axis=0)
    outs = []
    for h in range(N_KV_HEADS):
        kh = k3[:, h * HEAD_DIM:(h + 1) * HEAD_DIM].astype(BF16)
        vh = v3[:, h * HEAD_DIM:(h + 1) * HEAD_DIM].astype(BF16)
        for g0 in range(0, ATTN_GROUP, ATTN_STACK):
            heads = range(h * ATTN_GROUP + g0, h * ATTN_GROUP + g0 + ATTN_STACK)
            qs = jnp.concatenate([q[:, hd * HEAD_DIM:(hd + 1) * HEAD_DIM] for hd in heads], axis=0)
            qs = (qs * (HEAD_DIM ** -0.5)).astype(BF16)
            s = lax.dot_general(kh, qs, _NT, preferred_element_type=F32)
            s = s + bias_ref[h, :, g0 * L:(g0 + ATTN_STACK) * L] + edge
            sk = jnp.concatenate([jnp.full((1, L), sink_ref[hd], F32) for hd in heads], axis=1)
            m = jnp.maximum(jnp.max(s, axis=0, keepdims=True), sk)
            p = jnp.exp(s - m)
            den = jnp.sum(p, axis=0, keepdims=True) + jnp.exp(sk - m)
            o_t = lax.dot_general(vh, p.astype(BF16), _TN, preferred_element_type=F32) / den
            outs.extend(o_t[:, g * L:(g + 1) * L].T for g in range(ATTN_STACK))
    o_ref[0] = jnp.concatenate(outs, axis=-1).astype(o_ref.dtype)


def _attn_bias():
    L = ATTN_BLOCK
    k_rel = jnp.arange(3 * L) - L
    dist = jnp.abs(k_rel[None, :] - jnp.arange(L)[:, None]).astype(F32)
    slopes = 2.0 ** (-8.0 * jnp.arange(1, N_ATTN_HEADS + 1, dtype=F32) / N_ATTN_HEADS)
    bias = -slopes[:, None, None] * dist[None]
    bias = jnp.where(dist[None] <= WINDOW, bias, NEG_BIG)
    bias = bias.reshape(N_KV_HEADS, ATTN_GROUP, L, 3 * L).transpose(0, 3, 1, 2)
    return bias.reshape(N_KV_HEADS, 3 * L, ATTN_GROUP * L)


def attention(proj, sink):
    B, S, _ = proj.shape
    L = ATTN_BLOCK
    nb = S // L
    kv = lambda col, shift: pl.BlockSpec(
        (1, L, LANES), lambda b, n: (b, jnp.clip(n + shift, 0, nb - 1), col))
    return pl.pallas_call(
        _attn_kernel,
        grid=(B, nb),
        in_specs=[pl.BlockSpec(memory_space=pltpu.SMEM),
                  pl.BlockSpec((1, L, W_GROUP), lambda b, n: (b, n, COL_AQ // 4)),
                  kv(COL_AK, -1), kv(COL_AK, 0), kv(COL_AK, 1),
                  kv(COL_AV, -1), kv(COL_AV, 0), kv(COL_AV, 1),
                  pl.BlockSpec((N_KV_HEADS, 3 * L, ATTN_GROUP * L), lambda b, n: (0, 0, 0))],
        out_specs=pl.BlockSpec((1, L, W_GROUP), lambda b, n: (b, n, 0)),
        out_shape=jax.ShapeDtypeStruct((B, S, W_GROUP), BF16),
        compiler_params=_params(("parallel", "arbitrary")),
        name="attention",
    )(sink.astype(F32), proj, proj, proj, proj, proj, proj, proj, _attn_bias())


CONV_ROWS = 512
HALO = 8


def _conv_kernel(b_ref, c_ref, h_ref, w_ref, o_ref, u_ref):
    S = b_ref.shape[1]
    R = min(CONV_ROWS, S)
    u_ref[0:HALO, :] = jnp.zeros((HALO, LANES), F32)
    u_ref[S + HALO:S + 2 * HALO, :] = jnp.zeros((HALO, LANES), F32)

    def gate(i, carry):
        r = pl.multiple_of(i * R, R)
        u_ref[pl.ds(r + HALO, R), :] = c_ref[0, pl.ds(r, R), :] * h_ref[0, pl.ds(r, R), :]
        return carry

    lax.fori_loop(0, S // R, gate, 0)
    w0, w1, w2 = w_ref[0:1, :], w_ref[1:2, :], w_ref[2:3, :]

    def conv(i, carry):
        r = pl.multiple_of(i * R, R)
        a = u_ref[pl.ds(r, R + 2 * HALO), :]
        prev = pltpu.roll(a, 1, 0)[HALO:HALO + R]
        nxt = pltpu.roll(a, R + 2 * HALO - 1, 0)[HALO:HALO + R]
        y = w0 * prev + w1 * a[HALO:HALO + R] + w2 * nxt
        o_ref[0, pl.ds(r, R), :] = (b_ref[0, pl.ds(r, R), :] * y).astype(o_ref.dtype)
        return carry

    lax.fori_loop(0, S // R, conv, 0)


def short_conv(proj, conv_w):
    B, S, _ = proj.shape
    nj = W_GROUP // LANES
    col = lambda c: pl.BlockSpec((1, S, LANES), lambda b, j: (b, 0, c + j))
    return pl.pallas_call(
        _conv_kernel,
        grid=(B, nj),
        in_specs=[col(COL_CB), col(COL_CC), col(COL_CH),
                  pl.BlockSpec((3, LANES), lambda b, j: (0, j))],
        out_specs=pl.BlockSpec((1, S, LANES), lambda b, j: (b, 0, j)),
        out_shape=jax.ShapeDtypeStruct((B, S, W_GROUP), BF16),
        scratch_shapes=[pltpu.VMEM((S + 2 * HALO, LANES), F32)],
        compiler_params=_params(("parallel", "parallel")),
        name="short_conv",
    )(proj, proj, proj, conv_w.astype(F32))


def _level_codes():
    t = lax.broadcasted_iota(jnp.int32, (CHUNK, CHUNK), 0)
    s = lax.broadcasted_iota(jnp.int32, (CHUNK, CHUNK), 1)
    x = t ^ s
    hb = jnp.zeros((CHUNK, CHUNK), jnp.int32)
    c = 1
    while c < CHUNK:
        hb = jnp.where((x & c) != 0, c, hb)
        c *= 2
    diag = jnp.where(t == s, 0, -1)
    return jnp.where(t > s, hb, diag), jnp.where(t < s, hb, diag)


def _hgrn_chunk(q, v, z, lb, code, reverse):
    e = jnp.exp(-jnp.abs(z))
    r = 1.0 / (1.0 + e)
    er = e * r
    nonneg = z >= 0
    f = lb + (1.0 - lb) * jnp.where(nonneg, r, er)
    k = (1.0 - lb) * jnp.where(nonneg, er, r)
    row = lax.broadcasted_iota(jnp.int32, (CHUNK, REC_DIM), 0)
    qs = f
    ks = jnp.ones_like(f)
    blk = f
    a = jnp.where(code == 0,
                  lax.dot_general(q.astype(BF16), k.astype(BF16), _NT, preferred_element_type=F32),
                  0.0)
    c = 1
    while c < CHUNK:
        p = lax.dot_general((q * qs).astype(BF16), (k * ks).astype(BF16), _NT,
                            preferred_element_type=F32)
        a = jnp.where(code == c, p, a)
        if c < SUBLANES:
            upper = (row & c) != 0
            grouped = blk.reshape(CHUNK // SUBLANES, SUBLANES, REC_DIM)
            down = pltpu.roll(grouped, c, 1).reshape(CHUNK, REC_DIM)
            up = (down if 2 * c == SUBLANES
                  else pltpu.roll(grouped, SUBLANES - c, 1).reshape(CHUNK, REC_DIM))
            sib = jnp.where(upper, down, up)
            grow_q = jnp.logical_not(upper) if reverse else upper
            qs = qs * jnp.where(grow_q, sib, 1.0)
            ks = ks * jnp.where(grow_q, 1.0, sib)
            blk = blk * sib
        else:
            step = c // SUBLANES
            group = lambda x, j: x[j * SUBLANES:(j + 1) * SUBLANES]
            nq, nk, nb = [], [], []
            for j in range(CHUNK // SUBLANES):
                upper = (j & step) != 0
                sib = group(blk, j ^ step)
                grow_q = (not upper) if reverse else upper
                nq.append(group(qs, j) * sib if grow_q else group(qs, j))
                nk.append(group(ks, j) if grow_q else group(ks, j) * sib)
                nb.append(group(blk, j) * sib)
            qs, ks, blk = (jnp.concatenate(x, axis=0) for x in (nq, nk, nb))
        c *= 2
    return a, q * qs, k * ks, blk[0:1, :]


def _hgrn_kernel(q_ref, zf_ref, zb_ref, i_ref, g_ref, lb_ref, ng_ref, o_ref, acc_ref, code_ref,
                 stf_ref, stb_ref):
    S = q_ref.shape[1]
    nc = S // CHUNK
    lb = lb_ref[0]
    cf, cb = _level_codes()
    code_ref[0] = cf
    code_ref[1] = cb
    stf_ref[...] = jnp.zeros_like(stf_ref)
    stb_ref[...] = jnp.zeros_like(stb_ref)

    def part(ci, reverse):
        z_ref, st = (zb_ref, stb_ref) if reverse else (zf_ref, stf_ref)
        rows = pl.ds(pl.multiple_of(ci * CHUNK, CHUNK), CHUNK)
        q = q_ref[0, rows, :]
        v = i_ref[0, rows, :]
        vb = v.astype(BF16)
        a, qd, kd, dec = _hgrn_chunk(q, v, z_ref[0, rows, :], lb,
                                     code_ref[1 if reverse else 0], reverse)
        o = jnp.dot(a.astype(BF16), vb, preferred_element_type=F32)
        o = o + lax.dot_general(qd.astype(BF16), st[...].astype(BF16), _NT,
                                preferred_element_type=F32)
        st[...] = st[...] * dec + lax.dot_general(vb, kd.astype(BF16), _TN,
                                                  preferred_element_type=F32)
        return rows, o

    def finish(rows, o):
        o = o + acc_ref[rows, :]
        y = o * lax.rsqrt(jnp.mean(o * o, axis=-1, keepdims=True) + HEAD_NORM_EPS)
        y = y * ng_ref[0] * _silu(g_ref[0, rows, :])
        o_ref[0, rows, :] = y.astype(o_ref.dtype)

    def first_half(j, carry):
        rows, o = part(j, False)
        acc_ref[rows, :] = o
        rows, o = part(nc - 1 - j, True)
        acc_ref[rows, :] = o
        return carry

    def second_half(j, carry):
        finish(*part(j, False))
        finish(*part(nc - 1 - j, True))
        return carry

    lax.fori_loop(0, nc // 2, first_half, 0)
    lax.fori_loop(nc // 2, nc, second_half, 0)


def hgrn2(proj, lb, norm_g):
    B, S, _ = proj.shape
    assert (S // CHUNK) % 2 == 0
    col = lambda c: pl.BlockSpec((1, S, LANES), lambda b, h: (b, 0, c + h))
    vec = pl.BlockSpec((1, 1, LANES), lambda b, h: (h, 0, 0))
    return pl.pallas_call(
        _hgrn_kernel,
        grid=(B, REC_HEADS),
        in_specs=[col(COL_GQ), col(COL_GZF), col(COL_GZB), col(COL_GI), col(COL_GO), vec, vec],
        out_specs=pl.BlockSpec((1, S, LANES), lambda b, h: (b, 0, h)),
        out_shape=jax.ShapeDtypeStruct((B, S, W_GROUP), BF16),
        scratch_shapes=[pltpu.VMEM((S, REC_DIM), F32),
                        pltpu.VMEM((2, CHUNK, CHUNK), jnp.int32),
                        pltpu.VMEM((REC_DIM, REC_DIM), F32),
                        pltpu.VMEM((REC_DIM, REC_DIM), F32)],
        compiler_params=_params(("parallel", "parallel")),
        name="hgrn2",
    )(proj, proj, proj, proj, proj,
      lb.astype(F32).reshape(REC_HEADS, 1, LANES), norm_g.astype(F32).reshape(REC_HEADS, 1, LANES))


RET_HEADS_PER_STEP = 2


def _retention_kernel(dl_ref, q_ref, k_ref, v_ref, g_ref, ng_ref, o_ref, const_ref, *scratch):
    S = q_ref.shape[1]
    nc = S // CHUNK
    nh = RET_HEADS_PER_STEP
    acc_refs, st_refs = scratch[:nh], scratch[nh:]
    scale = REC_DIM ** -0.5
    t = lax.broadcasted_iota(jnp.int32, (CHUNK, CHUNK), 0)
    s = lax.broadcasted_iota(jnp.int32, (CHUNK, CHUNK), 1)
    rel = (t - s).astype(F32)
    pos = lax.broadcasted_iota(jnp.int32, (CHUNK, REC_DIM), 0).astype(F32)
    chunk_decay = []
    for hh in range(nh):
        head = pl.program_id(1) * nh + hh

        def log_gamma(d):
            x = jnp.full((1, LANES), dl_ref[d, head], F32)
            return jnp.minimum(x, 0.0) - jnp.log1p(jnp.exp(-jnp.abs(x)))

        lgf, lgb = log_gamma(0), log_gamma(1)
        const_ref[hh, 0] = (jnp.where(t >= s, jnp.exp(lgf * rel), 0.0)
                            + jnp.where(s >= t, jnp.exp(-lgb * rel), 0.0)) * scale
        const_ref[hh, 1] = jnp.exp(lgf * (pos + 1.0))
        const_ref[hh, 2] = jnp.exp(lgf * (CHUNK - 1.0 - pos)) * scale
        const_ref[hh, 3] = jnp.exp(lgb * (CHUNK - pos))
        const_ref[hh, 4] = jnp.exp(lgb * pos) * scale
        chunk_decay.append((jnp.exp(lgf * CHUNK), jnp.exp(lgb * CHUNK)))
    for st in st_refs:
        st[...] = jnp.zeros_like(st)

    def load(ci, hh):
        rows = pl.ds(pl.multiple_of(ci * CHUNK, CHUNK), CHUNK)
        lanes = slice(hh * LANES, (hh + 1) * LANES)
        return rows, lanes, q_ref[0, rows, lanes], k_ref[0, rows, lanes], v_ref[0, rows, lanes].astype(BF16)

    def fwd_part(ci, hh):
        rows, lanes, q, k, vb = load(ci, hh)
        st = st_refs[2 * hh]
        a = lax.dot_general(q.astype(BF16), k.astype(BF16), _NT, preferred_element_type=F32)
        o = jnp.dot((a * const_ref[hh, 0]).astype(BF16), vb, preferred_element_type=F32)
        o = o + lax.dot_general((q * const_ref[hh, 1]).astype(BF16), st[...].astype(BF16), _NT,
                                preferred_element_type=F32)
        st[...] = st[...] * chunk_decay[hh][0] + lax.dot_general(
            vb, (k * const_ref[hh, 2]).astype(BF16), _TN, preferred_element_type=F32)
        return rows, lanes, o

    def bwd_part(ci, hh):
        rows, lanes, q, k, vb = load(ci, hh)
        st = st_refs[2 * hh + 1]
        o = lax.dot_general((q * const_ref[hh, 3]).astype(BF16), st[...].astype(BF16), _NT,
                            preferred_element_type=F32)
        st[...] = st[...] * chunk_decay[hh][1] + lax.dot_general(
            vb, (k * const_ref[hh, 4]).astype(BF16), _TN, preferred_element_type=F32)
        return rows, lanes, o

    def finish(rows, lanes, hh, o):
        o = o + acc_refs[hh][rows, :]
        oc = o - jnp.mean(o, axis=-1, keepdims=True)
        y = oc * lax.rsqrt(jnp.mean(oc * oc, axis=-1, keepdims=True) + HEAD_NORM_EPS)
        o_ref[0, rows, lanes] = (y * ng_ref[hh] * _silu(g_ref[0, rows, lanes])).astype(o_ref.dtype)

    def first_half(j, carry):
        for hh in range(nh):
            rows, _, o = fwd_part(j, hh)
            acc_refs[hh][rows, :] = o
            rows, _, o = bwd_part(nc - 1 - j, hh)
            acc_refs[hh][rows, :] = o
        return carry

    def second_half(j, carry):
        for hh in range(nh):
            rows, lanes, o = fwd_part(j, hh)
            finish(rows, lanes, hh, o)
            rows, lanes, o = bwd_part(nc - 1 - j, hh)
            finish(rows, lanes, hh, o)
        return carry

    lax.fori_loop(0, nc // 2, first_half, 0)
    lax.fori_loop(nc // 2, nc, second_half, 0)


def retention(proj, decay_logit, norm_g):
    B, S, _ = proj.shape
    nh = RET_HEADS_PER_STEP
    assert (S // CHUNK) % 2 == 0 and REC_HEADS % nh == 0
    col = lambda c: pl.BlockSpec((1, S, nh * LANES), lambda b, h: (b, 0, c // nh + h))
    return pl.pallas_call(
        _retention_kernel,
        grid=(B, REC_HEADS // nh),
        in_specs=[pl.BlockSpec(memory_space=pltpu.SMEM),
                  col(COL_RQ), col(COL_RK), col(COL_RV), col(COL_RG),
                  pl.BlockSpec((nh, 1, LANES), lambda b, h: (h, 0, 0))],
        out_specs=pl.BlockSpec((1, S, nh * LANES), lambda b, h: (b, 0, h)),
        out_shape=jax.ShapeDtypeStruct((B, S, W_GROUP), BF16),
        scratch_shapes=([pltpu.VMEM((nh, 5, CHUNK, REC_DIM), F32)]
                        + [pltpu.VMEM((S, REC_DIM), F32)] * nh
                        + [pltpu.VMEM((REC_DIM, REC_DIM), F32)] * (2 * nh)),
        compiler_params=_params(("parallel", "parallel")),
        name="retention",
    )(decay_logit.astype(F32), proj, proj, proj, proj,
      norm_g.astype(F32).reshape(REC_HEADS, 1, LANES))


ROUTE_ROWS = 8

def _first_max(vals):
    best, idx = vals[0], jnp.zeros(vals[0].shape, jnp.int32)
    for i in range(1, len(vals)):
        take = vals[i] > best
        best = jnp.where(take, vals[i], best)
        idx = jnp.where(take, i, idx)
    return best, idx


def _pick(idx, vals):
    out = vals[-1]
    for i in range(len(vals) - 2, -1, -1):
        out = jnp.where(idx == i, vals[i], out)
    return out


def _out_proj_kernel(ya_ref, yc_ref, yg_ref, yr_ref, w_ref, h_ref, g_ref, b_ref, rwh_ref, rwl_ref,
                     rb_ref, h1_ref, h1p_ref, ri_ref, rwt_ref, cnt_ref, carry_ref, ycat_ref):
    tm = h_ref.shape[0]

    @pl.when(pl.program_id(0) == 0)
    def _():
        carry_ref[...] = jnp.zeros_like(carry_ref)

    ycat_ref[:, 0:W_GROUP] = ya_ref[...]
    ycat_ref[:, W_GROUP:2 * W_GROUP] = yc_ref[...]
    ycat_ref[:, 2 * W_GROUP:3 * W_GROUP] = yg_ref[...]
    ycat_ref[:, 3 * W_GROUP:4 * W_GROUP] = yr_ref[...]
    mix = jnp.dot(ycat_ref[...], w_ref[...], preferred_element_type=F32)
    h1 = _layer_norm(DEEPNORM_ALPHA * h_ref[...] + mix, g_ref[...], b_ref[...])
    h1_ref[...] = h1
    h1p_ref[...] = _pack_rows(h1)

    h1_hi = h1.astype(BF16)
    h1_lo = (h1 - h1_hi.astype(F32)).astype(BF16)
    logits = (lax.dot_general(rwh_ref[...], h1_hi, _NT, preferred_element_type=F32)
              + lax.dot_general(rwh_ref[...], h1_lo, _NT, preferred_element_type=F32)
              + lax.dot_general(rwl_ref[...], h1_hi, _NT, preferred_element_type=F32)
              + rb_ref[...])
    rows = [logits[e:e + 1, :] for e in range(N_EXPERTS)]
    mx = functools.reduce(jnp.maximum, rows)
    ex = [jnp.exp(r - mx) for r in rows]
    den = functools.reduce(jnp.add, ex)
    pr = [x / den for x in ex]
    scores = []
    for g in range(N_GROUPS):
        a, b, c, d = pr[4 * g:4 * g + 4]
        hi1, lo1, hi2, lo2 = jnp.maximum(a, b), jnp.minimum(a, b), jnp.maximum(c, d), jnp.minimum(c, d)
        scores.append(jnp.maximum(hi1, hi2) + jnp.maximum(jnp.minimum(hi1, hi2), jnp.maximum(lo1, lo2)))
    _, gsel = _first_max(scores)
    cand = [_pick(gsel, [pr[4 * g + i] for g in range(N_GROUPS)]) for i in range(EXPERTS_PER_GROUP)]
    p0, i0 = _first_max(cand)
    p1, i1 = _first_max([jnp.where(i0 == i, -1.0, cand[i]) for i in range(EXPERTS_PER_GROUP)])
    e0 = gsel * EXPERTS_PER_GROUP + i0
    e1 = gsel * EXPERTS_PER_GROUP + i1
    tot = p0 + p1
    w0, w1 = p0 / tot, p1 / tot

    ind = jnp.concatenate([((e0 == e) | (e1 == e)).astype(F32) for e in range(N_EXPERTS)], axis=0)
    before = (lax.broadcasted_iota(jnp.int32, (tm, tm), 0)
              < lax.broadcasted_iota(jnp.int32, (tm, tm), 1)).astype(BF16)
    rank = jnp.dot(ind.astype(BF16), before, preferred_element_type=F32) + carry_ref[...]
    carry_ref[...] = carry_ref[...] + jnp.sum(ind, axis=-1, keepdims=True)
    rk = [rank[e:e + 1, :] for e in range(N_EXPERTS)]
    r0 = _pick(e0, rk).astype(jnp.int32)
    r1 = _pick(e1, rk).astype(jnp.int32)
    zi = jnp.zeros((ROUTE_ROWS - 4, tm), jnp.int32)
    ri_ref[...] = jnp.concatenate([e0, e1, r0, r1, zi], axis=0)
    rwt_ref[...] = jnp.concatenate([w0, w1, jnp.zeros((ROUTE_ROWS - 2, tm), F32)], axis=0)
    cnt_ref[...] = jnp.broadcast_to(carry_ref[...], cnt_ref.shape)


def out_proj_ln_route(ys, w_out_bf16, layer, h, g, b, router_w, router_b, tm=512):
    T, D = h.shape
    tm = min(tm, T)
    part = pl.BlockSpec((tm, W_GROUP), lambda i: (i, 0))
    row = pl.BlockSpec((tm, D), lambda i: (i, 0))
    vec = pl.BlockSpec((1, D), lambda i: (0, 0))
    route = pl.BlockSpec((ROUTE_ROWS, tm), lambda i: (0, i))
    once = pl.Buffered(1)
    rw_t = router_w.astype(F32).T
    rw_hi = rw_t.astype(BF16)
    rw_lo = (rw_t - rw_hi.astype(F32)).astype(BF16)
    return pl.pallas_call(
        _out_proj_kernel,
        grid=(T // tm,),
        in_specs=[part, part, part, part,
                  pl.BlockSpec((None, D, D), lambda i: (layer, 0, 0), pipeline_mode=once),
                  row, vec, vec,
                  pl.BlockSpec((N_EXPERTS, D), lambda i: (0, 0), pipeline_mode=once),
                  pl.BlockSpec((N_EXPERTS, D), lambda i: (0, 0), pipeline_mode=once),
                  pl.BlockSpec((N_EXPERTS, 1), lambda i: (0, 0))],
        out_specs=[row, pl.BlockSpec((tm, D // 2), lambda i: (i, 0)), route, route,
                   pl.BlockSpec((N_EXPERTS, LANES), lambda i: (0, 0))],
        out_shape=[jax.ShapeDtypeStruct((T, D), F32),
                   jax.ShapeDtypeStruct((T, D // 2), jnp.uint32),
                   jax.ShapeDtypeStruct((ROUTE_ROWS, T), jnp.int32),
                   jax.ShapeDtypeStruct((ROUTE_ROWS, T), F32),
                   jax.ShapeDtypeStruct((N_EXPERTS, LANES), F32)],
        scratch_shapes=[pltpu.VMEM((N_EXPERTS, 1), F32), pltpu.VMEM((tm, D), BF16)],
        compiler_params=_params(("arbitrary",)),
        name="out_proj_ln_route",
    )(*[y.reshape(T, W_GROUP) for y in ys], w_out_bf16, h, g.reshape(1, D), b.reshape(1, D),
      rw_hi, rw_lo, router_b.astype(F32).reshape(N_EXPERTS, 1))


ROW_DMA_UNROLL = 4


def _dispatch_kernel(pad_start_ref, pad_len_ref, nv_ref, pos_ref, h_ref, x_hbm, zero_ref, sems, zsem):
    tm = h_ref.shape[0]
    tile = zero_ref.shape[0]
    n_tiles = x_hbm.shape[0] // tile

    def issue(i, carry):
        for u in range(ROW_DMA_UNROLL):
            r = i * ROW_DMA_UNROLL + u
            for slot in range(2):
                pltpu.make_async_copy(h_ref.at[pl.ds(r, 1), :],
                                      x_hbm.at[pl.ds(pos_ref[0, 0, slot * tm + r], 1), :],
                                      sems.at[slot]).start(priority=slot)
        return carry

    lax.fori_loop(0, tm // ROW_DMA_UNROLL, issue, 0)

    @pl.when(pl.program_id(0) == 0)
    def _():
        zero_ref[...] = jnp.zeros_like(zero_ref)

        def tcopy(i):
            return pltpu.make_async_copy(
                zero_ref, x_hbm.at[pl.ds(pl.multiple_of(i * tile, tile), tile), :], zsem)

        def tissue(i, carry):
            tcopy(i).start()
            return carry

        def twait(i, carry):
            tcopy(i).wait()
            return carry

        lax.fori_loop(nv_ref[0], n_tiles, tissue, 0)
        lax.fori_loop(nv_ref[0], n_tiles, twait, 0)
        for e in range(N_EXPERTS):
            def zcopy(i):
                return pltpu.make_async_copy(
                    zero_ref.at[pl.ds(0, 1), :], x_hbm.at[pl.ds(pad_start_ref[e] + i, 1), :], zsem)

            def zissue(i, carry):
                zcopy(i).start()
                return carry

            def zwait(i, carry):
                zcopy(i).wait()
                return carry

            lax.fori_loop(0, pad_len_ref[e], zissue, 0)
            lax.fori_loop(0, pad_len_ref[e], zwait, 0)

    for slot in range(2):
        pltpu.make_async_copy(h_ref, x_hbm.at[pl.ds(0, tm), :], sems.at[slot]).wait()


def dispatch(h1, pos, pad_start, pad_len, n_valid, n_rows, tile, tm=256):
    T, D = h1.shape
    tm = min(tm, T)
    return pl.pallas_call(
        _dispatch_kernel,
        grid_spec=pltpu.PrefetchScalarGridSpec(
            num_scalar_prefetch=3,
            grid=(T // tm,),
            in_specs=[pl.BlockSpec((1, 1, 2 * tm), lambda i, *_: (i, 0, 0), memory_space=pltpu.SMEM),
                      pl.BlockSpec((tm, D), lambda i, *_: (i, 0))],
            out_specs=pl.BlockSpec(memory_space=pl.ANY),
            scratch_shapes=[pltpu.VMEM((tile, D), h1.dtype), pltpu.SemaphoreType.DMA((2,)),
                            pltpu.SemaphoreType.DMA]),
        out_shape=jax.ShapeDtypeStruct((n_rows, D), h1.dtype),
        compiler_params=_params(("arbitrary",)),
        name="dispatch",
    )(pad_start, pad_len, n_valid, pos, h1)


def _expert_kernel(te_ref, nv_ref, x_ref, wg_ref, wu_ref, wd_ref, y_ref):
    @pl.when(pl.program_id(0) < nv_ref[0])
    def _():
        x = _unpack_rows(x_ref[...]).astype(BF16)
        gate = jnp.dot(x, wg_ref[...], preferred_element_type=F32)
        up = jnp.dot(x, wu_ref[...], preferred_element_type=F32)
        hid = (_silu(gate) * up).astype(BF16)
        y_ref[...] = _pack_rows(jnp.dot(hid, wd_ref[...], preferred_element_type=F32))

    @pl.when(pl.program_id(0) >= nv_ref[0])
    def _():
        y_ref[...] = jnp.zeros_like(y_ref)


def expert_ffn(x_sorted, tile_expert, n_valid, w_gate, w_up, w_down, layer, tm):
    A, DP = x_sorted.shape
    D = 2 * DP
    n_tiles = A // tm
    tile = lambda i, te, nv: (jnp.minimum(i, nv[0] - 1), 0)
    out_tile = lambda i, te, nv: (i, 0)
    wmap = lambda i, te, nv: (layer, te[jnp.minimum(i, nv[0] - 1)], 0, 0)
    return pl.pallas_call(
        _expert_kernel,
        grid_spec=pltpu.PrefetchScalarGridSpec(
            num_scalar_prefetch=2,
            grid=(n_tiles,),
            in_specs=[pl.BlockSpec((tm, DP), tile),
                      pl.BlockSpec((None, None, D, D_EXPERT), wmap),
                      pl.BlockSpec((None, None, D, D_EXPERT), wmap),
                      pl.BlockSpec((None, None, D_EXPERT, D), wmap)],
            out_specs=pl.BlockSpec((tm, DP), out_tile)),
        out_shape=jax.ShapeDtypeStruct((A, DP), jnp.uint32),
        compiler_params=_params(("arbitrary",)),
        name="expert_ffn",
    )(tile_expert, n_valid, x_sorted, w_gate, w_up, w_down)


def _combine_kernel(pos_ref, pos_next_ref, h_ref, w_ref, g_ref, b_ref, y_hbm, o_ref, ob_ref,
                    buf_ref, sems):
    tm = h_ref.shape[0]
    step = pl.program_id(0)
    cur = step % 2

    def gather(p_ref, half):
        def issue(i, carry):
            for u in range(ROW_DMA_UNROLL):
                r = i * ROW_DMA_UNROLL + u
                for slot in range(2):
                    pltpu.make_async_copy(y_hbm.at[pl.ds(p_ref[0, 0, slot * tm + r], 1), :],
                                          buf_ref.at[half, slot, pl.ds(r, 1), :],
                                          sems.at[half, slot]).start(priority=slot)
            return carry

        lax.fori_loop(0, tm // ROW_DMA_UNROLL, issue, 0)

    @pl.when(step == 0)
    def _():
        gather(pos_ref, 0)

    @pl.when(step + 1 < pl.num_programs(0))
    def _():
        gather(pos_next_ref, 1 - cur)

    for slot in range(2):
        pltpu.make_async_copy(y_hbm.at[pl.ds(0, tm), :], buf_ref.at[cur, slot],
                              sems.at[cur, slot]).wait()
    w = w_ref[...]
    ffn = (w[:, 0:1] * _unpack_rows(buf_ref[cur, 0]) + w[:, 1:2] * _unpack_rows(buf_ref[cur, 1]))
    h2 = _layer_norm(DEEPNORM_ALPHA * h_ref[...] + ffn, g_ref[...], b_ref[...])
    o_ref[...] = h2
    ob_ref[...] = h2.astype(BF16)


def combine_ln(h1, y_sorted, pos, wcol, g, b, tm=256):
    T, D = h1.shape
    tm = min(tm, T)
    row = pl.BlockSpec((tm, D), lambda i: (i, 0))
    vec = pl.BlockSpec((1, D), lambda i: (0, 0))
    last = T // tm - 1
    return pl.pallas_call(
        _combine_kernel,
        grid=(T // tm,),
        in_specs=[pl.BlockSpec((1, 1, 2 * tm), lambda i: (i, 0, 0), memory_space=pltpu.SMEM),
                  pl.BlockSpec((1, 1, 2 * tm), lambda i: (jnp.minimum(i + 1, last), 0, 0),
                               memory_space=pltpu.SMEM),
                  row,
                  pl.BlockSpec((tm, 2), lambda i: (i, 0)),
                  vec, vec,
                  pl.BlockSpec(memory_space=pl.ANY)],
        out_specs=[row, row],
        out_shape=[jax.ShapeDtypeStruct((T, D), F32), jax.ShapeDtypeStruct((T, D), BF16)],
        scratch_shapes=[pltpu.VMEM((2, 2, tm, D // 2), jnp.uint32),
                        pltpu.SemaphoreType.DMA((2, 2))],
        compiler_params=_params(("arbitrary",)),
        name="combine_ln",
    )(pos, pos, h1, wcol, g.reshape(1, D), b.reshape(1, D), y_sorted)


EXPERT_TILE = 256
ROUTE_TILE = 256


def _routing_plan(route_i, counts, T, tile, route_tile):
    cnt = counts[:, 0].astype(jnp.int32)
    padded = ((cnt + tile - 1) // tile) * tile
    ends = jnp.cumsum(padded)
    offs = ends - padded
    e0, e1, r0, r1 = route_i[0], route_i[1], route_i[2], route_i[3]
    pos0 = offs[e0] + r0
    pos1 = offs[e1] + r1
    nrt = T // route_tile
    pos = jnp.concatenate([pos0.reshape(nrt, 1, route_tile), pos1.reshape(nrt, 1, route_tile)], axis=-1)
    n_tiles = (2 * T) // tile + N_EXPERTS
    tile_ids = jnp.arange(n_tiles, dtype=jnp.int32)
    tile_expert = jnp.minimum(
        jnp.sum((ends[None, :] // tile <= tile_ids[:, None]).astype(jnp.int32), axis=1),
        N_EXPERTS - 1).astype(jnp.int32)
    n_valid = (ends[-1] // tile).astype(jnp.int32).reshape(1)
    return pos, tile_expert, n_valid, (offs + cnt).astype(jnp.int32), (padded - cnt).astype(jnp.int32)


def _hgrn_lower_bounds(hgrn_lb):
    lb = jnp.cumsum(jax.nn.softmax(hgrn_lb.astype(F32), axis=0), axis=0)
    return lb - lb[0:1]


def kernel(x, emb_ln_g, emb_ln_b, w_in, attn_sink, conv_w, hgrn_lb, hgrn_norm_g, ret_decay_logit,
           ret_norm_g, w_out, ln1_g, ln1_b, router_w, router_b, w_gate, w_up, w_down, ln2_g, ln2_b):
    B, S, D = x.shape
    T = B * S
    depth = w_in.shape[0]
    lb_all = _hgrn_lower_bounds(hgrn_lb)
    route_tile = min(ROUTE_TILE, T)
    n_rows = 2 * T + N_EXPERTS * EXPERT_TILE

    w_out_b, w_gate_b, w_up_b, w_down_b = (w.astype(BF16) for w in (w_out, w_gate, w_up, w_down))
    h, hb = embed_ln(x.reshape(T, D), emb_ln_g, emb_ln_b)
    for l in range(depth):
        proj = in_proj(hb, w_in, l).reshape(B, S, D_IN_PROJ)
        ys = [attention(proj, attn_sink[l]),
              short_conv(proj, conv_w[l]),
              hgrn2(proj, lb_all[l], hgrn_norm_g[l]),
              retention(proj, ret_decay_logit[l], ret_norm_g[l])]
        h1, h1_packed, route_i, route_w, counts = out_proj_ln_route(
            ys, w_out_b, l, h, ln1_g[l], ln1_b[l], router_w, router_b)
        pos, tile_expert, n_valid, pad_start, pad_len = _routing_plan(
            route_i, counts, T, EXPERT_TILE, route_tile)
        x_sorted = dispatch(h1_packed, pos, pad_start, pad_len, n_valid, n_rows, EXPERT_TILE,
                            tm=route_tile)
        y_sorted = expert_ffn(x_sorted, tile_expert, n_valid, w_gate_b, w_up_b, w_down_b, l,
                              EXPERT_TILE)
        h, hb = combine_ln(h1, y_sorted, pos, route_w[0:2].T, ln2_g[l], ln2_b[l], tm=route_tile)
    return h.reshape(B, S, D)
```

```python
import functools

import jax
import jax.numpy as jnp
from jax import lax
from jax.experimental import pallas as pl
from jax.experimental.pallas import tpu as pltpu

F32 = jnp.float32
BF16 = jnp.bfloat16

D_MODEL = 2048
DEPTH = 2
W_GROUP = 512
HEAD_DIM = 64
N_ATTN_HEADS = 8
N_KV_HEADS = 2
ATTN_GROUP = N_ATTN_HEADS // N_KV_HEADS
WINDOW = 128
ATTN_BLOCK = 128
ATTN_STACK = 4
REC_HEADS = 4
REC_DIM = 128
N_EXPERTS = 16
N_GROUPS = 4
EXPERTS_PER_GROUP = 4
D_EXPERT = 1024
D_IN_PROJ = 6912
DEEPNORM_ALPHA = (2.0 * DEPTH) ** 0.25
LN_EPS = 1e-5
HEAD_NORM_EPS = 1e-6
NEG_BIG = -1e30

LANES = 128
SUBLANES = 8
COL_AQ, COL_AK, COL_AV = 0, 4, 5
COL_CB, COL_CC, COL_CH = 6, 10, 14
COL_GQ, COL_GZF, COL_GZB, COL_GI, COL_GO = 18, 22, 26, 30, 34
COL_RQ, COL_RK, COL_RV, COL_RG = 38, 42, 46, 50

CHUNK = 128
VMEM_LIMIT = 56 * 1024 * 1024

_NT = (((1,), (1,)), ((), ()))
_TN = (((0,), (0,)), ((), ()))


def _params(sem, vmem=VMEM_LIMIT):
    return pltpu.CompilerParams(dimension_semantics=sem, vmem_limit_bytes=vmem)


def _layer_norm(x, g, b):
    mu = jnp.mean(x, axis=-1, keepdims=True)
    xc = x - mu
    var = jnp.mean(xc * xc, axis=-1, keepdims=True)
    return xc * lax.rsqrt(var + LN_EPS) * g + b


def _silu(x):
    return x * (1.0 / (1.0 + jnp.exp(-x)))


def _pack_rows(x):
    n = x.shape[1] // 2
    hi = lax.bitcast_convert_type(x[:, :n].astype(BF16).astype(F32), jnp.uint32)
    lo = lax.bitcast_convert_type(x[:, n:].astype(BF16).astype(F32), jnp.uint32)
    return hi | (lo >> 16)


def _unpack_rows(w):
    hi = lax.bitcast_convert_type(w & jnp.uint32(0xFFFF0000), F32)
    lo = lax.bitcast_convert_type(w << 16, F32)
    return jnp.concatenate([hi, lo], axis=-1)


def _embed_ln_kernel(x_ref, g_ref, b_ref, h_ref, hb_ref):
    h = _layer_norm(x_ref[...], g_ref[...], b_ref[...])
    h_ref[...] = h
    hb_ref[...] = h.astype(BF16)


def embed_ln(x2, g, b, tm=512):
    T, D = x2.shape
    return pl.pallas_call(
        _embed_ln_kernel,
        grid=(T // tm,),
        in_specs=[pl.BlockSpec((tm, D), lambda i: (i, 0)),
                  pl.BlockSpec((1, D), lambda i: (0, 0)),
                  pl.BlockSpec((1, D), lambda i: (0, 0))],
        out_specs=[pl.BlockSpec((tm, D), lambda i: (i, 0)),
                   pl.BlockSpec((tm, D), lambda i: (i, 0))],
        out_shape=[jax.ShapeDtypeStruct((T, D), F32), jax.ShapeDtypeStruct((T, D), BF16)],
        compiler_params=_params(("parallel",)),
        name="embed_ln",
    )(x2, g.reshape(1, D), b.reshape(1, D))


def _in_proj_kernel(x_ref, w_ref, o_ref, wb_ref):
    @pl.when(pl.program_id(1) == 0)
    def _():
        wb_ref[...] = w_ref[...].astype(BF16)

    o_ref[...] = jnp.dot(x_ref[...], wb_ref[...], preferred_element_type=F32)


def in_proj(hb, w_in, layer, tm=1024, tn=768):
    T, K = hb.shape
    N = w_in.shape[2]
    tm = min(tm, T)
    return pl.pallas_call(
        _in_proj_kernel,
        grid=(N // tn, T // tm),
        in_specs=[pl.BlockSpec((tm, K), lambda n, m: (m, 0)),
                  pl.BlockSpec((None, K, tn), lambda n, m: (layer, 0, n))],
        out_specs=pl.BlockSpec((tm, tn), lambda n, m: (m, n)),
        out_shape=jax.ShapeDtypeStruct((T, N), F32),
        scratch_shapes=[pltpu.VMEM((K, tn), BF16)],
        compiler_params=_params(("arbitrary", "arbitrary")),
        name="in_proj",
    )(hb, w_in)


def _attn_kernel(sink_ref, q_ref, kp_ref, kc_ref, kn_ref, vp_ref, vc_ref, vn_ref, bias_ref, o_ref):
    L = ATTN_BLOCK
    n = pl.program_id(1)
    nb = pl.num_programs(1)
    key = lax.broadcasted_iota(jnp.int32, (3 * L, 1), 0)
    valid = ((key >= L) | (n > 0)) & ((key < 2 * L) | (n < nb - 1))
    edge = jnp.where(valid, 0.0, NEG_BIG)
    q = q_ref[0]
    k3 = jnp.concatenate([kp_ref[0], kc_ref[0], kn_ref[0]], axis=0)
    v3 = jnp.concatenate([vp_ref[0], vc_ref[0], vn_ref[0]], axis=0)
    outs = []
    for h in range(N_KV_HEADS):
        kh = k3[:, h * HEAD_DIM:(h + 1) * HEAD_DIM].astype(BF16)
        vh = v3[:, h * HEAD_DIM:(h + 1) * HEAD_DIM].astype(BF16)
        for g0 in range(0, ATTN_GROUP, ATTN_STACK):
            heads = range(h * ATTN_GROUP + g0, h * ATTN_GROUP + g0 + ATTN_STACK)
            qs = jnp.concatenate([q[:, hd * HEAD_DIM:(hd + 1) * HEAD_DIM] for hd in heads], axis=0)
            qs = (qs * (HEAD_DIM ** -0.5)).astype(BF16)
            s = lax.dot_general(kh, qs, _NT, preferred_element_type=F32)
            s = s + bias_ref[h, :, g0 * L:(g0 + ATTN_STACK) * L] + edge
            sk = jnp.concatenate([jnp.full((1, L), sink_ref[hd], F32) for hd in heads], axis=1)
            m = jnp.maximum(jnp.max(s, axis=0, keepdims=True), sk)
            p = jnp.exp(s - m)
            den = jnp.sum(p, axis=0, keepdims=True) + jnp.exp(sk - m)
            o_t = lax.dot_general(vh, p.astype(BF16), _TN, preferred_element_type=F32) / den
            outs.extend(o_t[:, g * L:(g + 1) * L].T for g in range(ATTN_STACK))
    o_ref[0] = jnp.concatenate(outs, axis=-1).astype(o_ref.dtype)


def _attn_bias():
    L = ATTN_BLOCK
    k_rel = jnp.arange(3 * L) - L
    dist = jnp.abs(k_rel[None, :] - jnp.arange(L)[:, None]).astype(F32)
    slopes = 2.0 ** (-8.0 * jnp.arange(1, N_ATTN_HEADS + 1, dtype=F32) / N_ATTN_HEADS)
    bias = -slopes[:, None, None] * dist[None]
    bias = jnp.where(dist[None] <= WINDOW, bias, NEG_BIG)
    bias = bias.reshape(N_KV_HEADS, ATTN_GROUP, L, 3 * L).transpose(0, 3, 1, 2)
    return bias.reshape(N_KV_HEADS, 3 * L, ATTN_GROUP * L)


def attention(proj, sink):
    B, S, _ = proj.shape
    L = ATTN_BLOCK
    nb = S // L
    kv = lambda col, shift: pl.BlockSpec(
        (1, L, LANES), lambda b, n: (b, jnp.clip(n + shift, 0, nb - 1), col))
    return pl.pallas_call(
        _attn_kernel,
        grid=(B, nb),
        in_specs=[pl.BlockSpec(memory_space=pltpu.SMEM),
                  pl.BlockSpec((1, L, W_GROUP), lambda b, n: (b, n, COL_AQ // 4)),
                  kv(COL_AK, -1), kv(COL_AK, 0), kv(COL_AK, 1),
                  kv(COL_AV, -1), kv(COL_AV, 0), kv(COL_AV, 1),
                  pl.BlockSpec((N_KV_HEADS, 3 * L, ATTN_GROUP * L), lambda b, n: (0, 0, 0))],
        out_specs=pl.BlockSpec((1, L, W_GROUP), lambda b, n: (b, n, 0)),
        out_shape=jax.ShapeDtypeStruct((B, S, W_GROUP), BF16),
        compiler_params=_params(("parallel", "arbitrary")),
        name="attention",
    )(sink.astype(F32), proj, proj, proj, proj, proj, proj, proj, _attn_bias())


CONV_ROWS = 512
HALO = 8


def _conv_kernel(b_ref, c_ref, h_ref, w_ref, o_ref, u_ref):
    S = b_ref.shape[1]
    R = min(CONV_ROWS, S)
    u_ref[0:HALO, :] = jnp.zeros((HALO, LANES), F32)
    u_ref[S + HALO:S + 2 * HALO, :] = jnp.zeros((HALO, LANES), F32)

    def gate(i, carry):
        r = pl.multiple_of(i * R, R)
        u_ref[pl.ds(r + HALO, R), :] = c_ref[0, pl.ds(r, R), :] * h_ref[0, pl.ds(r, R), :]
        return carry

    lax.fori_loop(0, S // R, gate, 0)
    w0, w1, w2 = w_ref[0:1, :], w_ref[1:2, :], w_ref[2:3, :]

    def conv(i, carry):
        r = pl.multiple_of(i * R, R)
        a = u_ref[pl.ds(r, R + 2 * HALO), :]
        prev = pltpu.roll(a, 1, 0)[HALO:HALO + R]
        nxt = pltpu.roll(a, R + 2 * HALO - 1, 0)[HALO:HALO + R]
        y = w0 * prev + w1 * a[HALO:HALO + R] + w2 * nxt
        o_ref[0, pl.ds(r, R), :] = (b_ref[0, pl.ds(r, R), :] * y).astype(o_ref.dtype)
        return carry

    lax.fori_loop(0, S // R, conv, 0)


def short_conv(proj, conv_w):
    B, S, _ = proj.shape
    nj = W_GROUP // LANES
    col = lambda c: pl.BlockSpec((1, S, LANES), lambda b, j: (b, 0, c + j))
    return pl.pallas_call(
        _conv_kernel,
        grid=(B, nj),
        in_specs=[col(COL_CB), col(COL_CC), col(COL_CH),
                  pl.BlockSpec((3, LANES), lambda b, j: (0, j))],
        out_specs=pl.BlockSpec((1, S, LANES), lambda b, j: (b, 0, j)),
        out_shape=jax.ShapeDtypeStruct((B, S, W_GROUP), BF16),
        scratch_shapes=[pltpu.VMEM((S + 2 * HALO, LANES), F32)],
        compiler_params=_params(("parallel", "parallel")),
        name="short_conv",
    )(proj, proj, proj, conv_w.astype(F32))


def _level_codes():
    t = lax.broadcasted_iota(jnp.int32, (CHUNK, CHUNK), 0)
    s = lax.broadcasted_iota(jnp.int32, (CHUNK, CHUNK), 1)
    x = t ^ s
    hb = jnp.zeros((CHUNK, CHUNK), jnp.int32)
    c = 1
    while c < CHUNK:
        hb = jnp.where((x & c) != 0, c, hb)
        c *= 2
    diag = jnp.where(t == s, 0, -1)
    return jnp.where(t > s, hb, diag), jnp.where(t < s, hb, diag)


def _hgrn_chunk(q, v, z, lb, code, reverse):
    e = jnp.exp(-jnp.abs(z))
    r = 1.0 / (1.0 + e)
    er = e * r
    nonneg = z >= 0
    f = lb + (1.0 - lb) * jnp.where(nonneg, r, er)
    k = (1.0 - lb) * jnp.where(nonneg, er, r)
    row = lax.broadcasted_iota(jnp.int32, (CHUNK, REC_DIM), 0)
    qs = f
    ks = jnp.ones_like(f)
    blk = f
    a = jnp.where(code == 0,
                  lax.dot_general(q.astype(BF16), k.astype(BF16), _NT, preferred_element_type=F32),
                  0.0)
    c = 1
    while c < CHUNK:
        p = lax.dot_general((q * qs).astype(BF16), (k * ks).astype(BF16), _NT,
                            preferred_element_type=F32)
        a = jnp.where(code == c, p, a)
        if c < SUBLANES:
            upper = (row & c) != 0
            grouped = blk.reshape(CHUNK // SUBLANES, SUBLANES, REC_DIM)
            down = pltpu.roll(grouped, c, 1).reshape(CHUNK, REC_DIM)
            up = (down if 2 * c == SUBLANES
                  else pltpu.roll(grouped, SUBLANES - c, 1).reshape(CHUNK, REC_DIM))
            sib = jnp.where(upper, down, up)
            grow_q = jnp.logical_not(upper) if reverse else upper
            qs = qs * jnp.where(grow_q, sib, 1.0)
            ks = ks * jnp.where(grow_q, 1.0, sib)
            blk = blk * sib
        else:
            step = c // SUBLANES
            group = lambda x, j: x[j * SUBLANES:(j + 1) * SUBLANES]
            nq, nk, nb = [], [], []
            for j in range(CHUNK // SUBLANES):
                upper = (j & step) != 0
                sib = group(blk, j ^ step)
                grow_q = (not upper) if reverse else upper
                nq.append(group(qs, j) * sib if grow_q else group(qs, j))
                nk.append(group(ks, j) if grow_q else group(ks, j) * sib)
                nb.append(group(blk, j) * sib)
            qs, ks, blk = (jnp.concatenate(x, axis=0) for x in (nq, nk, nb))
        c *= 2
    return a, q * qs, k * ks, blk[0:1, :]


def _hgrn_kernel(q_ref, zf_ref, zb_ref, i_ref, g_ref, lb_ref, ng_ref, o_ref, acc_ref, code_ref,
                 stf_ref, stb_ref):
    S = q_ref.shape[1]
    nc = S // CHUNK
    lb = lb_ref[0]
    cf, cb = _level_codes()
    code_ref[0] = cf
    code_ref[1] = cb
    stf_ref[...] = jnp.zeros_like(stf_ref)
    stb_ref[...] = jnp.zeros_like(stb_ref)

    def part(ci, reverse):
        z_ref, st = (zb_ref, stb_ref) if reverse else (zf_ref, stf_ref)
        rows = pl.ds(pl.multiple_of(ci * CHUNK, CHUNK), CHUNK)
        q = q_ref[0, rows, :]
        v = i_ref[0, rows, :]
        vb = v.astype(BF16)
        a, qd, kd, dec = _hgrn_chunk(q, v, z_ref[0, rows, :], lb,
                                     code_ref[1 if reverse else 0], reverse)
        o = jnp.dot(a.astype(BF16), vb, preferred_element_type=F32)
        o = o + lax.dot_general(qd.astype(BF16), st[...].astype(BF16), _NT,
                                preferred_element_type=F32)
        st[...] = st[...] * dec + lax.dot_general(vb, kd.astype(BF16), _TN,
                                                  preferred_element_type=F32)
        return rows, o

    def finish(rows, o):
        o = o + acc_ref[rows, :]
        y = o * lax.rsqrt(jnp.mean(o * o, axis=-1, keepdims=True) + HEAD_NORM_EPS)
        y = y * ng_ref[0] * _silu(g_ref[0, rows, :])
        o_ref[0, rows, :] = y.astype(o_ref.dtype)

    def first_half(j, carry):
        rows, o = part(j, False)
        acc_ref[rows, :] = o
        rows, o = part(nc - 1 - j, True)
        acc_ref[rows, :] = o
        return carry

    def second_half(j, carry):
        finish(*part(j, False))
        finish(*part(nc - 1 - j, True))
        return carry

    lax.fori_loop(0, nc // 2, first_half, 0)
    lax.fori_loop(nc // 2, nc, second_half, 0)


def hgrn2(proj, lb, norm_g):
    B, S, _ = proj.shape
    assert (S // CHUNK) % 2 == 0
    col = lambda c: pl.BlockSpec((1, S, LANES), lambda b, h: (b, 0, c + h))
    vec = pl.BlockSpec((1, 1, LANES), lambda b, h: (h, 0, 0))
    return pl.pallas_call(
        _hgrn_kernel,
        grid=(B, REC_HEADS),
        in_specs=[col(COL_GQ), col(COL_GZF), col(COL_GZB), col(COL_GI), col(COL_GO), vec, vec],
        out_specs=pl.BlockSpec((1, S, LANES), lambda b, h: (b, 0, h)),
        out_shape=jax.ShapeDtypeStruct((B, S, W_GROUP), BF16),
        scratch_shapes=[pltpu.VMEM((S, REC_DIM), F32),
                        pltpu.VMEM((2, CHUNK, CHUNK), jnp.int32),
                        pltpu.VMEM((REC_DIM, REC_DIM), F32),
                        pltpu.VMEM((REC_DIM, REC_DIM), F32)],
        compiler_params=_params(("parallel", "parallel")),
        name="hgrn2",
    )(proj, proj, proj, proj, proj,
      lb.astype(F32).reshape(REC_HEADS, 1, LANES), norm_g.astype(F32).reshape(REC_HEADS, 1, LANES))


RET_HEADS_PER_STEP = 2


def _retention_kernel(dl_ref, q_ref, k_ref, v_ref, g_ref, ng_ref, o_ref, const_ref, *scratch):
    S = q_ref.shape[1]
    nc = S // CHUNK
    nh = RET_HEADS_PER_STEP
    acc_refs, st_refs = scratch[:nh], scratch[nh:]
    scale = REC_DIM ** -0.5
    t = lax.broadcasted_iota(jnp.int32, (CHUNK, CHUNK), 0)
    s = lax.broadcasted_iota(jnp.int32, (CHUNK, CHUNK), 1)
    rel = (t - s).astype(F32)
    pos = lax.broadcasted_iota(jnp.int32, (CHUNK, REC_DIM), 0).astype(F32)
    chunk_decay = []
    for hh in range(nh):
        head = pl.program_id(1) * nh + hh

        def log_gamma(d):
            x = jnp.full((1, LANES), dl_ref[d, head], F32)
            return jnp.minimum(x, 0.0) - jnp.log1p(jnp.exp(-jnp.abs(x)))

        lgf, lgb = log_gamma(0), log_gamma(1)
        const_ref[hh, 0] = (jnp.where(t >= s, jnp.exp(lgf * rel), 0.0)
                            + jnp.where(s >= t, jnp.exp(-lgb * rel), 0.0)) * scale
        const_ref[hh, 1] = jnp.exp(lgf * (pos + 1.0))
        const_ref[hh, 2] = jnp.exp(lgf * (CHUNK - 1.0 - pos)) * scale
        const_ref[hh, 3] = jnp.exp(lgb * (CHUNK - pos))
        const_ref[hh, 4] = jnp.exp(lgb * pos) * scale
        chunk_decay.append((jnp.exp(lgf * CHUNK), jnp.exp(lgb * CHUNK)))
    for st in st_refs:
        st[...] = jnp.zeros_like(st)

    def load(ci, hh):
        rows = pl.ds(pl.multiple_of(ci * CHUNK, CHUNK), CHUNK)
        lanes = slice(hh * LANES, (hh + 1) * LANES)
        return rows, lanes, q_ref[0, rows, lanes], k_ref[0, rows, lanes], v_ref[0, rows, lanes].astype(BF16)

    def fwd_part(ci, hh):
        rows, lanes, q, k, vb = load(ci, hh)
        st = st_refs[2 * hh]
        a = lax.dot_general(q.astype(BF16), k.astype(BF16), _NT, preferred_element_type=F32)
        o = jnp.dot((a * const_ref[hh, 0]).astype(BF16), vb, preferred_element_type=F32)
        o = o + lax.dot_general((q * const_ref[hh, 1]).astype(BF16), st[...].astype(BF16), _NT,
                                preferred_element_type=F32)
        st[...] = st[...] * chunk_decay[hh][0] + lax.dot_general(
            vb, (k * const_ref[hh, 2]).astype(BF16), _TN, preferred_element_type=F32)
        return rows, lanes, o

    def bwd_part(ci, hh):
        rows, lanes, q, k, vb = load(ci, hh)
        st = st_refs[2 * hh + 1]
        o = lax.dot_general((q * const_ref[hh, 3]).astype(BF16), st[...].astype(BF16), _NT,
                            preferred_element_type=F32)
        st[...] = st[...] * chunk_decay[hh][1] + lax.dot_general(
            vb, (k * const_ref[hh, 4]).astype(BF16), _TN, preferred_element_type=F32)
        return rows, lanes, o

    def finish(rows, lanes, hh, o):
        o = o + acc_refs[hh][rows, :]
        oc = o - jnp.mean(o, axis=-1, keepdims=True)
        y = oc * lax.rsqrt(jnp.mean(oc * oc, axis=-1, keepdims=True) + HEAD_NORM_EPS)
        o_ref[0, rows, lanes] = (y * ng_ref[hh] * _silu(g_ref[0, rows, lanes])).astype(o_ref.dtype)

    def first_half(j, carry):
        for hh in range(nh):
            rows, _, o = fwd_part(j, hh)
            acc_refs[hh][rows, :] = o
            rows, _, o = bwd_part(nc - 1 - j, hh)
            acc_refs[hh][rows, :] = o
        return carry

    def second_half(j, carry):
        for hh in range(nh):
            rows, lanes, o = fwd_part(j, hh)
            finish(rows, lanes, hh, o)
            rows, lanes, o = bwd_part(nc - 1 - j, hh)
            finish(rows, lanes, hh, o)
        return carry

    lax.fori_loop(0, nc // 2, first_half, 0)
    lax.fori_loop(nc // 2, nc, second_half, 0)


def retention(proj, decay_logit, norm_g):
    B, S, _ = proj.shape
    nh = RET_HEADS_PER_STEP
    assert (S // CHUNK) % 2 == 0 and REC_HEADS % nh == 0
    col = lambda c: pl.BlockSpec((1, S, nh * LANES), lambda b, h: (b, 0, c // nh + h))
    return pl.pallas_call(
        _retention_kernel,
        grid=(B, REC_HEADS // nh),
        in_specs=[pl.BlockSpec(memory_space=pltpu.SMEM),
                  col(COL_RQ), col(COL_RK), col(COL_RV), col(COL_RG),
                  pl.BlockSpec((nh, 1, LANES), lambda b, h: (h, 0, 0))],
        out_specs=pl.BlockSpec((1, S, nh * LANES), lambda b, h: (b, 0, h)),
        out_shape=jax.ShapeDtypeStruct((B, S, W_GROUP), BF16),
        scratch_shapes=([pltpu.VMEM((nh, 5, CHUNK, REC_DIM), F32)]
                        + [pltpu.VMEM((S, REC_DIM), F32)] * nh
                        + [pltpu.VMEM((REC_DIM, REC_DIM), F32)] * (2 * nh)),
        compiler_params=_params(("parallel", "parallel")),
        name="retention",
    )(decay_logit.astype(F32), proj, proj, proj, proj,
      norm_g.astype(F32).reshape(REC_HEADS, 1, LANES))


ROUTE_ROWS = 8

def _first_max(vals):
    best, idx = vals[0], jnp.zeros(vals[0].shape, jnp.int32)
    for i in range(1, len(vals)):
        take = vals[i] > best
        best = jnp.where(take, vals[i], best)
        idx = jnp.where(take, i, idx)
    return best, idx


def _pick(idx, vals):
    out = vals[-1]
    for i in range(len(vals) - 2, -1, -1):
        out = jnp.where(idx == i, vals[i], out)
    return out


def _out_proj_kernel(ya_ref, yc_ref, yg_ref, yr_ref, w_ref, h_ref, g_ref, b_ref, rwh_ref, rwl_ref,
                     rb_ref, h1_ref, h1p_ref, ri_ref, rwt_ref, cnt_ref, carry_ref, ycat_ref):
    tm = h_ref.shape[0]

    @pl.when(pl.program_id(0) == 0)
    def _():
        carry_ref[...] = jnp.zeros_like(carry_ref)

    ycat_ref[:, 0:W_GROUP] = ya_ref[...]
    ycat_ref[:, W_GROUP:2 * W_GROUP] = yc_ref[...]
    ycat_ref[:, 2 * W_GROUP:3 * W_GROUP] = yg_ref[...]
    ycat_ref[:, 3 * W_GROUP:4 * W_GROUP] = yr_ref[...]
    mix = jnp.dot(ycat_ref[...], w_ref[...], preferred_element_type=F32)
    h1 = _layer_norm(DEEPNORM_ALPHA * h_ref[...] + mix, g_ref[...], b_ref[...])
    h1_ref[...] = h1
    h1p_ref[...] = _pack_rows(h1)

    h1_hi = h1.astype(BF16)
    h1_lo = (h1 - h1_hi.astype(F32)).astype(BF16)
    logits = (lax.dot_general(rwh_ref[...], h1_hi, _NT, preferred_element_type=F32)
              + lax.dot_general(rwh_ref[...], h1_lo, _NT, preferred_element_type=F32)
              + lax.dot_general(rwl_ref[...], h1_hi, _NT, preferred_element_type=F32)
              + rb_ref[...])
    rows = [logits[e:e + 1, :] for e in range(N_EXPERTS)]
    mx = functools.reduce(jnp.maximum, rows)
    ex = [jnp.exp(r - mx) for r in rows]
    den = functools.reduce(jnp.add, ex)
    pr = [x / den for x in ex]
    scores = []
    for g in range(N_GROUPS):
        a, b, c, d = pr[4 * g:4 * g + 4]
        hi1, lo1, hi2, lo2 = jnp.maximum(a, b), jnp.minimum(a, b), jnp.maximum(c, d), jnp.minimum(c, d)
        scores.append(jnp.maximum(hi1, hi2) + jnp.maximum(jnp.minimum(hi1, hi2), jnp.maximum(lo1, lo2)))
    _, gsel = _first_max(scores)
    cand = [_pick(gsel, [pr[4 * g + i] for g in range(N_GROUPS)]) for i in range(EXPERTS_PER_GROUP)]
    p0, i0 = _first_max(cand)
    p1, i1 = _first_max([jnp.where(i0 == i, -1.0, cand[i]) for i in range(EXPERTS_PER_GROUP)])
    e0 = gsel * EXPERTS_PER_GROUP + i0
    e1 = gsel * EXPERTS_PER_GROUP + i1
    tot = p0 + p1
    w0, w1 = p0 / tot, p1 / tot

    ind = jnp.concatenate([((e0 == e) | (e1 == e)).astype(F32) for e in range(N_EXPERTS)], axis=0)
    before = (lax.broadcasted_iota(jnp.int32, (tm, tm), 0)
              < lax.broadcasted_iota(jnp.int32, (tm, tm), 1)).astype(BF16)
    rank = jnp.dot(ind.astype(BF16), before, preferred_element_type=F32) + carry_ref[...]
    carry_ref[...] = carry_ref[...] + jnp.sum(ind, axis=-1, keepdims=True)
    rk = [rank[e:e + 1, :] for e in range(N_EXPERTS)]
    r0 = _pick(e0, rk).astype(jnp.int32)
    r1 = _pick(e1, rk).astype(jnp.int32)
    zi = jnp.zeros((ROUTE_ROWS - 4, tm), jnp.int32)
    ri_ref[...] = jnp.concatenate([e0, e1, r0, r1, zi], axis=0)
    rwt_ref[...] = jnp.concatenate([w0, w1, jnp.zeros((ROUTE_ROWS - 2, tm), F32)], axis=0)
    cnt_ref[...] = jnp.broadcast_to(carry_ref[...], cnt_ref.shape)


def out_proj_ln_route(ys, w_out_bf16, layer, h, g, b, router_w, router_b, tm=512):
    T, D = h.shape
    tm = min(tm, T)
    part = pl.BlockSpec((tm, W_GROUP), lambda i: (i, 0))
    row = pl.BlockSpec((tm, D), lambda i: (i, 0))
    vec = pl.BlockSpec((1, D), lambda i: (0, 0))
    route = pl.BlockSpec((ROUTE_ROWS, tm), lambda i: (0, i))
    once = pl.Buffered(1)
    rw_t = router_w.astype(F32).T
    rw_hi = rw_t.astype(BF16)
    rw_lo = (rw_t - rw_hi.astype(F32)).astype(BF16)
    return pl.pallas_call(
        _out_proj_kernel,
        grid=(T // tm,),
        in_specs=[part, part, part, part,
                  pl.BlockSpec((None, D, D), lambda i: (layer, 0, 0), pipeline_mode=once),
                  row, vec, vec,
                  pl.BlockSpec((N_EXPERTS, D), lambda i: (0, 0), pipeline_mode=once),
                  pl.BlockSpec((N_EXPERTS, D), lambda i: (0, 0), pipeline_mode=once),
                  pl.BlockSpec((N_EXPERTS, 1), lambda i: (0, 0))],
        out_specs=[row, pl.BlockSpec((tm, D // 2), lambda i: (i, 0)), route, route,
                   pl.BlockSpec((N_EXPERTS, LANES), lambda i: (0, 0))],
        out_shape=[jax.ShapeDtypeStruct((T, D), F32),
                   jax.ShapeDtypeStruct((T, D // 2), jnp.uint32),
                   jax.ShapeDtypeStruct((ROUTE_ROWS, T), jnp.int32),
                   jax.ShapeDtypeStruct((ROUTE_ROWS, T), F32),
                   jax.ShapeDtypeStruct((N_EXPERTS, LANES), F32)],
        scratch_shapes=[pltpu.VMEM((N_EXPERTS, 1), F32), pltpu.VMEM((tm, D), BF16)],
        compiler_params=_params(("arbitrary",)),
        name="out_proj_ln_route",
    )(*[y.reshape(T, W_GROUP) for y in ys], w_out_bf16, h, g.reshape(1, D), b.reshape(1, D),
      rw_hi, rw_lo, router_b.astype(F32).reshape(N_EXPERTS, 1))


ROW_DMA_UNROLL = 4


DISPATCH_BUFFERS = 3


def _dispatch_kernel(pad_start_ref, pad_len_ref, nv_ref, pos_ref, h_hbm, x_hbm, hbuf_ref, zero_ref,
                     load_sems, row_sems, zsem):
    tm = hbuf_ref.shape[1]
    tile = zero_ref.shape[0]
    n_tiles = x_hbm.shape[0] // tile
    step = pl.program_id(0)
    n_steps = pl.num_programs(0)

    def load(t):
        return pltpu.make_async_copy(h_hbm.at[pl.ds(pl.multiple_of(t * tm, tm), tm), :],
                                     hbuf_ref.at[t % DISPATCH_BUFFERS],
                                     load_sems.at[t % DISPATCH_BUFFERS])

    def rows_done(t):
        for slot in range(2):
            pltpu.make_async_copy(hbuf_ref.at[t % DISPATCH_BUFFERS], x_hbm.at[pl.ds(0, tm), :],
                                  row_sems.at[t % DISPATCH_BUFFERS, slot]).wait()

    @pl.when(step == 0)
    def _():
        load(0).start()

        @pl.when(n_steps > 1)
        def _():
            load(1).start()

    load(step).wait()
    cur = step % DISPATCH_BUFFERS

    def issue(i, carry):
        for u in range(ROW_DMA_UNROLL):
            r = i * ROW_DMA_UNROLL + u
            for slot in range(2):
                pltpu.make_async_copy(hbuf_ref.at[cur, pl.ds(r, 1), :],
                                      x_hbm.at[pl.ds(pos_ref[0, 0, slot * tm + r], 1), :],
                                      row_sems.at[cur, slot]).start(priority=slot)
        return carry

    lax.fori_loop(0, tm // ROW_DMA_UNROLL, issue, 0)

    @pl.when(step >= 1)
    def _():
        rows_done(step - 1)

    @pl.when(step + 2 < n_steps)
    def _():
        load(step + 2).start()

    @pl.when(step == n_steps - 1)
    def _():
        rows_done(step)

    @pl.when(step == 0)
    def _():
        zero_ref[...] = jnp.zeros_like(zero_ref)

        def tcopy(i):
            return pltpu.make_async_copy(
                zero_ref, x_hbm.at[pl.ds(pl.multiple_of(i * tile, tile), tile), :], zsem)

        def tissue(i, carry):
            tcopy(i).start()
            return carry

        def twait(i, carry):
            tcopy(i).wait()
            return carry

        lax.fori_loop(nv_ref[0], n_tiles, tissue, 0)
        lax.fori_loop(nv_ref[0], n_tiles, twait, 0)
        for e in range(N_EXPERTS):
            def zcopy(i):
                return pltpu.make_async_copy(
                    zero_ref.at[pl.ds(0, 1), :], x_hbm.at[pl.ds(pad_start_ref[e] + i, 1), :], zsem)

            def zissue(i, carry):
                zcopy(i).start()
                return carry

            def zwait(i, carry):
                zcopy(i).wait()
                return carry

            lax.fori_loop(0, pad_len_ref[e], zissue, 0)
            lax.fori_loop(0, pad_len_ref[e], zwait, 0)


def dispatch(h1, pos, pad_start, pad_len, n_valid, n_rows, tile, tm=256):
    T, D = h1.shape
    tm = min(tm, T)
    return pl.pallas_call(
        _dispatch_kernel,
        grid_spec=pltpu.PrefetchScalarGridSpec(
            num_scalar_prefetch=3,
            grid=(T // tm,),
            in_specs=[pl.BlockSpec((1, 1, 2 * tm), lambda i, *_: (i, 0, 0), memory_space=pltpu.SMEM),
                      pl.BlockSpec(memory_space=pl.ANY)],
            out_specs=pl.BlockSpec(memory_space=pl.ANY),
            scratch_shapes=[pltpu.VMEM((DISPATCH_BUFFERS, tm, D), h1.dtype),
                            pltpu.VMEM((tile, D), h1.dtype),
                            pltpu.SemaphoreType.DMA((DISPATCH_BUFFERS,)),
                            pltpu.SemaphoreType.DMA((DISPATCH_BUFFERS, 2)),
                            pltpu.SemaphoreType.DMA]),
        out_shape=jax.ShapeDtypeStruct((n_rows, D), h1.dtype),
        compiler_params=_params(("arbitrary",)),
        name="dispatch",
    )(pad_start, pad_len, n_valid, pos, h1)


def _expert_kernel(layer, te_ref, nv_ref, nxt_ref, x_ref, wg_hbm, wu_hbm, wd_hbm, y_ref,
                   sg_ref, su_ref, sd_ref, wg_ref, wu_ref, wd_ref, sems):
    i = pl.program_id(0)
    valid = i < nv_ref[0]

    def stage(e):
        return [pltpu.make_async_copy(src.at[layer, e], dst, sems.at[k])
                for k, (src, dst) in enumerate(((wg_hbm, sg_ref), (wu_hbm, su_ref), (wd_hbm, sd_ref)))]

    @pl.when(i == 0)
    def _():
        for c in stage(te_ref[0]):
            c.start()

    first_of_expert = valid & ((i == 0) | (te_ref[jnp.maximum(i - 1, 0)] != te_ref[i]))

    @pl.when(first_of_expert)
    def _():
        for c in stage(te_ref[i]):
            c.wait()
        wg_ref[...] = sg_ref[...].astype(BF16)
        wu_ref[...] = su_ref[...].astype(BF16)
        wd_ref[...] = sd_ref[...].astype(BF16)

        @pl.when(nxt_ref[i] >= 0)
        def _():
            for c in stage(nxt_ref[i]):
                c.start()

    @pl.when(valid)
    def _():
        x = _unpack_rows(x_ref[...]).astype(BF16)
        gate = jnp.dot(x, wg_ref[...], preferred_element_type=F32)
        up = jnp.dot(x, wu_ref[...], preferred_element_type=F32)
        hid = (_silu(gate) * up).astype(BF16)
        y_ref[...] = _pack_rows(jnp.dot(hid, wd_ref[...], preferred_element_type=F32))

    @pl.when(jnp.logical_not(valid))
    def _():
        y_ref[...] = jnp.zeros_like(y_ref)


def expert_ffn(x_sorted, tile_expert, n_valid, next_expert, w_gate, w_up, w_down, layer, tm):
    A, DP = x_sorted.shape
    D = 2 * DP
    n_tiles = A // tm
    tile = lambda i, te, nv, nx: (jnp.minimum(i, nv[0] - 1), 0)
    out_tile = lambda i, te, nv, nx: (i, 0)
    anywhere = pl.BlockSpec(memory_space=pl.ANY)
    return pl.pallas_call(
        functools.partial(_expert_kernel, layer),
        grid_spec=pltpu.PrefetchScalarGridSpec(
            num_scalar_prefetch=3,
            grid=(n_tiles,),
            in_specs=[pl.BlockSpec((tm, DP), tile), anywhere, anywhere, anywhere],
            out_specs=pl.BlockSpec((tm, DP), out_tile),
            scratch_shapes=[pltpu.VMEM((D, D_EXPERT), F32), pltpu.VMEM((D, D_EXPERT), F32),
                            pltpu.VMEM((D_EXPERT, D), F32),
                            pltpu.VMEM((D, D_EXPERT), BF16), pltpu.VMEM((D, D_EXPERT), BF16),
                            pltpu.VMEM((D_EXPERT, D), BF16),
                            pltpu.SemaphoreType.DMA((3,))]),
        out_shape=jax.ShapeDtypeStruct((A, DP), jnp.uint32),
        compiler_params=_params(("arbitrary",)),
        name="expert_ffn",
    )(tile_expert, n_valid, next_expert, x_sorted, w_gate, w_up, w_down)


def _combine_kernel(pos_ref, pos_next_ref, h_ref, w_ref, g_ref, b_ref, y_hbm, o_ref, ob_ref,
                    buf_ref, sems):
    tm = h_ref.shape[0]
    step = pl.program_id(0)
    cur = step % 2

    def gather(p_ref, half):
        def issue(i, carry):
            for u in range(ROW_DMA_UNROLL):
                r = i * ROW_DMA_UNROLL + u
                for slot in range(2):
                    pltpu.make_async_copy(y_hbm.at[pl.ds(p_ref[0, 0, slot * tm + r], 1), :],
                                          buf_ref.at[half, slot, pl.ds(r, 1), :],
                                          sems.at[half, slot]).start(priority=slot)
            return carry

        lax.fori_loop(0, tm // ROW_DMA_UNROLL, issue, 0)

    @pl.when(step == 0)
    def _():
        gather(pos_ref, 0)

    @pl.when(step + 1 < pl.num_programs(0))
    def _():
        gather(pos_next_ref, 1 - cur)

    for slot in range(2):
        pltpu.make_async_copy(y_hbm.at[pl.ds(0, tm), :], buf_ref.at[cur, slot],
                              sems.at[cur, slot]).wait()
    w = w_ref[...]
    ffn = (w[:, 0:1] * _unpack_rows(buf_ref[cur, 0]) + w[:, 1:2] * _unpack_rows(buf_ref[cur, 1]))
    h2 = _layer_norm(DEEPNORM_ALPHA * h_ref[...] + ffn, g_ref[...], b_ref[...])
    o_ref[...] = h2
    ob_ref[...] = h2.astype(BF16)


def combine_ln(h1, y_sorted, pos, wcol, g, b, tm=256):
    T, D = h1.shape
    tm = min(tm, T)
    row = pl.BlockSpec((tm, D), lambda i: (i, 0))
    vec = pl.BlockSpec((1, D), lambda i: (0, 0))
    last = T // tm - 1
    return pl.pallas_call(
        _combine_kernel,
        grid=(T // tm,),
        in_specs=[pl.BlockSpec((1, 1, 2 * tm), lambda i: (i, 0, 0), memory_space=pltpu.SMEM),
                  pl.BlockSpec((1, 1, 2 * tm), lambda i: (jnp.minimum(i + 1, last), 0, 0),
                               memory_space=pltpu.SMEM),
                  row,
                  pl.BlockSpec((tm, 2), lambda i: (i, 0)),
                  vec, vec,
                  pl.BlockSpec(memory_space=pl.ANY)],
        out_specs=[row, row],
        out_shape=[jax.ShapeDtypeStruct((T, D), F32), jax.ShapeDtypeStruct((T, D), BF16)],
        scratch_shapes=[pltpu.VMEM((2, 2, tm, D // 2), jnp.uint32),
                        pltpu.SemaphoreType.DMA((2, 2))],
        compiler_params=_params(("arbitrary",)),
        name="combine_ln",
    )(pos, pos, h1, wcol, g.reshape(1, D), b.reshape(1, D), y_sorted)


EXPERT_TILE = 256
ROUTE_TILE = 256


def _routing_plan(route_i, counts, T, tile, route_tile):
    cnt = counts[:, 0].astype(jnp.int32)
    padded = ((cnt + tile - 1) // tile) * tile
    ends = jnp.cumsum(padded)
    offs = ends - padded
    e0, e1, r0, r1 = route_i[0], route_i[1], route_i[2], route_i[3]
    pos0 = offs[e0] + r0
    pos1 = offs[e1] + r1
    nrt = T // route_tile
    pos = jnp.concatenate([pos0.reshape(nrt, 1, route_tile), pos1.reshape(nrt, 1, route_tile)], axis=-1)
    n_tiles = (2 * T) // tile + N_EXPERTS
    tile_ids = jnp.arange(n_tiles, dtype=jnp.int32)
    tile_expert = jnp.minimum(
        jnp.sum((ends[None, :] // tile <= tile_ids[:, None]).astype(jnp.int32), axis=1),
        N_EXPERTS - 1).astype(jnp.int32)
    n_valid = (ends[-1] // tile).astype(jnp.int32).reshape(1)
    ids = jnp.arange(N_EXPERTS, dtype=jnp.int32)
    later = (ids[None, :] > ids[:, None]) & (cnt[None, :] > 0)
    following = jnp.min(jnp.where(later, ids[None, :], N_EXPERTS), axis=1)
    following = jnp.where(following == N_EXPERTS, -1, following).astype(jnp.int32)
    return (pos, tile_expert, n_valid, following[tile_expert],
            (offs + cnt).astype(jnp.int32), (padded - cnt).astype(jnp.int32))


def _hgrn_lower_bounds(hgrn_lb):
    lb = jnp.cumsum(jax.nn.softmax(hgrn_lb.astype(F32), axis=0), axis=0)
    return lb - lb[0:1]


def kernel(x, emb_ln_g, emb_ln_b, w_in, attn_sink, conv_w, hgrn_lb, hgrn_norm_g, ret_decay_logit,
           ret_norm_g, w_out, ln1_g, ln1_b, router_w, router_b, w_gate, w_up, w_down, ln2_g, ln2_b):
    B, S, D = x.shape
    T = B * S
    depth = w_in.shape[0]
    lb_all = _hgrn_lower_bounds(hgrn_lb)
    route_tile = min(ROUTE_TILE, T)
    n_rows = 2 * T + N_EXPERTS * EXPERT_TILE

    w_out_b = w_out.astype(BF16)
    h, hb = embed_ln(x.reshape(T, D), emb_ln_g, emb_ln_b)
    for l in range(depth):
        proj = in_proj(hb, w_in, l).reshape(B, S, D_IN_PROJ)
        ys = [attention(proj, attn_sink[l]),
              short_conv(proj, conv_w[l]),
              hgrn2(proj, lb_all[l], hgrn_norm_g[l]),
              retention(proj, ret_decay_logit[l], ret_norm_g[l])]
        h1, h1_packed, route_i, route_w, counts = out_proj_ln_route(
            ys, w_out_b, l, h, ln1_g[l], ln1_b[l], router_w, router_b)
        pos, tile_expert, n_valid, next_expert, pad_start, pad_len = _routing_plan(
            route_i, counts, T, EXPERT_TILE, route_tile)
        x_sorted = dispatch(h1_packed, pos, pad_start, pad_len, n_valid, n_rows, EXPERT_TILE,
                            tm=route_tile)
        y_sorted = expert_ffn(x_sorted, tile_expert, n_valid, next_expert, w_gate, w_up, w_down, l,
                              EXPERT_TILE)
        h, hb = combine_ln(h1, y_sorted, pos, route_w[0:2].T, ln2_g[l], ln2_b[l], tm=route_tile)
    return h.reshape(B, S, D)
```

```python
import functools

import jax
import jax.numpy as jnp
from jax import lax
from jax.experimental import pallas as pl
from jax.experimental.pallas import tpu as pltpu

F32 = jnp.float32
BF16 = jnp.bfloat16

D_MODEL = 2048
DEPTH = 2
W_GROUP = 512
HEAD_DIM = 64
N_ATTN_HEADS = 8
N_KV_HEADS = 2
ATTN_GROUP = N_ATTN_HEADS // N_KV_HEADS
WINDOW = 128
ATTN_BLOCK = 128
ATTN_STACK = 4
REC_HEADS = 4
REC_DIM = 128
N_EXPERTS = 16
N_GROUPS = 4
EXPERTS_PER_GROUP = 4
D_EXPERT = 1024
D_IN_PROJ = 6912
DEEPNORM_ALPHA = (2.0 * DEPTH) ** 0.25
LN_EPS = 1e-5
HEAD_NORM_EPS = 1e-6
NEG_BIG = -1e30

LANES = 128
SUBLANES = 8
COL_AQ, COL_AK, COL_AV = 0, 4, 5
COL_CB, COL_CC, COL_CH = 6, 10, 14
COL_GQ, COL_GZF, COL_GZB, COL_GI, COL_GO = 18, 22, 26, 30, 34
COL_RQ, COL_RK, COL_RV, COL_RG = 38, 42, 46, 50

CHUNK = 128
VMEM_LIMIT = 56 * 1024 * 1024

_NT = (((1,), (1,)), ((), ()))
_TN = (((0,), (0,)), ((), ()))


def _params(sem, vmem=VMEM_LIMIT):
    return pltpu.CompilerParams(dimension_semantics=sem, vmem_limit_bytes=vmem)


def _layer_norm(x, g, b):
    mu = jnp.mean(x, axis=-1, keepdims=True)
    xc = x - mu
    var = jnp.mean(xc * xc, axis=-1, keepdims=True)
    return xc * lax.rsqrt(var + LN_EPS) * g + b


def _silu(x):
    return x * (1.0 / (1.0 + jnp.exp(-x)))


def _pack_rows(x):
    n = x.shape[1] // 2
    hi = lax.bitcast_convert_type(x[:, :n].astype(BF16).astype(F32), jnp.uint32)
    lo = lax.bitcast_convert_type(x[:, n:].astype(BF16).astype(F32), jnp.uint32)
    return hi | (lo >> 16)


def _unpack_rows(w):
    hi = lax.bitcast_convert_type(w & jnp.uint32(0xFFFF0000), F32)
    lo = lax.bitcast_convert_type(w << 16, F32)
    return jnp.concatenate([hi, lo], axis=-1)


def _embed_ln_kernel(x_ref, g_ref, b_ref, h_ref, hb_ref):
    h = _layer_norm(x_ref[...], g_ref[...], b_ref[...])
    h_ref[...] = h
    hb_ref[...] = h.astype(BF16)


def embed_ln(x2, g, b, tm=512):
    T, D = x2.shape
    return pl.pallas_call(
        _embed_ln_kernel,
        grid=(T // tm,),
        in_specs=[pl.BlockSpec((tm, D), lambda i: (i, 0)),
                  pl.BlockSpec((1, D), lambda i: (0, 0)),
                  pl.BlockSpec((1, D), lambda i: (0, 0))],
        out_specs=[pl.BlockSpec((tm, D), lambda i: (i, 0)),
                   pl.BlockSpec((tm, D), lambda i: (i, 0))],
        out_shape=[jax.ShapeDtypeStruct((T, D), F32), jax.ShapeDtypeStruct((T, D), BF16)],
        compiler_params=_params(("parallel",)),
        name="embed_ln",
    )(x2, g.reshape(1, D), b.reshape(1, D))


def _in_proj_kernel(x_ref, w_ref, o_ref, wb_ref):
    @pl.when(pl.program_id(1) == 0)
    def _():
        wb_ref[...] = w_ref[...].astype(BF16)

    o_ref[...] = jnp.dot(x_ref[...], wb_ref[...], preferred_element_type=F32)


def in_proj(hb, w_in, layer, tm=2048, tn=768):
    T, K = hb.shape
    N = w_in.shape[2]
    tm = min(tm, T)
    return pl.pallas_call(
        _in_proj_kernel,
        grid=(N // tn, T // tm),
        in_specs=[pl.BlockSpec((tm, K), lambda n, m: (m, 0)),
                  pl.BlockSpec((None, K, tn), lambda n, m: (layer, 0, n))],
        out_specs=pl.BlockSpec((tm, tn), lambda n, m: (m, n)),
        out_shape=jax.ShapeDtypeStruct((T, N), F32),
        scratch_shapes=[pltpu.VMEM((K, tn), BF16)],
        compiler_params=_params(("arbitrary", "arbitrary")),
        name="in_proj",
    )(hb, w_in)


def _attn_kernel(sink_ref, q_ref, kp_ref, kc_ref, kn_ref, vp_ref, vc_ref, vn_ref, bias_ref, o_ref):
    L = ATTN_BLOCK
    n = pl.program_id(1)
    nb = pl.num_programs(1)
    key = lax.broadcasted_iota(jnp.int32, (3 * L, 1), 0)
    valid = ((key >= L) | (n > 0)) & ((key < 2 * L) | (n < nb - 1))
    edge = jnp.where(valid, 0.0, NEG_BIG)
    q = q_ref[0]
    k3 = jnp.concatenate([kp_ref[0], kc_ref[0], kn_ref[0]], axis=0)
    v3 = jnp.concatenate([vp_ref[0], vc_ref[0], vn_ref[0]], axis=0)
    outs = []
    for h in range(N_KV_HEADS):
        kh = k3[:, h * HEAD_DIM:(h + 1) * HEAD_DIM].astype(BF16)
        vh = v3[:, h * HEAD_DIM:(h + 1) * HEAD_DIM].astype(BF16)
        for g0 in range(0, ATTN_GROUP, ATTN_STACK):
            heads = range(h * ATTN_GROUP + g0, h * ATTN_GROUP + g0 + ATTN_STACK)
            qs = jnp.concatenate([q[:, hd * HEAD_DIM:(hd + 1) * HEAD_DIM] for hd in heads], axis=0)
            qs = (qs * (HEAD_DIM ** -0.5)).astype(BF16)
            s = lax.dot_general(kh, qs, _NT, preferred_element_type=F32)
            s = s + bias_ref[h, :, g0 * L:(g0 + ATTN_STACK) * L] + edge
            sk = jnp.concatenate([jnp.full((1, L), sink_ref[hd], F32) for hd in heads], axis=1)
            m = jnp.maximum(jnp.max(s, axis=0, keepdims=True), sk)
            p = jnp.exp(s - m)
            den = jnp.sum(p, axis=0, keepdims=True) + jnp.exp(sk - m)
            o_t = lax.dot_general(vh, p.astype(BF16), _TN, preferred_element_type=F32) / den
            outs.extend(o_t[:, g * L:(g + 1) * L].T for g in range(ATTN_STACK))
    o_ref[0] = jnp.concatenate(outs, axis=-1).astype(o_ref.dtype)


def _attn_bias():
    L = ATTN_BLOCK
    k_rel = jnp.arange(3 * L) - L
    dist = jnp.abs(k_rel[None, :] - jnp.arange(L)[:, None]).astype(F32)
    slopes = 2.0 ** (-8.0 * jnp.arange(1, N_ATTN_HEADS + 1, dtype=F32) / N_ATTN_HEADS)
    bias = -slopes[:, None, None] * dist[None]
    bias = jnp.where(dist[None] <= WINDOW, bias, NEG_BIG)
    bias = bias.reshape(N_KV_HEADS, ATTN_GROUP, L, 3 * L).transpose(0, 3, 1, 2)
    return bias.reshape(N_KV_HEADS, 3 * L, ATTN_GROUP * L)


def attention(proj, sink):
    B, S, _ = proj.shape
    L = ATTN_BLOCK
    nb = S // L
    kv = lambda col, shift: pl.BlockSpec(
        (1, L, LANES), lambda b, n: (b, jnp.clip(n + shift, 0, nb - 1), col))
    return pl.pallas_call(
        _attn_kernel,
        grid=(B, nb),
        in_specs=[pl.BlockSpec(memory_space=pltpu.SMEM),
                  pl.BlockSpec((1, L, W_GROUP), lambda b, n: (b, n, COL_AQ // 4)),
                  kv(COL_AK, -1), kv(COL_AK, 0), kv(COL_AK, 1),
                  kv(COL_AV, -1), kv(COL_AV, 0), kv(COL_AV, 1),
                  pl.BlockSpec((N_KV_HEADS, 3 * L, ATTN_GROUP * L), lambda b, n: (0, 0, 0))],
        out_specs=pl.BlockSpec((1, L, W_GROUP), lambda b, n: (b, n, 0)),
        out_shape=jax.ShapeDtypeStruct((B, S, W_GROUP), BF16),
        compiler_params=_params(("parallel", "arbitrary")),
        name="attention",
    )(sink.astype(F32), proj, proj, proj, proj, proj, proj, proj, _attn_bias())


CONV_ROWS = 512
HALO = 8


def _conv_kernel(b_ref, c_ref, h_ref, w_ref, o_ref, u_ref):
    S = b_ref.shape[1]
    R = min(CONV_ROWS, S)
    u_ref[0:HALO, :] = jnp.zeros((HALO, LANES), F32)
    u_ref[S + HALO:S + 2 * HALO, :] = jnp.zeros((HALO, LANES), F32)

    def gate(i, carry):
        r = pl.multiple_of(i * R, R)
        u_ref[pl.ds(r + HALO, R), :] = c_ref[0, pl.ds(r, R), :] * h_ref[0, pl.ds(r, R), :]
        return carry

    lax.fori_loop(0, S // R, gate, 0)
    w0, w1, w2 = w_ref[0:1, :], w_ref[1:2, :], w_ref[2:3, :]

    def conv(i, carry):
        r = pl.multiple_of(i * R, R)
        a = u_ref[pl.ds(r, R + 2 * HALO), :]
        prev = pltpu.roll(a, 1, 0)[HALO:HALO + R]
        nxt = pltpu.roll(a, R + 2 * HALO - 1, 0)[HALO:HALO + R]
        y = w0 * prev + w1 * a[HALO:HALO + R] + w2 * nxt
        o_ref[0, pl.ds(r, R), :] = (b_ref[0, pl.ds(r, R), :] * y).astype(o_ref.dtype)
        return carry

    lax.fori_loop(0, S // R, conv, 0)


def short_conv(proj, conv_w):
    B, S, _ = proj.shape
    nj = W_GROUP // LANES
    col = lambda c: pl.BlockSpec((1, S, LANES), lambda b, j: (b, 0, c + j))
    return pl.pallas_call(
        _conv_kernel,
        grid=(B, nj),
        in_specs=[col(COL_CB), col(COL_CC), col(COL_CH),
                  pl.BlockSpec((3, LANES), lambda b, j: (0, j))],
        out_specs=pl.BlockSpec((1, S, LANES), lambda b, j: (b, 0, j)),
        out_shape=jax.ShapeDtypeStruct((B, S, W_GROUP), BF16),
        scratch_shapes=[pltpu.VMEM((S + 2 * HALO, LANES), F32)],
        compiler_params=_params(("parallel", "parallel")),
        name="short_conv",
    )(proj, proj, proj, conv_w.astype(F32))


def _level_codes():
    t = lax.broadcasted_iota(jnp.int32, (CHUNK, CHUNK), 0)
    s = lax.broadcasted_iota(jnp.int32, (CHUNK, CHUNK), 1)
    x = t ^ s
    hb = jnp.zeros((CHUNK, CHUNK), jnp.int32)
    c = 1
    while c < CHUNK:
        hb = jnp.where((x & c) != 0, c, hb)
        c *= 2
    diag = jnp.where(t == s, 0, -1)
    return jnp.where(t > s, hb, diag), jnp.where(t < s, hb, diag)


def _hgrn_chunk(q, v, z, lb, code, reverse):
    e = jnp.exp(-jnp.abs(z))
    r = 1.0 / (1.0 + e)
    er = e * r
    nonneg = z >= 0
    f = lb + (1.0 - lb) * jnp.where(nonneg, r, er)
    k = (1.0 - lb) * jnp.where(nonneg, er, r)
    row = lax.broadcasted_iota(jnp.int32, (CHUNK, REC_DIM), 0)
    qs = f
    ks = jnp.ones_like(f)
    blk = f
    a = jnp.where(code == 0,
                  lax.dot_general(q.astype(BF16), k.astype(BF16), _NT, preferred_element_type=F32),
                  0.0)
    c = 1
    while c < CHUNK:
        p = lax.dot_general((q * qs).astype(BF16), (k * ks).astype(BF16), _NT,
                            preferred_element_type=F32)
        a = jnp.where(code == c, p, a)
        if c < SUBLANES:
            upper = (row & c) != 0
            grouped = blk.reshape(CHUNK // SUBLANES, SUBLANES, REC_DIM)
            down = pltpu.roll(grouped, c, 1).reshape(CHUNK, REC_DIM)
            up = (down if 2 * c == SUBLANES
                  else pltpu.roll(grouped, SUBLANES - c, 1).reshape(CHUNK, REC_DIM))
            sib = jnp.where(upper, down, up)
            grow_q = jnp.logical_not(upper) if reverse else upper
            qs = qs * jnp.where(grow_q, sib, 1.0)
            ks = ks * jnp.where(grow_q, 1.0, sib)
            blk = blk * sib
        else:
            step = c // SUBLANES
            group = lambda x, j: x[j * SUBLANES:(j + 1) * SUBLANES]
            nq, nk, nb = [], [], []
            for j in range(CHUNK // SUBLANES):
                upper = (j & step) != 0
                sib = group(blk, j ^ step)
                grow_q = (not upper) if reverse else upper
                nq.append(group(qs, j) * sib if grow_q else group(qs, j))
                nk.append(group(ks, j) if grow_q else group(ks, j) * sib)
                nb.append(group(blk, j) * sib)
            qs, ks, blk = (jnp.concatenate(x, axis=0) for x in (nq, nk, nb))
        c *= 2
    return a, q * qs, k * ks, blk[0:1, :]


HGRN_HEADS_PER_STEP = 2


def _hgrn_kernel(q_ref, zf_ref, zb_ref, i_ref, g_ref, lb_ref, ng_ref, o_ref, code_ref, *scratch):
    S = q_ref.shape[1]
    nc = S // CHUNK
    nh = HGRN_HEADS_PER_STEP
    acc_refs, st_refs = scratch[:nh], scratch[nh:]
    cf, cb = _level_codes()
    code_ref[0] = cf
    code_ref[1] = cb
    for st in st_refs:
        st[...] = jnp.zeros_like(st)

    def part(ci, hh, reverse):
        z_ref = zb_ref if reverse else zf_ref
        st = st_refs[2 * hh + (1 if reverse else 0)]
        rows = pl.ds(pl.multiple_of(ci * CHUNK, CHUNK), CHUNK)
        lanes = slice(hh * LANES, (hh + 1) * LANES)
        q = q_ref[0, rows, lanes]
        v = i_ref[0, rows, lanes]
        vb = v.astype(BF16)
        a, qd, kd, dec = _hgrn_chunk(q, v, z_ref[0, rows, lanes], lb_ref[hh],
                                     code_ref[1 if reverse else 0], reverse)
        o = jnp.dot(a.astype(BF16), vb, preferred_element_type=F32)
        o = o + lax.dot_general(qd.astype(BF16), st[...].astype(BF16), _NT,
                                preferred_element_type=F32)
        st[...] = st[...] * dec + lax.dot_general(vb, kd.astype(BF16), _TN,
                                                  preferred_element_type=F32)
        return rows, lanes, o

    def finish(hh, rows, lanes, o):
        o = o + acc_refs[hh][rows, :]
        y = o * lax.rsqrt(jnp.mean(o * o, axis=-1, keepdims=True) + HEAD_NORM_EPS)
        y = y * ng_ref[hh] * _silu(g_ref[0, rows, lanes])
        o_ref[0, rows, lanes] = y.astype(o_ref.dtype)

    def first_half(j, carry):
        for hh in range(nh):
            rows, _, o = part(j, hh, False)
            acc_refs[hh][rows, :] = o
            rows, _, o = part(nc - 1 - j, hh, True)
            acc_refs[hh][rows, :] = o
        return carry

    def second_half(j, carry):
        for hh in range(nh):
            finish(hh, *part(j, hh, False))
            finish(hh, *part(nc - 1 - j, hh, True))
        return carry

    lax.fori_loop(0, nc // 2, first_half, 0)
    lax.fori_loop(nc // 2, nc, second_half, 0)


def hgrn2(proj, lb, norm_g):
    B, S, _ = proj.shape
    nh = HGRN_HEADS_PER_STEP
    assert (S // CHUNK) % 2 == 0 and REC_HEADS % nh == 0
    col = lambda c: pl.BlockSpec((1, S, nh * LANES), lambda b, h: (b, 0, c // nh + h))
    vec = pl.BlockSpec((nh, 1, LANES), lambda b, h: (h, 0, 0))
    return pl.pallas_call(
        _hgrn_kernel,
        grid=(B, REC_HEADS // nh),
        in_specs=[col(COL_GQ), col(COL_GZF), col(COL_GZB), col(COL_GI), col(COL_GO), vec, vec],
        out_specs=pl.BlockSpec((1, S, nh * LANES), lambda b, h: (b, 0, h)),
        out_shape=jax.ShapeDtypeStruct((B, S, W_GROUP), BF16),
        scratch_shapes=([pltpu.VMEM((2, CHUNK, CHUNK), jnp.int32)]
                        + [pltpu.VMEM((S, REC_DIM), F32)] * nh
                        + [pltpu.VMEM((REC_DIM, REC_DIM), F32)] * (2 * nh)),
        compiler_params=_params(("parallel", "parallel")),
        name="hgrn2",
    )(proj, proj, proj, proj, proj,
      lb.astype(F32).reshape(REC_HEADS, 1, LANES), norm_g.astype(F32).reshape(REC_HEADS, 1, LANES))


RET_HEADS_PER_STEP = 2


def _retention_kernel(dl_ref, q_ref, k_ref, v_ref, g_ref, ng_ref, o_ref, const_ref, *scratch):
    S = q_ref.shape[1]
    nc = S // CHUNK
    nh = RET_HEADS_PER_STEP
    acc_refs, st_refs = scratch[:nh], scratch[nh:]
    scale = REC_DIM ** -0.5
    t = lax.broadcasted_iota(jnp.int32, (CHUNK, CHUNK), 0)
    s = lax.broadcasted_iota(jnp.int32, (CHUNK, CHUNK), 1)
    rel = (t - s).astype(F32)
    pos = lax.broadcasted_iota(jnp.int32, (CHUNK, REC_DIM), 0).astype(F32)
    chunk_decay = []
    for hh in range(nh):
        head = pl.program_id(1) * nh + hh

        def log_gamma(d):
            x = jnp.full((1, LANES), dl_ref[d, head], F32)
            return jnp.minimum(x, 0.0) - jnp.log1p(jnp.exp(-jnp.abs(x)))

        lgf, lgb = log_gamma(0), log_gamma(1)
        const_ref[hh, 0] = (jnp.where(t >= s, jnp.exp(lgf * rel), 0.0)
                            + jnp.where(s >= t, jnp.exp(-lgb * rel), 0.0)) * scale
        const_ref[hh, 1] = jnp.exp(lgf * (pos + 1.0))
        const_ref[hh, 2] = jnp.exp(lgf * (CHUNK - 1.0 - pos)) * scale
        const_ref[hh, 3] = jnp.exp(lgb * (CHUNK - pos))
        const_ref[hh, 4] = jnp.exp(lgb * pos) * scale
        chunk_decay.append((jnp.exp(lgf * CHUNK), jnp.exp(lgb * CHUNK)))
    for st in st_refs:
        st[...] = jnp.zeros_like(st)

    def load(ci, hh):
        rows = pl.ds(pl.multiple_of(ci * CHUNK, CHUNK), CHUNK)
        lanes = slice(hh * LANES, (hh + 1) * LANES)
        return rows, lanes, q_ref[0, rows, lanes], k_ref[0, rows, lanes], v_ref[0, rows, lanes].astype(BF16)

    def fwd_part(ci, hh):
        rows, lanes, q, k, vb = load(ci, hh)
        st = st_refs[2 * hh]
        a = lax.dot_general(q.astype(BF16), k.astype(BF16), _NT, preferred_element_type=F32)
        o = jnp.dot((a * const_ref[hh, 0]).astype(BF16), vb, preferred_element_type=F32)
        o = o + lax.dot_general((q * const_ref[hh, 1]).astype(BF16), st[...].astype(BF16), _NT,
                                preferred_element_type=F32)
        st[...] = st[...] * chunk_decay[hh][0] + lax.dot_general(
            vb, (k * const_ref[hh, 2]).astype(BF16), _TN, preferred_element_type=F32)
        return rows, lanes, o

    def bwd_part(ci, hh):
        rows, lanes, q, k, vb = load(ci, hh)
        st = st_refs[2 * hh + 1]
        o = lax.dot_general((q * const_ref[hh, 3]).astype(BF16), st[...].astype(BF16), _NT,
                            preferred_element_type=F32)
        st[...] = st[...] * chunk_decay[hh][1] + lax.dot_general(
            vb, (k * const_ref[hh, 4]).astype(BF16), _TN, preferred_element_type=F32)
        return rows, lanes, o

    def finish(rows, lanes, hh, o):
        o = o + acc_refs[hh][rows, :]
        oc = o - jnp.mean(o, axis=-1, keepdims=True)
        y = oc * lax.rsqrt(jnp.mean(oc * oc, axis=-1, keepdims=True) + HEAD_NORM_EPS)
        o_ref[0, rows, lanes] = (y * ng_ref[hh] * _silu(g_ref[0, rows, lanes])).astype(o_ref.dtype)

    def first_half(j, carry):
        for hh in range(nh):
            rows, _, o = fwd_part(j, hh)
            acc_refs[hh][rows, :] = o
            rows, _, o = bwd_part(nc - 1 - j, hh)
            acc_refs[hh][rows, :] = o
        return carry

    def second_half(j, carry):
        for hh in range(nh):
            rows, lanes, o = fwd_part(j, hh)
            finish(rows, lanes, hh, o)
            rows, lanes, o = bwd_part(nc - 1 - j, hh)
            finish(rows, lanes, hh, o)
        return carry

    lax.fori_loop(0, nc // 2, first_half, 0)
    lax.fori_loop(nc // 2, nc, second_half, 0)


def retention(proj, decay_logit, norm_g):
    B, S, _ = proj.shape
    nh = RET_HEADS_PER_STEP
    assert (S // CHUNK) % 2 == 0 and REC_HEADS % nh == 0
    col = lambda c: pl.BlockSpec((1, S, nh * LANES), lambda b, h: (b, 0, c // nh + h))
    return pl.pallas_call(
        _retention_kernel,
        grid=(B, REC_HEADS // nh),
        in_specs=[pl.BlockSpec(memory_space=pltpu.SMEM),
                  col(COL_RQ), col(COL_RK), col(COL_RV), col(COL_RG),
                  pl.BlockSpec((nh, 1, LANES), lambda b, h: (h, 0, 0))],
        out_specs=pl.BlockSpec((1, S, nh * LANES), lambda b, h: (b, 0, h)),
        out_shape=jax.ShapeDtypeStruct((B, S, W_GROUP), BF16),
        scratch_shapes=([pltpu.VMEM((nh, 5, CHUNK, REC_DIM), F32)]
                        + [pltpu.VMEM((S, REC_DIM), F32)] * nh
                        + [pltpu.VMEM((REC_DIM, REC_DIM), F32)] * (2 * nh)),
        compiler_params=_params(("parallel", "parallel")),
        name="retention",
    )(decay_logit.astype(F32), proj, proj, proj, proj,
      norm_g.astype(F32).reshape(REC_HEADS, 1, LANES))


ROUTE_ROWS = 8

def _first_max(vals):
    best, idx = vals[0], jnp.zeros(vals[0].shape, jnp.int32)
    for i in range(1, len(vals)):
        take = vals[i] > best
        best = jnp.where(take, vals[i], best)
        idx = jnp.where(take, i, idx)
    return best, idx


def _pick(idx, vals):
    out = vals[-1]
    for i in range(len(vals) - 2, -1, -1):
        out = jnp.where(idx == i, vals[i], out)
    return out


def _out_proj_kernel(ya_ref, yc_ref, yg_ref, yr_ref, w_ref, h_ref, g_ref, b_ref, rwh_ref, rwl_ref,
                     rb_ref, h1_ref, h1p_ref, ri_ref, rwt_ref, cnt_ref, carry_ref, ycat_ref):
    tm = h_ref.shape[0]

    @pl.when(pl.program_id(0) == 0)
    def _():
        carry_ref[...] = jnp.zeros_like(carry_ref)

    ycat_ref[:, 0:W_GROUP] = ya_ref[...]
    ycat_ref[:, W_GROUP:2 * W_GROUP] = yc_ref[...]
    ycat_ref[:, 2 * W_GROUP:3 * W_GROUP] = yg_ref[...]
    ycat_ref[:, 3 * W_GROUP:4 * W_GROUP] = yr_ref[...]
    mix = jnp.dot(ycat_ref[...], w_ref[...], preferred_element_type=F32)
    h1 = _layer_norm(DEEPNORM_ALPHA * h_ref[...] + mix, g_ref[...], b_ref[...])
    h1_ref[...] = h1
    h1p_ref[...] = _pack_rows(h1)

    h1_hi = h1.astype(BF16)
    h1_lo = (h1 - h1_hi.astype(F32)).astype(BF16)
    logits = (lax.dot_general(rwh_ref[...], h1_hi, _NT, preferred_element_type=F32)
              + lax.dot_general(rwh_ref[...], h1_lo, _NT, preferred_element_type=F32)
              + lax.dot_general(rwl_ref[...], h1_hi, _NT, preferred_element_type=F32)
              + rb_ref[...])
    rows = [logits[e:e + 1, :] for e in range(N_EXPERTS)]
    mx = functools.reduce(jnp.maximum, rows)
    ex = [jnp.exp(r - mx) for r in rows]
    den = functools.reduce(jnp.add, ex)
    pr = [x / den for x in ex]
    scores = []
    for g in range(N_GROUPS):
        a, b, c, d = pr[4 * g:4 * g + 4]
        hi1, lo1, hi2, lo2 = jnp.maximum(a, b), jnp.minimum(a, b), jnp.maximum(c, d), jnp.minimum(c, d)
        scores.append(jnp.maximum(hi1, hi2) + jnp.maximum(jnp.minimum(hi1, hi2), jnp.maximum(lo1, lo2)))
    _, gsel = _first_max(scores)
    cand = [_pick(gsel, [pr[4 * g + i] for g in range(N_GROUPS)]) for i in range(EXPERTS_PER_GROUP)]
    p0, i0 = _first_max(cand)
    p1, i1 = _first_max([jnp.where(i0 == i, -1.0, cand[i]) for i in range(EXPERTS_PER_GROUP)])
    e0 = gsel * EXPERTS_PER_GROUP + i0
    e1 = gsel * EXPERTS_PER_GROUP + i1
    tot = p0 + p1
    w0, w1 = p0 / tot, p1 / tot

    ind = jnp.concatenate([((e0 == e) | (e1 == e)).astype(F32) for e in range(N_EXPERTS)], axis=0)
    before = (lax.broadcasted_iota(jnp.int32, (tm, tm), 0)
              < lax.broadcasted_iota(jnp.int32, (tm, tm), 1)).astype(BF16)
    rank = jnp.dot(ind.astype(BF16), before, preferred_element_type=F32) + carry_ref[...]
    carry_ref[...] = carry_ref[...] + jnp.sum(ind, axis=-1, keepdims=True)
    rk = [rank[e:e + 1, :] for e in range(N_EXPERTS)]
    r0 = _pick(e0, rk).astype(jnp.int32)
    r1 = _pick(e1, rk).astype(jnp.int32)
    zi = jnp.zeros((ROUTE_ROWS - 4, tm), jnp.int32)
    ri_ref[...] = jnp.concatenate([e0, e1, r0, r1, zi], axis=0)
    rwt_ref[...] = jnp.concatenate([w0, w1, jnp.zeros((ROUTE_ROWS - 2, tm), F32)], axis=0)
    cnt_ref[...] = jnp.broadcast_to(carry_ref[...], cnt_ref.shape)


def out_proj_ln_route(ys, w_out_bf16, layer, h, g, b, router_w, router_b, tm=512):
    T, D = h.shape
    tm = min(tm, T)
    part = pl.BlockSpec((tm, W_GROUP), lambda i: (i, 0))
    row = pl.BlockSpec((tm, D), lambda i: (i, 0))
    vec = pl.BlockSpec((1, D), lambda i: (0, 0))
    route = pl.BlockSpec((ROUTE_ROWS, tm), lambda i: (0, i))
    once = pl.Buffered(1)
    rw_t = router_w.astype(F32).T
    rw_hi = rw_t.astype(BF16)
    rw_lo = (rw_t - rw_hi.astype(F32)).astype(BF16)
    return pl.pallas_call(
        _out_proj_kernel,
        grid=(T // tm,),
        in_specs=[part, part, part, part,
                  pl.BlockSpec((None, D, D), lambda i: (layer, 0, 0), pipeline_mode=once),
                  row, vec, vec,
                  pl.BlockSpec((N_EXPERTS, D), lambda i: (0, 0), pipeline_mode=once),
                  pl.BlockSpec((N_EXPERTS, D), lambda i: (0, 0), pipeline_mode=once),
                  pl.BlockSpec((N_EXPERTS, 1), lambda i: (0, 0))],
        out_specs=[row, pl.BlockSpec((tm, D // 2), lambda i: (i, 0)), route, route,
                   pl.BlockSpec((N_EXPERTS, LANES), lambda i: (0, 0))],
        out_shape=[jax.ShapeDtypeStruct((T, D), F32),
                   jax.ShapeDtypeStruct((T, D // 2), jnp.uint32),
                   jax.ShapeDtypeStruct((ROUTE_ROWS, T), jnp.int32),
                   jax.ShapeDtypeStruct((ROUTE_ROWS, T), F32),
                   jax.ShapeDtypeStruct((N_EXPERTS, LANES), F32)],
        scratch_shapes=[pltpu.VMEM((N_EXPERTS, 1), F32), pltpu.VMEM((tm, D), BF16)],
        compiler_params=_params(("arbitrary",)),
        name="out_proj_ln_route",
    )(*[y.reshape(T, W_GROUP) for y in ys], w_out_bf16, h, g.reshape(1, D), b.reshape(1, D),
      rw_hi, rw_lo, router_b.astype(F32).reshape(N_EXPERTS, 1))


ROW_DMA_UNROLL = 4


DISPATCH_BUFFERS = 3


def _dispatch_kernel(pad_start_ref, pad_len_ref, nv_ref, pos_ref, h_hbm, x_hbm, hbuf_ref, zero_ref,
                     load_sems, row_sems, zsem):
    tm = hbuf_ref.shape[1]
    tile = zero_ref.shape[0]
    n_tiles = x_hbm.shape[0] // tile
    step = pl.program_id(0)
    n_steps = pl.num_programs(0)

    def load(t):
        return pltpu.make_async_copy(h_hbm.at[pl.ds(pl.multiple_of(t * tm, tm), tm), :],
                                     hbuf_ref.at[t % DISPATCH_BUFFERS],
                                     load_sems.at[t % DISPATCH_BUFFERS])

    def rows_done(t):
        for slot in range(2):
            pltpu.make_async_copy(hbuf_ref.at[t % DISPATCH_BUFFERS], x_hbm.at[pl.ds(0, tm), :],
                                  row_sems.at[t % DISPATCH_BUFFERS, slot]).wait()

    @pl.when(step == 0)
    def _():
        load(0).start()

        @pl.when(n_steps > 1)
        def _():
            load(1).start()

    load(step).wait()
    cur = step % DISPATCH_BUFFERS

    def issue(i, carry):
        for u in range(ROW_DMA_UNROLL):
            r = i * ROW_DMA_UNROLL + u
            for slot in range(2):
                pltpu.make_async_copy(hbuf_ref.at[cur, pl.ds(r, 1), :],
                                      x_hbm.at[pl.ds(pos_ref[0, 0, slot * tm + r], 1), :],
                                      row_sems.at[cur, slot]).start(priority=slot)
        return carry

    lax.fori_loop(0, tm // ROW_DMA_UNROLL, issue, 0)

    @pl.when(step >= 1)
    def _():
        rows_done(step - 1)

    @pl.when(step + 2 < n_steps)
    def _():
        load(step + 2).start()

    @pl.when(step == n_steps - 1)
    def _():
        rows_done(step)

    @pl.when(step == 0)
    def _():
        zero_ref[...] = jnp.zeros_like(zero_ref)

        def tcopy(i):
            return pltpu.make_async_copy(
                zero_ref, x_hbm.at[pl.ds(pl.multiple_of(i * tile, tile), tile), :], zsem)

        def tissue(i, carry):
            tcopy(i).start()
            return carry

        def twait(i, carry):
            tcopy(i).wait()
            return carry

        lax.fori_loop(nv_ref[0], n_tiles, tissue, 0)
        lax.fori_loop(nv_ref[0], n_tiles, twait, 0)
        for e in range(N_EXPERTS):
            def zcopy(i):
                return pltpu.make_async_copy(
                    zero_ref.at[pl.ds(0, 1), :], x_hbm.at[pl.ds(pad_start_ref[e] + i, 1), :], zsem)

            def zissue(i, carry):
                zcopy(i).start()
                return carry

            def zwait(i, carry):
                zcopy(i).wait()
                return carry

            lax.fori_loop(0, pad_len_ref[e], zissue, 0)
            lax.fori_loop(0, pad_len_ref[e], zwait, 0)


def dispatch(h1, pos, pad_start, pad_len, n_valid, n_rows, tile, tm=256):
    T, D = h1.shape
    tm = min(tm, T)
    return pl.pallas_call(
        _dispatch_kernel,
        grid_spec=pltpu.PrefetchScalarGridSpec(
            num_scalar_prefetch=3,
            grid=(T // tm,),
            in_specs=[pl.BlockSpec((1, 1, 2 * tm), lambda i, *_: (i, 0, 0), memory_space=pltpu.SMEM),
                      pl.BlockSpec(memory_space=pl.ANY)],
            out_specs=pl.BlockSpec(memory_space=pl.ANY),
            scratch_shapes=[pltpu.VMEM((DISPATCH_BUFFERS, tm, D), h1.dtype),
                            pltpu.VMEM((tile, D), h1.dtype),
                            pltpu.SemaphoreType.DMA((DISPATCH_BUFFERS,)),
                            pltpu.SemaphoreType.DMA((DISPATCH_BUFFERS, 2)),
                            pltpu.SemaphoreType.DMA]),
        out_shape=jax.ShapeDtypeStruct((n_rows, D), h1.dtype),
        compiler_params=_params(("arbitrary",)),
        name="dispatch",
    )(pad_start, pad_len, n_valid, pos, h1)


def _expert_kernel(layer, te_ref, nv_ref, nxt_ref, x_ref, wg_hbm, wu_hbm, wd_hbm, y_ref,
                   sg_ref, su_ref, sd_ref, wg_ref, wu_ref, wd_ref, sems):
    i = pl.program_id(0)
    valid = i < nv_ref[0]

    def stage(e):
        return [pltpu.make_async_copy(src.at[layer, e], dst, sems.at[k])
                for k, (src, dst) in enumerate(((wg_hbm, sg_ref), (wu_hbm, su_ref), (wd_hbm, sd_ref)))]

    @pl.when(i == 0)
    def _():
        for c in stage(te_ref[0]):
            c.start()

    first_of_expert = valid & ((i == 0) | (te_ref[jnp.maximum(i - 1, 0)] != te_ref[i]))

    @pl.when(first_of_expert)
    def _():
        for c in stage(te_ref[i]):
            c.wait()
        wg_ref[...] = sg_ref[...].astype(BF16)
        wu_ref[...] = su_ref[...].astype(BF16)
        wd_ref[...] = sd_ref[...].astype(BF16)

        @pl.when(nxt_ref[i] >= 0)
        def _():
            for c in stage(nxt_ref[i]):
                c.start()

    @pl.when(valid)
    def _():
        x = _unpack_rows(x_ref[...]).astype(BF16)
        gate = jnp.dot(x, wg_ref[...], preferred_element_type=F32)
        up = jnp.dot(x, wu_ref[...], preferred_element_type=F32)
        hid = (_silu(gate) * up).astype(BF16)
        y_ref[...] = _pack_rows(jnp.dot(hid, wd_ref[...], preferred_element_type=F32))

    @pl.when(jnp.logical_not(valid))
    def _():
        y_ref[...] = jnp.zeros_like(y_ref)


def expert_ffn(x_sorted, tile_expert, n_valid, next_expert, w_gate, w_up, w_down, layer, tm):
    A, DP = x_sorted.shape
    D = 2 * DP
    n_tiles = A // tm
    tile = lambda i, te, nv, nx: (jnp.minimum(i, nv[0] - 1), 0)
    out_tile = lambda i, te, nv, nx: (i, 0)
    anywhere = pl.BlockSpec(memory_space=pl.ANY)
    return pl.pallas_call(
        functools.partial(_expert_kernel, layer),
        grid_spec=pltpu.PrefetchScalarGridSpec(
            num_scalar_prefetch=3,
            grid=(n_tiles,),
            in_specs=[pl.BlockSpec((tm, DP), tile), anywhere, anywhere, anywhere],
            out_specs=pl.BlockSpec((tm, DP), out_tile),
            scratch_shapes=[pltpu.VMEM((D, D_EXPERT), F32), pltpu.VMEM((D, D_EXPERT), F32),
                            pltpu.VMEM((D_EXPERT, D), F32),
                            pltpu.VMEM((D, D_EXPERT), BF16), pltpu.VMEM((D, D_EXPERT), BF16),
                            pltpu.VMEM((D_EXPERT, D), BF16),
                            pltpu.SemaphoreType.DMA((3,))]),
        out_shape=jax.ShapeDtypeStruct((A, DP), jnp.uint32),
        compiler_params=_params(("arbitrary",)),
        name="expert_ffn",
    )(tile_expert, n_valid, next_expert, x_sorted, w_gate, w_up, w_down)


def _combine_kernel(pos_ref, pos_next_ref, h_ref, w_ref, g_ref, b_ref, y_hbm, o_ref, ob_ref,
                    buf_ref, sems):
    tm = h_ref.shape[0]
    step = pl.program_id(0)
    cur = step % 2

    def gather(p_ref, half):
        def issue(i, carry):
            for u in range(ROW_DMA_UNROLL):
                r = i * ROW_DMA_UNROLL + u
                for slot in range(2):
                    pltpu.make_async_copy(y_hbm.at[pl.ds(p_ref[0, 0, slot * tm + r], 1), :],
                                          buf_ref.at[half, slot, pl.ds(r, 1), :],
                                          sems.at[half, slot]).start(priority=slot)
            return carry

        lax.fori_loop(0, tm // ROW_DMA_UNROLL, issue, 0)

    @pl.when(step == 0)
    def _():
        gather(pos_ref, 0)

    @pl.when(step + 1 < pl.num_programs(0))
    def _():
        gather(pos_next_ref, 1 - cur)

    for slot in range(2):
        pltpu.make_async_copy(y_hbm.at[pl.ds(0, tm), :], buf_ref.at[cur, slot],
                              sems.at[cur, slot]).wait()
    w = w_ref[...]
    ffn = (w[:, 0:1] * _unpack_rows(buf_ref[cur, 0]) + w[:, 1:2] * _unpack_rows(buf_ref[cur, 1]))
    h2 = _layer_norm(DEEPNORM_ALPHA * h_ref[...] + ffn, g_ref[...], b_ref[...])
    o_ref[...] = h2
    ob_ref[...] = h2.astype(BF16)


def combine_ln(h1, y_sorted, pos, wcol, g, b, tm=256):
    T, D = h1.shape
    tm = min(tm, T)
    row = pl.BlockSpec((tm, D), lambda i: (i, 0))
    vec = pl.BlockSpec((1, D), lambda i: (0, 0))
    last = T // tm - 1
    return pl.pallas_call(
        _combine_kernel,
        grid=(T // tm,),
        in_specs=[pl.BlockSpec((1, 1, 2 * tm), lambda i: (i, 0, 0), memory_space=pltpu.SMEM),
                  pl.BlockSpec((1, 1, 2 * tm), lambda i: (jnp.minimum(i + 1, last), 0, 0),
                               memory_space=pltpu.SMEM),
                  row,
                  pl.BlockSpec((tm, 2), lambda i: (i, 0)),
                  vec, vec,
                  pl.BlockSpec(memory_space=pl.ANY)],
        out_specs=[row, row],
        out_shape=[jax.ShapeDtypeStruct((T, D), F32), jax.ShapeDtypeStruct((T, D), BF16)],
        scratch_shapes=[pltpu.VMEM((2, 2, tm, D // 2), jnp.uint32),
                        pltpu.SemaphoreType.DMA((2, 2))],
        compiler_params=_params(("arbitrary",)),
        name="combine_ln",
    )(pos, pos, h1, wcol, g.reshape(1, D), b.reshape(1, D), y_sorted)


EXPERT_TILE = 256
ROUTE_TILE = 256


def _routing_plan(route_i, counts, T, tile, route_tile):
    cnt = counts[:, 0].astype(jnp.int32)
    padded = ((cnt + tile - 1) // tile) * tile
    ends = jnp.cumsum(padded)
    offs = ends - padded
    e0, e1, r0, r1 = route_i[0], route_i[1], route_i[2], route_i[3]
    pos0 = offs[e0] + r0
    pos1 = offs[e1] + r1
    nrt = T // route_tile
    pos = jnp.concatenate([pos0.reshape(nrt, 1, route_tile), pos1.reshape(nrt, 1, route_tile)], axis=-1)
    n_tiles = (2 * T) // tile + N_EXPERTS
    tile_ids = jnp.arange(n_tiles, dtype=jnp.int32)
    tile_expert = jnp.minimum(
        jnp.sum((ends[None, :] // tile <= tile_ids[:, None]).astype(jnp.int32), axis=1),
        N_EXPERTS - 1).astype(jnp.int32)
    n_valid = (ends[-1] // tile).astype(jnp.int32).reshape(1)
    ids = jnp.arange(N_EXPERTS, dtype=jnp.int32)
    later = (ids[None, :] > ids[:, None]) & (cnt[None, :] > 0)
    following = jnp.min(jnp.where(later, ids[None, :], N_EXPERTS), axis=1)
    following = jnp.where(following == N_EXPERTS, -1, following).astype(jnp.int32)
    return (pos, tile_expert, n_valid, following[tile_expert],
            (offs + cnt).astype(jnp.int32), (padded - cnt).astype(jnp.int32))


def _hgrn_lower_bounds(hgrn_lb):
    lb = jnp.cumsum(jax.nn.softmax(hgrn_lb.astype(F32), axis=0), axis=0)
    return lb - lb[0:1]


def kernel(x, emb_ln_g, emb_ln_b, w_in, attn_sink, conv_w, hgrn_lb, hgrn_norm_g, ret_decay_logit,
           ret_norm_g, w_out, ln1_g, ln1_b, router_w, router_b, w_gate, w_up, w_down, ln2_g, ln2_b):
    B, S, D = x.shape
    T = B * S
    depth = w_in.shape[0]
    lb_all = _hgrn_lower_bounds(hgrn_lb)
    route_tile = min(ROUTE_TILE, T)
    n_rows = 2 * T + N_EXPERTS * EXPERT_TILE

    w_out_b = w_out.astype(BF16)
    h, hb = embed_ln(x.reshape(T, D), emb_ln_g, emb_ln_b)
    for l in range(depth):
        proj = in_proj(hb, w_in, l).reshape(B, S, D_IN_PROJ)
        ys = [attention(proj, attn_sink[l]),
              short_conv(proj, conv_w[l]),
              hgrn2(proj, lb_all[l], hgrn_norm_g[l]),
              retention(proj, ret_decay_logit[l], ret_norm_g[l])]
        h1, h1_packed, route_i, route_w, counts = out_proj_ln_route(
            ys, w_out_b, l, h, ln1_g[l], ln1_b[l], router_w, router_b)
        pos, tile_expert, n_valid, next_expert, pad_start, pad_len = _routing_plan(
            route_i, counts, T, EXPERT_TILE, route_tile)
        x_sorted = dispatch(h1_packed, pos, pad_start, pad_len, n_valid, n_rows, EXPERT_TILE,
                            tm=route_tile)
        y_sorted = expert_ffn(x_sorted, tile_expert, n_valid, next_expert, w_gate, w_up, w_down, l,
                              EXPERT_TILE)
        h, hb = combine_ln(h1, y_sorted, pos, route_w[0:2].T, ln2_g[l], ln2_b[l], tm=route_tile)
    return h.reshape(B, S, D)
```

```python
import functools

import jax
import jax.numpy as jnp
from jax import lax
from jax.experimental import pallas as pl
from jax.experimental.pallas import tpu as pltpu

F32 = jnp.float32
BF16 = jnp.bfloat16

D_MODEL = 2048
DEPTH = 2
W_GROUP = 512
HEAD_DIM = 64
N_ATTN_HEADS = 8
N_KV_HEADS = 2
ATTN_GROUP = N_ATTN_HEADS // N_KV_HEADS
WINDOW = 128
ATTN_BLOCK = 128
ATTN_STACK = 4
REC_HEADS = 4
REC_DIM = 128
N_EXPERTS = 16
N_GROUPS = 4
EXPERTS_PER_GROUP = 4
D_EXPERT = 1024
D_IN_PROJ = 6912
DEEPNORM_ALPHA = (2.0 * DEPTH) ** 0.25
LN_EPS = 1e-5
HEAD_NORM_EPS = 1e-6
NEG_BIG = -1e30

LANES = 128
SUBLANES = 8
COL_AQ, COL_AK, COL_AV = 0, 4, 5
COL_CB, COL_CC, COL_CH = 6, 10, 14
COL_GQ, COL_GZF, COL_GZB, COL_GI, COL_GO = 18, 22, 26, 30, 34
COL_RQ, COL_RK, COL_RV, COL_RG = 38, 42, 46, 50

CHUNK = 128
VMEM_LIMIT = 56 * 1024 * 1024

_NT = (((1,), (1,)), ((), ()))
_TN = (((0,), (0,)), ((), ()))


def _params(sem, vmem=VMEM_LIMIT):
    return pltpu.CompilerParams(dimension_semantics=sem, vmem_limit_bytes=vmem)


def _layer_norm(x, g, b):
    mu = jnp.mean(x, axis=-1, keepdims=True)
    xc = x - mu
    var = jnp.mean(xc * xc, axis=-1, keepdims=True)
    return xc * lax.rsqrt(var + LN_EPS) * g + b


def _silu(x):
    return x * (1.0 / (1.0 + jnp.exp(-x)))


def _pack_rows(x):
    n = x.shape[1] // 2
    hi = lax.bitcast_convert_type(x[:, :n].astype(BF16).astype(F32), jnp.uint32)
    lo = lax.bitcast_convert_type(x[:, n:].astype(BF16).astype(F32), jnp.uint32)
    return hi | (lo >> 16)


def _unpack_rows(w):
    hi = lax.bitcast_convert_type(w & jnp.uint32(0xFFFF0000), F32)
    lo = lax.bitcast_convert_type(w << 16, F32)
    return jnp.concatenate([hi, lo], axis=-1)


def _embed_ln_kernel(x_ref, g_ref, b_ref, h_ref, hb_ref):
    h = _layer_norm(x_ref[...], g_ref[...], b_ref[...])
    h_ref[...] = h
    hb_ref[...] = h.astype(BF16)


def embed_ln(x2, g, b, tm=1024):
    T, D = x2.shape
    return pl.pallas_call(
        _embed_ln_kernel,
        grid=(T // tm,),
        in_specs=[pl.BlockSpec((tm, D), lambda i: (i, 0)),
                  pl.BlockSpec((1, D), lambda i: (0, 0)),
                  pl.BlockSpec((1, D), lambda i: (0, 0))],
        out_specs=[pl.BlockSpec((tm, D), lambda i: (i, 0)),
                   pl.BlockSpec((tm, D), lambda i: (i, 0))],
        out_shape=[jax.ShapeDtypeStruct((T, D), F32), jax.ShapeDtypeStruct((T, D), BF16)],
        compiler_params=_params(("parallel",)),
        name="embed_ln",
    )(x2, g.reshape(1, D), b.reshape(1, D))


def _in_proj_kernel(x_ref, w_ref, o_ref, wb_ref):
    @pl.when(pl.program_id(1) == 0)
    def _():
        wb_ref[...] = w_ref[...].astype(BF16)

    o_ref[...] = jnp.dot(x_ref[...], wb_ref[...], preferred_element_type=F32)


def in_proj(hb, w_in, layer, tm=2048, tn=768):
    T, K = hb.shape
    N = w_in.shape[2]
    tm = min(tm, T)
    return pl.pallas_call(
        _in_proj_kernel,
        grid=(N // tn, T // tm),
        in_specs=[pl.BlockSpec((tm, K), lambda n, m: (m, 0)),
                  pl.BlockSpec((None, K, tn), lambda n, m: (layer, 0, n))],
        out_specs=pl.BlockSpec((tm, tn), lambda n, m: (m, n)),
        out_shape=jax.ShapeDtypeStruct((T, N), F32),
        scratch_shapes=[pltpu.VMEM((K, tn), BF16)],
        compiler_params=_params(("arbitrary", "arbitrary")),
        name="in_proj",
    )(hb, w_in)


def _attn_kernel(sink_ref, q_ref, kp_ref, kc_ref, kn_ref, vp_ref, vc_ref, vn_ref, bias_ref, o_ref):
    L = ATTN_BLOCK
    n = pl.program_id(1)
    nb = pl.num_programs(1)
    key = lax.broadcasted_iota(jnp.int32, (3 * L, 1), 0)
    valid = ((key >= L) | (n > 0)) & ((key < 2 * L) | (n < nb - 1))
    edge = jnp.where(valid, 0.0, NEG_BIG)
    q = q_ref[0]
    k3 = jnp.concatenate([kp_ref[0], kc_ref[0], kn_ref[0]], axis=0)
    v3 = jnp.concatenate([vp_ref[0], vc_ref[0], vn_ref[0]], axis=0)
    outs = []
    for h in range(N_KV_HEADS):
        kh = k3[:, h * HEAD_DIM:(h + 1) * HEAD_DIM].astype(BF16)
        vh = v3[:, h * HEAD_DIM:(h + 1) * HEAD_DIM].astype(BF16)
        for g0 in range(0, ATTN_GROUP, ATTN_STACK):
            heads = range(h * ATTN_GROUP + g0, h * ATTN_GROUP + g0 + ATTN_STACK)
            qs = jnp.concatenate([q[:, hd * HEAD_DIM:(hd + 1) * HEAD_DIM] for hd in heads], axis=0)
            qs = (qs * (HEAD_DIM ** -0.5)).astype(BF16)
            s = lax.dot_general(kh, qs, _NT, preferred_element_type=F32)
            s = s + bias_ref[h, :, g0 * L:(g0 + ATTN_STACK) * L] + edge
            sk = jnp.concatenate([jnp.full((1, L), sink_ref[hd], F32) for hd in heads], axis=1)
            m = jnp.maximum(jnp.max(s, axis=0, keepdims=True), sk)
            p = jnp.exp(s - m)
            den = jnp.sum(p, axis=0, keepdims=True) + jnp.exp(sk - m)
            o_t = lax.dot_general(vh, p.astype(BF16), _TN, preferred_element_type=F32) / den
            outs.extend(o_t[:, g * L:(g + 1) * L].T for g in range(ATTN_STACK))
    o_ref[0] = jnp.concatenate(outs, axis=-1).astype(o_ref.dtype)


def _attn_bias():
    L = ATTN_BLOCK
    k_rel = jnp.arange(3 * L) - L
    dist = jnp.abs(k_rel[None, :] - jnp.arange(L)[:, None]).astype(F32)
    slopes = 2.0 ** (-8.0 * jnp.arange(1, N_ATTN_HEADS + 1, dtype=F32) / N_ATTN_HEADS)
    bias = -slopes[:, None, None] * dist[None]
    bias = jnp.where(dist[None] <= WINDOW, bias, NEG_BIG)
    bias = bias.reshape(N_KV_HEADS, ATTN_GROUP, L, 3 * L).transpose(0, 3, 1, 2)
    return bias.reshape(N_KV_HEADS, 3 * L, ATTN_GROUP * L)


def attention(proj, sink):
    B, S, _ = proj.shape
    L = ATTN_BLOCK
    nb = S // L
    kv = lambda col, shift: pl.BlockSpec(
        (1, L, LANES), lambda b, n: (b, jnp.clip(n + shift, 0, nb - 1), col))
    return pl.pallas_call(
        _attn_kernel,
        grid=(B, nb),
        in_specs=[pl.BlockSpec(memory_space=pltpu.SMEM),
                  pl.BlockSpec((1, L, W_GROUP), lambda b, n: (b, n, COL_AQ // 4)),
                  kv(COL_AK, -1), kv(COL_AK, 0), kv(COL_AK, 1),
                  kv(COL_AV, -1), kv(COL_AV, 0), kv(COL_AV, 1),
                  pl.BlockSpec((N_KV_HEADS, 3 * L, ATTN_GROUP * L), lambda b, n: (0, 0, 0))],
        out_specs=pl.BlockSpec((1, L, W_GROUP), lambda b, n: (b, n, 0)),
        out_shape=jax.ShapeDtypeStruct((B, S, W_GROUP), BF16),
        compiler_params=_params(("parallel", "arbitrary")),
        name="attention",
    )(sink.astype(F32), proj, proj, proj, proj, proj, proj, proj, _attn_bias())


CONV_ROWS = 512
HALO = 8


def _conv_kernel(b_ref, c_ref, h_ref, w_ref, o_ref, u_ref):
    S = b_ref.shape[1]
    R = min(CONV_ROWS, S)
    u_ref[0:HALO, :] = jnp.zeros((HALO, LANES), F32)
    u_ref[S + HALO:S + 2 * HALO, :] = jnp.zeros((HALO, LANES), F32)

    def gate(i, carry):
        r = pl.multiple_of(i * R, R)
        u_ref[pl.ds(r + HALO, R), :] = c_ref[0, pl.ds(r, R), :] * h_ref[0, pl.ds(r, R), :]
        return carry

    lax.fori_loop(0, S // R, gate, 0)
    w0, w1, w2 = w_ref[0:1, :], w_ref[1:2, :], w_ref[2:3, :]

    def conv(i, carry):
        r = pl.multiple_of(i * R, R)
        a = u_ref[pl.ds(r, R + 2 * HALO), :]
        prev = pltpu.roll(a, 1, 0)[HALO:HALO + R]
        nxt = pltpu.roll(a, R + 2 * HALO - 1, 0)[HALO:HALO + R]
        y = w0 * prev + w1 * a[HALO:HALO + R] + w2 * nxt
        o_ref[0, pl.ds(r, R), :] = (b_ref[0, pl.ds(r, R), :] * y).astype(o_ref.dtype)
        return carry

    lax.fori_loop(0, S // R, conv, 0)


def short_conv(proj, conv_w):
    B, S, _ = proj.shape
    nj = W_GROUP // LANES
    col = lambda c: pl.BlockSpec((1, S, LANES), lambda b, j: (b, 0, c + j))
    return pl.pallas_call(
        _conv_kernel,
        grid=(B, nj),
        in_specs=[col(COL_CB), col(COL_CC), col(COL_CH),
                  pl.BlockSpec((3, LANES), lambda b, j: (0, j))],
        out_specs=pl.BlockSpec((1, S, LANES), lambda b, j: (b, 0, j)),
        out_shape=jax.ShapeDtypeStruct((B, S, W_GROUP), BF16),
        scratch_shapes=[pltpu.VMEM((S + 2 * HALO, LANES), F32)],
        compiler_params=_params(("parallel", "parallel")),
        name="short_conv",
    )(proj, proj, proj, conv_w.astype(F32))


def _level_codes():
    t = lax.broadcasted_iota(jnp.int32, (CHUNK, CHUNK), 0)
    s = lax.broadcasted_iota(jnp.int32, (CHUNK, CHUNK), 1)
    x = t ^ s
    hb = jnp.zeros((CHUNK, CHUNK), jnp.int32)
    c = 1
    while c < CHUNK:
        hb = jnp.where((x & c) != 0, c, hb)
        c *= 2
    diag = jnp.where(t == s, 0, -1)
    return jnp.where(t > s, hb, diag), jnp.where(t < s, hb, diag)


def _hgrn_chunk(q, v, z, lb, code, reverse):
    e = jnp.exp(-jnp.abs(z))
    r = 1.0 / (1.0 + e)
    er = e * r
    nonneg = z >= 0
    f = lb + (1.0 - lb) * jnp.where(nonneg, r, er)
    k = (1.0 - lb) * jnp.where(nonneg, er, r)
    row = lax.broadcasted_iota(jnp.int32, (CHUNK, REC_DIM), 0)
    qs = f
    ks = jnp.ones_like(f)
    blk = f
    a = jnp.where(code == 0,
                  lax.dot_general(q.astype(BF16), k.astype(BF16), _NT, preferred_element_type=F32),
                  0.0)
    c = 1
    while c < CHUNK:
        p = lax.dot_general((q * qs).astype(BF16), (k * ks).astype(BF16), _NT,
                            preferred_element_type=F32)
        a = jnp.where(code == c, p, a)
        if c < SUBLANES:
            upper = (row & c) != 0
            grouped = blk.reshape(CHUNK // SUBLANES, SUBLANES, REC_DIM)
            down = pltpu.roll(grouped, c, 1).reshape(CHUNK, REC_DIM)
            up = (down if 2 * c == SUBLANES
                  else pltpu.roll(grouped, SUBLANES - c, 1).reshape(CHUNK, REC_DIM))
            sib = jnp.where(upper, down, up)
            grow_q = jnp.logical_not(upper) if reverse else upper
            qs = qs * jnp.where(grow_q, sib, 1.0)
            ks = ks * jnp.where(grow_q, 1.0, sib)
            blk = blk * sib
        else:
            step = c // SUBLANES
            group = lambda x, j: x[j * SUBLANES:(j + 1) * SUBLANES]
            nq, nk, nb = [], [], []
            for j in range(CHUNK // SUBLANES):
                upper = (j & step) != 0
                sib = group(blk, j ^ step)
                grow_q = (not upper) if reverse else upper
                nq.append(group(qs, j) * sib if grow_q else group(qs, j))
                nk.append(group(ks, j) if grow_q else group(ks, j) * sib)
                nb.append(group(blk, j) * sib)
            qs, ks, blk = (jnp.concatenate(x, axis=0) for x in (nq, nk, nb))
        c *= 2
    return a, q * qs, k * ks, blk[0:1, :]


HGRN_HEADS_PER_STEP = 2


def _hgrn_kernel(q_ref, zf_ref, zb_ref, i_ref, g_ref, lb_ref, ng_ref, o_ref, code_ref, *scratch):
    S = q_ref.shape[1]
    nc = S // CHUNK
    nh = HGRN_HEADS_PER_STEP
    acc_refs, st_refs = scratch[:nh], scratch[nh:]
    cf, cb = _level_codes()
    code_ref[0] = cf
    code_ref[1] = cb
    for st in st_refs:
        st[...] = jnp.zeros_like(st)

    def part(ci, hh, reverse):
        z_ref = zb_ref if reverse else zf_ref
        st = st_refs[2 * hh + (1 if reverse else 0)]
        rows = pl.ds(pl.multiple_of(ci * CHUNK, CHUNK), CHUNK)
        lanes = slice(hh * LANES, (hh + 1) * LANES)
        q = q_ref[0, rows, lanes]
        v = i_ref[0, rows, lanes]
        vb = v.astype(BF16)
        a, qd, kd, dec = _hgrn_chunk(q, v, z_ref[0, rows, lanes], lb_ref[hh],
                                     code_ref[1 if reverse else 0], reverse)
        o = jnp.dot(a.astype(BF16), vb, preferred_element_type=F32)
        o = o + lax.dot_general(qd.astype(BF16), st[...].astype(BF16), _NT,
                                preferred_element_type=F32)
        st[...] = st[...] * dec + lax.dot_general(vb, kd.astype(BF16), _TN,
                                                  preferred_element_type=F32)
        return rows, lanes, o

    def finish(hh, rows, lanes, o):
        o = o + acc_refs[hh][rows, :]
        y = o * lax.rsqrt(jnp.mean(o * o, axis=-1, keepdims=True) + HEAD_NORM_EPS)
        y = y * ng_ref[hh] * _silu(g_ref[0, rows, lanes])
        o_ref[0, rows, lanes] = y.astype(o_ref.dtype)

    def first_half(j, carry):
        for hh in range(nh):
            rows, _, o = part(j, hh, False)
            acc_refs[hh][rows, :] = o
            rows, _, o = part(nc - 1 - j, hh, True)
            acc_refs[hh][rows, :] = o
        return carry

    def second_half(j, carry):
        for hh in range(nh):
            finish(hh, *part(j, hh, False))
            finish(hh, *part(nc - 1 - j, hh, True))
        return carry

    lax.fori_loop(0, nc // 2, first_half, 0)
    lax.fori_loop(nc // 2, nc, second_half, 0)


def hgrn2(proj, lb, norm_g):
    B, S, _ = proj.shape
    nh = HGRN_HEADS_PER_STEP
    assert (S // CHUNK) % 2 == 0 and REC_HEADS % nh == 0
    col = lambda c: pl.BlockSpec((1, S, nh * LANES), lambda b, h: (b, 0, c // nh + h))
    vec = pl.BlockSpec((nh, 1, LANES), lambda b, h: (h, 0, 0))
    return pl.pallas_call(
        _hgrn_kernel,
        grid=(B, REC_HEADS // nh),
        in_specs=[col(COL_GQ), col(COL_GZF), col(COL_GZB), col(COL_GI), col(COL_GO), vec, vec],
        out_specs=pl.BlockSpec((1, S, nh * LANES), lambda b, h: (b, 0, h)),
        out_shape=jax.ShapeDtypeStruct((B, S, W_GROUP), BF16),
        scratch_shapes=([pltpu.VMEM((2, CHUNK, CHUNK), jnp.int32)]
                        + [pltpu.VMEM((S, REC_DIM), F32)] * nh
                        + [pltpu.VMEM((REC_DIM, REC_DIM), F32)] * (2 * nh)),
        compiler_params=_params(("parallel", "parallel")),
        name="hgrn2",
    )(proj, proj, proj, proj, proj,
      lb.astype(F32).reshape(REC_HEADS, 1, LANES), norm_g.astype(F32).reshape(REC_HEADS, 1, LANES))


RET_HEADS_PER_STEP = 2


def _retention_kernel(dl_ref, q_ref, k_ref, v_ref, g_ref, ng_ref, o_ref, const_ref, *scratch):
    S = q_ref.shape[1]
    nc = S // CHUNK
    nh = RET_HEADS_PER_STEP
    acc_refs, st_refs = scratch[:nh], scratch[nh:]
    scale = REC_DIM ** -0.5
    t = lax.broadcasted_iota(jnp.int32, (CHUNK, CHUNK), 0)
    s = lax.broadcasted_iota(jnp.int32, (CHUNK, CHUNK), 1)
    rel = (t - s).astype(F32)
    pos = lax.broadcasted_iota(jnp.int32, (CHUNK, REC_DIM), 0).astype(F32)
    chunk_decay = []
    for hh in range(nh):
        head = pl.program_id(1) * nh + hh

        def log_gamma(d):
            x = jnp.full((1, LANES), dl_ref[d, head], F32)
            return jnp.minimum(x, 0.0) - jnp.log1p(jnp.exp(-jnp.abs(x)))

        lgf, lgb = log_gamma(0), log_gamma(1)
        const_ref[hh, 0] = (jnp.where(t >= s, jnp.exp(lgf * rel), 0.0)
                            + jnp.where(s >= t, jnp.exp(-lgb * rel), 0.0)) * scale
        const_ref[hh, 1] = jnp.exp(lgf * (pos + 1.0))
        const_ref[hh, 2] = jnp.exp(lgf * (CHUNK - 1.0 - pos)) * scale
        const_ref[hh, 3] = jnp.exp(lgb * (CHUNK - pos))
        const_ref[hh, 4] = jnp.exp(lgb * pos) * scale
        chunk_decay.append((jnp.exp(lgf * CHUNK), jnp.exp(lgb * CHUNK)))
    for st in st_refs:
        st[...] = jnp.zeros_like(st)

    def load(ci, hh):
        rows = pl.ds(pl.multiple_of(ci * CHUNK, CHUNK), CHUNK)
        lanes = slice(hh * LANES, (hh + 1) * LANES)
        return rows, lanes, q_ref[0, rows, lanes], k_ref[0, rows, lanes], v_ref[0, rows, lanes].astype(BF16)

    def fwd_part(ci, hh):
        rows, lanes, q, k, vb = load(ci, hh)
        st = st_refs[2 * hh]
        a = lax.dot_general(q.astype(BF16), k.astype(BF16), _NT, preferred_element_type=F32)
        o = jnp.dot((a * const_ref[hh, 0]).astype(BF16), vb, preferred_element_type=F32)
        o = o + lax.dot_general((q * const_ref[hh, 1]).astype(BF16), st[...].astype(BF16), _NT,
                                preferred_element_type=F32)
        st[...] = st[...] * chunk_decay[hh][0] + lax.dot_general(
            vb, (k * const_ref[hh, 2]).astype(BF16), _TN, preferred_element_type=F32)
        return rows, lanes, o

    def bwd_part(ci, hh):
        rows, lanes, q, k, vb = load(ci, hh)
        st = st_refs[2 * hh + 1]
        o = lax.dot_general((q * const_ref[hh, 3]).astype(BF16), st[...].astype(BF16), _NT,
                            preferred_element_type=F32)
        st[...] = st[...] * chunk_decay[hh][1] + lax.dot_general(
            vb, (k * const_ref[hh, 4]).astype(BF16), _TN, preferred_element_type=F32)
        return rows, lanes, o

    def finish(rows, lanes, hh, o):
        o = o + acc_refs[hh][rows, :]
        oc = o - jnp.mean(o, axis=-1, keepdims=True)
        y = oc * lax.rsqrt(jnp.mean(oc * oc, axis=-1, keepdims=True) + HEAD_NORM_EPS)
        o_ref[0, rows, lanes] = (y * ng_ref[hh] * _silu(g_ref[0, rows, lanes])).astype(o_ref.dtype)

    def first_half(j, carry):
        for hh in range(nh):
            rows, _, o = fwd_part(j, hh)
            acc_refs[hh][rows, :] = o
            rows, _, o = bwd_part(nc - 1 - j, hh)
            acc_refs[hh][rows, :] = o
        return carry

    def second_half(j, carry):
        for hh in range(nh):
            rows, lanes, o = fwd_part(j, hh)
            finish(rows, lanes, hh, o)
            rows, lanes, o = bwd_part(nc - 1 - j, hh)
            finish(rows, lanes, hh, o)
        return carry

    lax.fori_loop(0, nc // 2, first_half, 0)
    lax.fori_loop(nc // 2, nc, second_half, 0)


def retention(proj, decay_logit, norm_g):
    B, S, _ = proj.shape
    nh = RET_HEADS_PER_STEP
    assert (S // CHUNK) % 2 == 0 and REC_HEADS % nh == 0
    col = lambda c: pl.BlockSpec((1, S, nh * LANES), lambda b, h: (b, 0, c // nh + h))
    return pl.pallas_call(
        _retention_kernel,
        grid=(B, REC_HEADS // nh),
        in_specs=[pl.BlockSpec(memory_space=pltpu.SMEM),
                  col(COL_RQ), col(COL_RK), col(COL_RV), col(COL_RG),
                  pl.BlockSpec((nh, 1, LANES), lambda b, h: (h, 0, 0))],
        out_specs=pl.BlockSpec((1, S, nh * LANES), lambda b, h: (b, 0, h)),
        out_shape=jax.ShapeDtypeStruct((B, S, W_GROUP), BF16),
        scratch_shapes=([pltpu.VMEM((nh, 5, CHUNK, REC_DIM), F32)]
                        + [pltpu.VMEM((S, REC_DIM), F32)] * nh
                        + [pltpu.VMEM((REC_DIM, REC_DIM), F32)] * (2 * nh)),
        compiler_params=_params(("parallel", "parallel")),
        name="retention",
    )(decay_logit.astype(F32), proj, proj, proj, proj,
      norm_g.astype(F32).reshape(REC_HEADS, 1, LANES))


ROUTE_ROWS = 8

def _first_max(vals):
    best, idx = vals[0], jnp.zeros(vals[0].shape, jnp.int32)
    for i in range(1, len(vals)):
        take = vals[i] > best
        best = jnp.where(take, vals[i], best)
        idx = jnp.where(take, i, idx)
    return best, idx


def _pick(idx, vals):
    out = vals[-1]
    for i in range(len(vals) - 2, -1, -1):
        out = jnp.where(idx == i, vals[i], out)
    return out


def _out_proj_kernel(ya_ref, yc_ref, yg_ref, yr_ref, w_ref, h_ref, g_ref, b_ref, rwh_ref, rwl_ref,
                     rb_ref, h1_ref, h1p_ref, ri_ref, rwt_ref, cnt_ref, carry_ref, ycat_ref):
    tm = h_ref.shape[0]

    @pl.when(pl.program_id(0) == 0)
    def _():
        carry_ref[...] = jnp.zeros_like(carry_ref)

    ycat_ref[:, 0:W_GROUP] = ya_ref[...]
    ycat_ref[:, W_GROUP:2 * W_GROUP] = yc_ref[...]
    ycat_ref[:, 2 * W_GROUP:3 * W_GROUP] = yg_ref[...]
    ycat_ref[:, 3 * W_GROUP:4 * W_GROUP] = yr_ref[...]
    mix = jnp.dot(ycat_ref[...], w_ref[...], preferred_element_type=F32)
    h1 = _layer_norm(DEEPNORM_ALPHA * h_ref[...] + mix, g_ref[...], b_ref[...])
    h1_ref[...] = h1
    h1p_ref[...] = _pack_rows(h1)

    h1_hi = h1.astype(BF16)
    h1_lo = (h1 - h1_hi.astype(F32)).astype(BF16)
    logits = (lax.dot_general(rwh_ref[...], h1_hi, _NT, preferred_element_type=F32)
              + lax.dot_general(rwh_ref[...], h1_lo, _NT, preferred_element_type=F32)
              + lax.dot_general(rwl_ref[...], h1_hi, _NT, preferred_element_type=F32)
              + rb_ref[...])
    rows = [logits[e:e + 1, :] for e in range(N_EXPERTS)]
    mx = functools.reduce(jnp.maximum, rows)
    ex = [jnp.exp(r - mx) for r in rows]
    den = functools.reduce(jnp.add, ex)
    pr = [x / den for x in ex]
    scores = []
    for g in range(N_GROUPS):
        a, b, c, d = pr[4 * g:4 * g + 4]
        hi1, lo1, hi2, lo2 = jnp.maximum(a, b), jnp.minimum(a, b), jnp.maximum(c, d), jnp.minimum(c, d)
        scores.append(jnp.maximum(hi1, hi2) + jnp.maximum(jnp.minimum(hi1, hi2), jnp.maximum(lo1, lo2)))
    _, gsel = _first_max(scores)
    cand = [_pick(gsel, [pr[4 * g + i] for g in range(N_GROUPS)]) for i in range(EXPERTS_PER_GROUP)]
    p0, i0 = _first_max(cand)
    p1, i1 = _first_max([jnp.where(i0 == i, -1.0, cand[i]) for i in range(EXPERTS_PER_GROUP)])
    e0 = gsel * EXPERTS_PER_GROUP + i0
    e1 = gsel * EXPERTS_PER_GROUP + i1
    tot = p0 + p1
    w0, w1 = p0 / tot, p1 / tot

    ind = jnp.concatenate([((e0 == e) | (e1 == e)).astype(F32) for e in range(N_EXPERTS)], axis=0)
    before = (lax.broadcasted_iota(jnp.int32, (tm, tm), 0)
              < lax.broadcasted_iota(jnp.int32, (tm, tm), 1)).astype(BF16)
    rank = jnp.dot(ind.astype(BF16), before, preferred_element_type=F32) + carry_ref[...]
    carry_ref[...] = carry_ref[...] + jnp.sum(ind, axis=-1, keepdims=True)
    rk = [rank[e:e + 1, :] for e in range(N_EXPERTS)]
    r0 = _pick(e0, rk).astype(jnp.int32)
    r1 = _pick(e1, rk).astype(jnp.int32)
    zi = jnp.zeros((ROUTE_ROWS - 4, tm), jnp.int32)
    ri_ref[...] = jnp.concatenate([e0, e1, r0, r1, zi], axis=0)
    rwt_ref[...] = jnp.concatenate([w0, w1, jnp.zeros((ROUTE_ROWS - 2, tm), F32)], axis=0)
    cnt_ref[...] = jnp.broadcast_to(carry_ref[...], cnt_ref.shape)


def out_proj_ln_route(ys, w_out_bf16, layer, h, g, b, router_w, router_b, tm=512):
    T, D = h.shape
    tm = min(tm, T)
    part = pl.BlockSpec((tm, W_GROUP), lambda i: (i, 0))
    row = pl.BlockSpec((tm, D), lambda i: (i, 0))
    vec = pl.BlockSpec((1, D), lambda i: (0, 0))
    route = pl.BlockSpec((ROUTE_ROWS, tm), lambda i: (0, i))
    once = pl.Buffered(1)
    rw_t = router_w.astype(F32).T
    rw_hi = rw_t.astype(BF16)
    rw_lo = (rw_t - rw_hi.astype(F32)).astype(BF16)
    return pl.pallas_call(
        _out_proj_kernel,
        grid=(T // tm,),
        in_specs=[part, part, part, part,
                  pl.BlockSpec((None, D, D), lambda i: (layer, 0, 0), pipeline_mode=once),
                  row, vec, vec,
                  pl.BlockSpec((N_EXPERTS, D), lambda i: (0, 0), pipeline_mode=once),
                  pl.BlockSpec((N_EXPERTS, D), lambda i: (0, 0), pipeline_mode=once),
                  pl.BlockSpec((N_EXPERTS, 1), lambda i: (0, 0))],
        out_specs=[row, pl.BlockSpec((tm, D // 2), lambda i: (i, 0)), route, route,
                   pl.BlockSpec((N_EXPERTS, LANES), lambda i: (0, 0))],
        out_shape=[jax.ShapeDtypeStruct((T, D), F32),
                   jax.ShapeDtypeStruct((T, D // 2), jnp.uint32),
                   jax.ShapeDtypeStruct((ROUTE_ROWS, T), jnp.int32),
                   jax.ShapeDtypeStruct((ROUTE_ROWS, T), F32),
                   jax.ShapeDtypeStruct((N_EXPERTS, LANES), F32)],
        scratch_shapes=[pltpu.VMEM((N_EXPERTS, 1), F32), pltpu.VMEM((tm, D), BF16)],
        compiler_params=_params(("arbitrary",)),
        name="out_proj_ln_route",
    )(*[y.reshape(T, W_GROUP) for y in ys], w_out_bf16, h, g.reshape(1, D), b.reshape(1, D),
      rw_hi, rw_lo, router_b.astype(F32).reshape(N_EXPERTS, 1))


ROW_DMA_UNROLL = 4


DISPATCH_BUFFERS = 3


def _dispatch_kernel(pad_start_ref, pad_len_ref, nv_ref, pos_ref, h_hbm, x_hbm, hbuf_ref, zero_ref,
                     load_sems, row_sems, zsem):
    tm = hbuf_ref.shape[1]
    tile = zero_ref.shape[0]
    n_tiles = x_hbm.shape[0] // tile
    step = pl.program_id(0)
    n_steps = pl.num_programs(0)

    def load(t):
        return pltpu.make_async_copy(h_hbm.at[pl.ds(pl.multiple_of(t * tm, tm), tm), :],
                                     hbuf_ref.at[t % DISPATCH_BUFFERS],
                                     load_sems.at[t % DISPATCH_BUFFERS])

    def rows_done(t):
        for slot in range(2):
            pltpu.make_async_copy(hbuf_ref.at[t % DISPATCH_BUFFERS], x_hbm.at[pl.ds(0, tm), :],
                                  row_sems.at[t % DISPATCH_BUFFERS, slot]).wait()

    @pl.when(step == 0)
    def _():
        load(0).start()

        @pl.when(n_steps > 1)
        def _():
            load(1).start()

    load(step).wait()
    cur = step % DISPATCH_BUFFERS

    def issue(i, carry):
        for u in range(ROW_DMA_UNROLL):
            r = i * ROW_DMA_UNROLL + u
            for slot in range(2):
                pltpu.make_async_copy(hbuf_ref.at[cur, pl.ds(r, 1), :],
                                      x_hbm.at[pl.ds(pos_ref[0, 0, slot * tm + r], 1), :],
                                      row_sems.at[cur, slot]).start(priority=slot)
        return carry

    lax.fori_loop(0, tm // ROW_DMA_UNROLL, issue, 0)

    @pl.when(step >= 1)
    def _():
        rows_done(step - 1)

    @pl.when(step + 2 < n_steps)
    def _():
        load(step + 2).start()

    @pl.when(step == n_steps - 1)
    def _():
        rows_done(step)

    @pl.when(step == 0)
    def _():
        zero_ref[...] = jnp.zeros_like(zero_ref)

        def tcopy(i):
            return pltpu.make_async_copy(
                zero_ref, x_hbm.at[pl.ds(pl.multiple_of(i * tile, tile), tile), :], zsem)

        def tissue(i, carry):
            tcopy(i).start()
            return carry

        def twait(i, carry):
            tcopy(i).wait()
            return carry

        lax.fori_loop(nv_ref[0], n_tiles, tissue, 0)
        lax.fori_loop(nv_ref[0], n_tiles, twait, 0)
        for e in range(N_EXPERTS):
            def zcopy(i):
                return pltpu.make_async_copy(
                    zero_ref.at[pl.ds(0, 1), :], x_hbm.at[pl.ds(pad_start_ref[e] + i, 1), :], zsem)

            def zissue(i, carry):
                zcopy(i).start()
                return carry

            def zwait(i, carry):
                zcopy(i).wait()
                return carry

            lax.fori_loop(0, pad_len_ref[e], zissue, 0)
            lax.fori_loop(0, pad_len_ref[e], zwait, 0)


def dispatch(h1, pos, pad_start, pad_len, n_valid, n_rows, tile, tm=256):
    T, D = h1.shape
    tm = min(tm, T)
    return pl.pallas_call(
        _dispatch_kernel,
        grid_spec=pltpu.PrefetchScalarGridSpec(
            num_scalar_prefetch=3,
            grid=(T // tm,),
            in_specs=[pl.BlockSpec((1, 1, 2 * tm), lambda i, *_: (i, 0, 0), memory_space=pltpu.SMEM),
                      pl.BlockSpec(memory_space=pl.ANY)],
            out_specs=pl.BlockSpec(memory_space=pl.ANY),
            scratch_shapes=[pltpu.VMEM((DISPATCH_BUFFERS, tm, D), h1.dtype),
                            pltpu.VMEM((tile, D), h1.dtype),
                            pltpu.SemaphoreType.DMA((DISPATCH_BUFFERS,)),
                            pltpu.SemaphoreType.DMA((DISPATCH_BUFFERS, 2)),
                            pltpu.SemaphoreType.DMA]),
        out_shape=jax.ShapeDtypeStruct((n_rows, D), h1.dtype),
        compiler_params=_params(("arbitrary",)),
        name="dispatch",
    )(pad_start, pad_len, n_valid, pos, h1)


def _expert_kernel(layer, te_ref, nv_ref, nxt_ref, x_ref, wg_hbm, wu_hbm, wd_hbm, y_ref,
                   sg_ref, su_ref, sd_ref, wg_ref, wu_ref, wd_ref, sems):
    i = pl.program_id(0)
    valid = i < nv_ref[0]

    def stage(e):
        return [pltpu.make_async_copy(src.at[layer, e], dst, sems.at[k])
                for k, (src, dst) in enumerate(((wg_hbm, sg_ref), (wu_hbm, su_ref), (wd_hbm, sd_ref)))]

    @pl.when(i == 0)
    def _():
        for c in stage(te_ref[0]):
            c.start()

    first_of_expert = valid & ((i == 0) | (te_ref[jnp.maximum(i - 1, 0)] != te_ref[i]))

    @pl.when(first_of_expert)
    def _():
        for c in stage(te_ref[i]):
            c.wait()
        wg_ref[...] = sg_ref[...].astype(BF16)
        wu_ref[...] = su_ref[...].astype(BF16)
        wd_ref[...] = sd_ref[...].astype(BF16)

        @pl.when(nxt_ref[i] >= 0)
        def _():
            for c in stage(nxt_ref[i]):
                c.start()

    @pl.when(valid)
    def _():
        x = _unpack_rows(x_ref[...]).astype(BF16)
        gate = jnp.dot(x, wg_ref[...], preferred_element_type=F32)
        up = jnp.dot(x, wu_ref[...], preferred_element_type=F32)
        hid = (_silu(gate) * up).astype(BF16)
        y_ref[...] = _pack_rows(jnp.dot(hid, wd_ref[...], preferred_element_type=F32))

    @pl.when(jnp.logical_not(valid))
    def _():
        y_ref[...] = jnp.zeros_like(y_ref)


def expert_ffn(x_sorted, tile_expert, n_valid, next_expert, w_gate, w_up, w_down, layer, tm):
    A, DP = x_sorted.shape
    D = 2 * DP
    n_tiles = A // tm
    tile = lambda i, te, nv, nx: (jnp.minimum(i, nv[0] - 1), 0)
    out_tile = lambda i, te, nv, nx: (i, 0)
    anywhere = pl.BlockSpec(memory_space=pl.ANY)
    return pl.pallas_call(
        functools.partial(_expert_kernel, layer),
        grid_spec=pltpu.PrefetchScalarGridSpec(
            num_scalar_prefetch=3,
            grid=(n_tiles,),
            in_specs=[pl.BlockSpec((tm, DP), tile), anywhere, anywhere, anywhere],
            out_specs=pl.BlockSpec((tm, DP), out_tile),
            scratch_shapes=[pltpu.VMEM((D, D_EXPERT), F32), pltpu.VMEM((D, D_EXPERT), F32),
                            pltpu.VMEM((D_EXPERT, D), F32),
                            pltpu.VMEM((D, D_EXPERT), BF16), pltpu.VMEM((D, D_EXPERT), BF16),
                            pltpu.VMEM((D_EXPERT, D), BF16),
                            pltpu.SemaphoreType.DMA((3,))]),
        out_shape=jax.ShapeDtypeStruct((A, DP), jnp.uint32),
        compiler_params=_params(("arbitrary",)),
        name="expert_ffn",
    )(tile_expert, n_valid, next_expert, x_sorted, w_gate, w_up, w_down)


def _combine_kernel(pos_ref, pos_next_ref, h_ref, w_ref, g_ref, b_ref, y_hbm, o_ref, ob_ref,
                    buf_ref, sems):
    tm = h_ref.shape[0]
    step = pl.program_id(0)
    cur = step % 2

    def gather(p_ref, half):
        def issue(i, carry):
            for u in range(ROW_DMA_UNROLL):
                r = i * ROW_DMA_UNROLL + u
                for slot in range(2):
                    pltpu.make_async_copy(y_hbm.at[pl.ds(p_ref[0, 0, slot * tm + r], 1), :],
                                          buf_ref.at[half, slot, pl.ds(r, 1), :],
                                          sems.at[half, slot]).start(priority=slot)
            return carry

        lax.fori_loop(0, tm // ROW_DMA_UNROLL, issue, 0)

    @pl.when(step == 0)
    def _():
        gather(pos_ref, 0)

    @pl.when(step + 1 < pl.num_programs(0))
    def _():
        gather(pos_next_ref, 1 - cur)

    for slot in range(2):
        pltpu.make_async_copy(y_hbm.at[pl.ds(0, tm), :], buf_ref.at[cur, slot],
                              sems.at[cur, slot]).wait()
    w = w_ref[...]
    ffn = (w[:, 0:1] * _unpack_rows(buf_ref[cur, 0]) + w[:, 1:2] * _unpack_rows(buf_ref[cur, 1]))
    h2 = _layer_norm(DEEPNORM_ALPHA * h_ref[...] + ffn, g_ref[...], b_ref[...])
    o_ref[...] = h2
    ob_ref[...] = h2.astype(BF16)


def combine_ln(h1, y_sorted, pos, wcol, g, b, tm=256):
    T, D = h1.shape
    tm = min(tm, T)
    row = pl.BlockSpec((tm, D), lambda i: (i, 0))
    vec = pl.BlockSpec((1, D), lambda i: (0, 0))
    last = T // tm - 1
    return pl.pallas_call(
        _combine_kernel,
        grid=(T // tm,),
        in_specs=[pl.BlockSpec((1, 1, 2 * tm), lambda i: (i, 0, 0), memory_space=pltpu.SMEM),
                  pl.BlockSpec((1, 1, 2 * tm), lambda i: (jnp.minimum(i + 1, last), 0, 0),
                               memory_space=pltpu.SMEM),
                  row,
                  pl.BlockSpec((tm, 2), lambda i: (i, 0)),
                  vec, vec,
                  pl.BlockSpec(memory_space=pl.ANY)],
        out_specs=[row, row],
        out_shape=[jax.ShapeDtypeStruct((T, D), F32), jax.ShapeDtypeStruct((T, D), BF16)],
        scratch_shapes=[pltpu.VMEM((2, 2, tm, D // 2), jnp.uint32),
                        pltpu.SemaphoreType.DMA((2, 2))],
        compiler_params=_params(("arbitrary",)),
        name="combine_ln",
    )(pos, pos, h1, wcol, g.reshape(1, D), b.reshape(1, D), y_sorted)


EXPERT_TILE = 256
ROUTE_TILE = 512


def _routing_plan(route_i, counts, T, tile, route_tile):
    cnt = counts[:, 0].astype(jnp.int32)
    padded = ((cnt + tile - 1) // tile) * tile
    ends = jnp.cumsum(padded)
    offs = ends - padded
    e0, e1, r0, r1 = route_i[0], route_i[1], route_i[2], route_i[3]
    pos0 = offs[e0] + r0
    pos1 = offs[e1] + r1
    nrt = T // route_tile
    pos = jnp.concatenate([pos0.reshape(nrt, 1, route_tile), pos1.reshape(nrt, 1, route_tile)], axis=-1)
    n_tiles = (2 * T) // tile + N_EXPERTS
    tile_ids = jnp.arange(n_tiles, dtype=jnp.int32)
    tile_expert = jnp.minimum(
        jnp.sum((ends[None, :] // tile <= tile_ids[:, None]).astype(jnp.int32), axis=1),
        N_EXPERTS - 1).astype(jnp.int32)
    n_valid = (ends[-1] // tile).astype(jnp.int32).reshape(1)
    ids = jnp.arange(N_EXPERTS, dtype=jnp.int32)
    later = (ids[None, :] > ids[:, None]) & (cnt[None, :] > 0)
    following = jnp.min(jnp.where(later, ids[None, :], N_EXPERTS), axis=1)
    following = jnp.where(following == N_EXPERTS, -1, following).astype(jnp.int32)
    return (pos, tile_expert, n_valid, following[tile_expert],
            (offs + cnt).astype(jnp.int32), (padded - cnt).astype(jnp.int32))


def _hgrn_lower_bounds(hgrn_lb):
    lb = jnp.cumsum(jax.nn.softmax(hgrn_lb.astype(F32), axis=0), axis=0)
    return lb - lb[0:1]


def kernel(x, emb_ln_g, emb_ln_b, w_in, attn_sink, conv_w, hgrn_lb, hgrn_norm_g, ret_decay_logit,
           ret_norm_g, w_out, ln1_g, ln1_b, router_w, router_b, w_gate, w_up, w_down, ln2_g, ln2_b):
    B, S, D = x.shape
    T = B * S
    depth = w_in.shape[0]
    lb_all = _hgrn_lower_bounds(hgrn_lb)
    route_tile = min(ROUTE_TILE, T)
    n_rows = 2 * T + N_EXPERTS * EXPERT_TILE

    w_out_b = w_out.astype(BF16)
    h, hb = embed_ln(x.reshape(T, D), emb_ln_g, emb_ln_b)
    for l in range(depth):
        proj = in_proj(hb, w_in, l).reshape(B, S, D_IN_PROJ)
        ys = [attention(proj, attn_sink[l]),
              short_conv(proj, conv_w[l]),
              hgrn2(proj, lb_all[l], hgrn_norm_g[l]),
              retention(proj, ret_decay_logit[l], ret_norm_g[l])]
        h1, h1_packed, route_i, route_w, counts = out_proj_ln_route(
            ys, w_out_b, l, h, ln1_g[l], ln1_b[l], router_w, router_b)
        pos, tile_expert, n_valid, next_expert, pad_start, pad_len = _routing_plan(
            route_i, counts, T, EXPERT_TILE, route_tile)
        x_sorted = dispatch(h1_packed, pos, pad_start, pad_len, n_valid, n_rows, EXPERT_TILE,
                            tm=route_tile)
        y_sorted = expert_ffn(x_sorted, tile_expert, n_valid, next_expert, w_gate, w_up, w_down, l,
                              EXPERT_TILE)
        h, hb = combine_ln(h1, y_sorted, pos, route_w[0:2].T, ln2_g[l], ln2_b[l], tm=route_tile)
    return h.reshape(B, S, D)
```

```python
import functools

import jax
import jax.numpy as jnp
from jax import lax
from jax.experimental import pallas as pl
from jax.experimental.pallas import tpu as pltpu

F32 = jnp.float32
BF16 = jnp.bfloat16

D_MODEL = 2048
DEPTH = 2
W_GROUP = 512
HEAD_DIM = 64
N_ATTN_HEADS = 8
N_KV_HEADS = 2
ATTN_GROUP = N_ATTN_HEADS // N_KV_HEADS
WINDOW = 128
ATTN_BLOCK = 128
ATTN_STACK = 4
REC_HEADS = 4
REC_DIM = 128
N_EXPERTS = 16
N_GROUPS = 4
EXPERTS_PER_GROUP = 4
D_EXPERT = 1024
D_IN_PROJ = 6912
DEEPNORM_ALPHA = (2.0 * DEPTH) ** 0.25
LN_EPS = 1e-5
HEAD_NORM_EPS = 1e-6
NEG_BIG = -1e30

LANES = 128
SUBLANES = 8
COL_AQ, COL_AK, COL_AV = 0, 4, 5
COL_CB, COL_CC, COL_CH = 6, 10, 14
COL_GQ, COL_GZF, COL_GZB, COL_GI, COL_GO = 18, 22, 26, 30, 34
COL_RQ, COL_RK, COL_RV, COL_RG = 38, 42, 46, 50

CHUNK = 128
VMEM_LIMIT = 56 * 1024 * 1024

_NT = (((1,), (1,)), ((), ()))
_TN = (((0,), (0,)), ((), ()))


def _params(sem, vmem=VMEM_LIMIT):
    return pltpu.CompilerParams(dimension_semantics=sem, vmem_limit_bytes=vmem)


def _layer_norm(x, g, b):
    mu = jnp.mean(x, axis=-1, keepdims=True)
    xc = x - mu
    var = jnp.mean(xc * xc, axis=-1, keepdims=True)
    return xc * lax.rsqrt(var + LN_EPS) * g + b


def _silu(x):
    return x * (1.0 / (1.0 + jnp.exp(-x)))


def _pack_rows(x):
    n = x.shape[1] // 2
    hi = lax.bitcast_convert_type(x[:, :n].astype(BF16).astype(F32), jnp.uint32)
    lo = lax.bitcast_convert_type(x[:, n:].astype(BF16).astype(F32), jnp.uint32)
    return hi | (lo >> 16)


def _unpack_rows(w):
    hi = lax.bitcast_convert_type(w & jnp.uint32(0xFFFF0000), F32)
    lo = lax.bitcast_convert_type(w << 16, F32)
    return jnp.concatenate([hi, lo], axis=-1)


def _embed_ln_kernel(x_ref, g_ref, b_ref, h_ref, hb_ref):
    h = _layer_norm(x_ref[...], g_ref[...], b_ref[...])
    h_ref[...] = h
    hb_ref[...] = h.astype(BF16)


def embed_ln(x2, g, b, tm=1024):
    T, D = x2.shape
    return pl.pallas_call(
        _embed_ln_kernel,
        grid=(T // tm,),
        in_specs=[pl.BlockSpec((tm, D), lambda i: (i, 0)),
                  pl.BlockSpec((1, D), lambda i: (0, 0)),
                  pl.BlockSpec((1, D), lambda i: (0, 0))],
        out_specs=[pl.BlockSpec((tm, D), lambda i: (i, 0)),
                   pl.BlockSpec((tm, D), lambda i: (i, 0))],
        out_shape=[jax.ShapeDtypeStruct((T, D), F32), jax.ShapeDtypeStruct((T, D), BF16)],
        compiler_params=_params(("parallel",)),
        name="embed_ln",
    )(x2, g.reshape(1, D), b.reshape(1, D))


def _in_proj_kernel(x_ref, w_ref, o_ref, wb_ref):
    @pl.when(pl.program_id(1) == 0)
    def _():
        wb_ref[...] = w_ref[...].astype(BF16)

    o_ref[...] = jnp.dot(x_ref[...], wb_ref[...], preferred_element_type=F32)


def in_proj(hb, w_in, layer, tm=2048, tn=768):
    T, K = hb.shape
    N = w_in.shape[2]
    tm = min(tm, T)
    return pl.pallas_call(
        _in_proj_kernel,
        grid=(N // tn, T // tm),
        in_specs=[pl.BlockSpec((tm, K), lambda n, m: (m, 0)),
                  pl.BlockSpec((None, K, tn), lambda n, m: (layer, 0, n))],
        out_specs=pl.BlockSpec((tm, tn), lambda n, m: (m, n)),
        out_shape=jax.ShapeDtypeStruct((T, N), F32),
        scratch_shapes=[pltpu.VMEM((K, tn), BF16)],
        compiler_params=_params(("arbitrary", "arbitrary")),
        name="in_proj",
    )(hb, w_in)


def _attn_kernel(sink_ref, q_ref, kp_ref, kc_ref, kn_ref, vp_ref, vc_ref, vn_ref, bias_ref, o_ref):
    L = ATTN_BLOCK
    n = pl.program_id(1)
    nb = pl.num_programs(1)
    key = lax.broadcasted_iota(jnp.int32, (3 * L, 1), 0)
    valid = ((key >= L) | (n > 0)) & ((key < 2 * L) | (n < nb - 1))
    edge = jnp.where(valid, 0.0, NEG_BIG)
    q = q_ref[0]
    k3 = jnp.concatenate([kp_ref[0], kc_ref[0], kn_ref[0]], axis=0)
    v3 = jnp.concatenate([vp_ref[0], vc_ref[0], vn_ref[0]], axis=0)
    outs = []
    for h in range(N_KV_HEADS):
        kh = k3[:, h * HEAD_DIM:(h + 1) * HEAD_DIM].astype(BF16)
        vh = v3[:, h * HEAD_DIM:(h + 1) * HEAD_DIM].astype(BF16)
        for g0 in range(0, ATTN_GROUP, ATTN_STACK):
            heads = range(h * ATTN_GROUP + g0, h * ATTN_GROUP + g0 + ATTN_STACK)
            qs = jnp.concatenate([q[:, hd * HEAD_DIM:(hd + 1) * HEAD_DIM] for hd in heads], axis=0)
            qs = (qs * (HEAD_DIM ** -0.5)).astype(BF16)
            s = lax.dot_general(kh, qs, _NT, preferred_element_type=F32)
            s = s + bias_ref[h, :, g0 * L:(g0 + ATTN_STACK) * L] + edge
            sk = jnp.concatenate([jnp.full((1, L), sink_ref[hd], F32) for hd in heads], axis=1)
            m = jnp.maximum(jnp.max(s, axis=0, keepdims=True), sk)
            p = jnp.exp(s - m)
            den = jnp.sum(p, axis=0, keepdims=True) + jnp.exp(sk - m)
            o_t = lax.dot_general(vh, p.astype(BF16), _TN, preferred_element_type=F32) / den
            outs.extend(o_t[:, g * L:(g + 1) * L].T for g in range(ATTN_STACK))
    o_ref[0] = jnp.concatenate(outs, axis=-1).astype(o_ref.dtype)


def _attn_bias():
    L = ATTN_BLOCK
    k_rel = jnp.arange(3 * L) - L
    dist = jnp.abs(k_rel[None, :] - jnp.arange(L)[:, None]).astype(F32)
    slopes = 2.0 ** (-8.0 * jnp.arange(1, N_ATTN_HEADS + 1, dtype=F32) / N_ATTN_HEADS)
    bias = -slopes[:, None, None] * dist[None]
    bias = jnp.where(dist[None] <= WINDOW, bias, NEG_BIG)
    bias = bias.reshape(N_KV_HEADS, ATTN_GROUP, L, 3 * L).transpose(0, 3, 1, 2)
    return bias.reshape(N_KV_HEADS, 3 * L, ATTN_GROUP * L)


def attention(proj, sink):
    B, S, _ = proj.shape
    L = ATTN_BLOCK
    nb = S // L
    kv = lambda col, shift: pl.BlockSpec(
        (1, L, LANES), lambda b, n: (b, jnp.clip(n + shift, 0, nb - 1), col))
    return pl.pallas_call(
        _attn_kernel,
        grid=(B, nb),
        in_specs=[pl.BlockSpec(memory_space=pltpu.SMEM),
                  pl.BlockSpec((1, L, W_GROUP), lambda b, n: (b, n, COL_AQ // 4)),
                  kv(COL_AK, -1), kv(COL_AK, 0), kv(COL_AK, 1),
                  kv(COL_AV, -1), kv(COL_AV, 0), kv(COL_AV, 1),
                  pl.BlockSpec((N_KV_HEADS, 3 * L, ATTN_GROUP * L), lambda b, n: (0, 0, 0))],
        out_specs=pl.BlockSpec((1, L, W_GROUP), lambda b, n: (b, n, 0)),
        out_shape=jax.ShapeDtypeStruct((B, S, W_GROUP), BF16),
        compiler_params=_params(("parallel", "arbitrary")),
        name="attention",
    )(sink.astype(F32), proj, proj, proj, proj, proj, proj, proj, _attn_bias())


CONV_ROWS = 512
HALO = 8


def _conv_kernel(b_ref, c_ref, h_ref, w_ref, o_ref, u_ref):
    S = b_ref.shape[1]
    R = min(CONV_ROWS, S)
    u_ref[0:HALO, :] = jnp.zeros((HALO, LANES), F32)
    u_ref[S + HALO:S + 2 * HALO, :] = jnp.zeros((HALO, LANES), F32)

    def gate(i, carry):
        r = pl.multiple_of(i * R, R)
        u_ref[pl.ds(r + HALO, R), :] = c_ref[0, pl.ds(r, R), :] * h_ref[0, pl.ds(r, R), :]
        return carry

    lax.fori_loop(0, S // R, gate, 0)
    w0, w1, w2 = w_ref[0:1, :], w_ref[1:2, :], w_ref[2:3, :]

    def conv(i, carry):
        r = pl.multiple_of(i * R, R)
        a = u_ref[pl.ds(r, R + 2 * HALO), :]
        prev = pltpu.roll(a, 1, 0)[HALO:HALO + R]
        nxt = pltpu.roll(a, R + 2 * HALO - 1, 0)[HALO:HALO + R]
        y = w0 * prev + w1 * a[HALO:HALO + R] + w2 * nxt
        o_ref[0, pl.ds(r, R), :] = (b_ref[0, pl.ds(r, R), :] * y).astype(o_ref.dtype)
        return carry

    lax.fori_loop(0, S // R, conv, 0)


def short_conv(proj, conv_w):
    B, S, _ = proj.shape
    nj = W_GROUP // LANES
    col = lambda c: pl.BlockSpec((1, S, LANES), lambda b, j: (b, 0, c + j))
    return pl.pallas_call(
        _conv_kernel,
        grid=(B, nj),
        in_specs=[col(COL_CB), col(COL_CC), col(COL_CH),
                  pl.BlockSpec((3, LANES), lambda b, j: (0, j))],
        out_specs=pl.BlockSpec((1, S, LANES), lambda b, j: (b, 0, j)),
        out_shape=jax.ShapeDtypeStruct((B, S, W_GROUP), BF16),
        scratch_shapes=[pltpu.VMEM((S + 2 * HALO, LANES), F32)],
        compiler_params=_params(("parallel", "parallel")),
        name="short_conv",
    )(proj, proj, proj, conv_w.astype(F32))


def _level_codes():
    t = lax.broadcasted_iota(jnp.int32, (CHUNK, CHUNK), 0)
    s = lax.broadcasted_iota(jnp.int32, (CHUNK, CHUNK), 1)
    x = t ^ s
    hb = jnp.zeros((CHUNK, CHUNK), jnp.int32)
    c = 1
    while c < CHUNK:
        hb = jnp.where((x & c) != 0, c, hb)
        c *= 2
    diag = jnp.where(t == s, 0, -1)
    return jnp.where(t > s, hb, diag), jnp.where(t < s, hb, diag)


def _hgrn_chunk(q, v, z, lb, code, reverse):
    e = jnp.exp(-jnp.abs(z))
    r = 1.0 / (1.0 + e)
    er = e * r
    nonneg = z >= 0
    f = lb + (1.0 - lb) * jnp.where(nonneg, r, er)
    k = (1.0 - lb) * jnp.where(nonneg, er, r)
    row = lax.broadcasted_iota(jnp.int32, (CHUNK, REC_DIM), 0)
    qs = f
    ks = jnp.ones_like(f)
    blk = f
    a = jnp.where(code == 0,
                  lax.dot_general(q.astype(BF16), k.astype(BF16), _NT, preferred_element_type=F32),
                  0.0)
    c = 1
    while c < CHUNK:
        p = lax.dot_general((q * qs).astype(BF16), (k * ks).astype(BF16), _NT,
                            preferred_element_type=F32)
        a = jnp.where(code == c, p, a)
        if c < SUBLANES:
            upper = (row & c) != 0
            grouped = blk.reshape(CHUNK // SUBLANES, SUBLANES, REC_DIM)
            down = pltpu.roll(grouped, c, 1).reshape(CHUNK, REC_DIM)
            up = (down if 2 * c == SUBLANES
                  else pltpu.roll(grouped, SUBLANES - c, 1).reshape(CHUNK, REC_DIM))
            sib = jnp.where(upper, down, up)
            grow_q = jnp.logical_not(upper) if reverse else upper
            qs = qs * jnp.where(grow_q, sib, 1.0)
            ks = ks * jnp.where(grow_q, 1.0, sib)
            blk = blk * sib
        else:
            step = c // SUBLANES
            group = lambda x, j: x[j * SUBLANES:(j + 1) * SUBLANES]
            nq, nk, nb = [], [], []
            for j in range(CHUNK // SUBLANES):
                upper = (j & step) != 0
                sib = group(blk, j ^ step)
                grow_q = (not upper) if reverse else upper
                nq.append(group(qs, j) * sib if grow_q else group(qs, j))
                nk.append(group(ks, j) if grow_q else group(ks, j) * sib)
                nb.append(group(blk, j) * sib)
            qs, ks, blk = (jnp.concatenate(x, axis=0) for x in (nq, nk, nb))
        c *= 2
    return a, q * qs, k * ks, blk[0:1, :]


HGRN_HEADS_PER_STEP = 2


def _hgrn_kernel(q_ref, zf_ref, zb_ref, i_ref, g_ref, lb_ref, ng_ref, o_ref, code_ref, *scratch):
    S = q_ref.shape[1]
    nc = S // CHUNK
    nh = HGRN_HEADS_PER_STEP
    acc_refs, st_refs = scratch[:nh], scratch[nh:]
    cf, cb = _level_codes()
    code_ref[0] = cf
    code_ref[1] = cb
    for st in st_refs:
        st[...] = jnp.zeros_like(st)

    def part(ci, hh, reverse):
        z_ref = zb_ref if reverse else zf_ref
        st = st_refs[2 * hh + (1 if reverse else 0)]
        rows = pl.ds(pl.multiple_of(ci * CHUNK, CHUNK), CHUNK)
        lanes = slice(hh * LANES, (hh + 1) * LANES)
        q = q_ref[0, rows, lanes]
        v = i_ref[0, rows, lanes]
        vb = v.astype(BF16)
        a, qd, kd, dec = _hgrn_chunk(q, v, z_ref[0, rows, lanes], lb_ref[hh],
                                     code_ref[1 if reverse else 0], reverse)
        o = jnp.dot(a.astype(BF16), vb, preferred_element_type=F32)
        o = o + lax.dot_general(qd.astype(BF16), st[...].astype(BF16), _NT,
                                preferred_element_type=F32)
        st[...] = st[...] * dec + lax.dot_general(vb, kd.astype(BF16), _TN,
                                                  preferred_element_type=F32)
        return rows, lanes, o

    def finish(hh, rows, lanes, o):
        o = o + acc_refs[hh][rows, :]
        y = o * lax.rsqrt(jnp.mean(o * o, axis=-1, keepdims=True) + HEAD_NORM_EPS)
        y = y * ng_ref[hh] * _silu(g_ref[0, rows, lanes])
        o_ref[0, rows, lanes] = y.astype(o_ref.dtype)

    def first_half(j, carry):
        for hh in range(nh):
            rows, _, o = part(j, hh, False)
            acc_refs[hh][rows, :] = o
            rows, _, o = part(nc - 1 - j, hh, True)
            acc_refs[hh][rows, :] = o
        return carry

    def second_half(j, carry):
        for hh in range(nh):
            finish(hh, *part(j, hh, False))
            finish(hh, *part(nc - 1 - j, hh, True))
        return carry

    lax.fori_loop(0, nc // 2, first_half, 0)
    lax.fori_loop(nc // 2, nc, second_half, 0)


def hgrn2(proj, lb, norm_g):
    B, S, _ = proj.shape
    nh = HGRN_HEADS_PER_STEP
    assert (S // CHUNK) % 2 == 0 and REC_HEADS % nh == 0
    col = lambda c: pl.BlockSpec((1, S, nh * LANES), lambda b, h: (b, 0, c // nh + h))
    vec = pl.BlockSpec((nh, 1, LANES), lambda b, h: (h, 0, 0))
    return pl.pallas_call(
        _hgrn_kernel,
        grid=(B, REC_HEADS // nh),
        in_specs=[col(COL_GQ), col(COL_GZF), col(COL_GZB), col(COL_GI), col(COL_GO), vec, vec],
        out_specs=pl.BlockSpec((1, S, nh * LANES), lambda b, h: (b, 0, h)),
        out_shape=jax.ShapeDtypeStruct((B, S, W_GROUP), BF16),
        scratch_shapes=([pltpu.VMEM((2, CHUNK, CHUNK), jnp.int32)]
                        + [pltpu.VMEM((S, REC_DIM), F32)] * nh
                        + [pltpu.VMEM((REC_DIM, REC_DIM), F32)] * (2 * nh)),
        compiler_params=_params(("parallel", "parallel")),
        name="hgrn2",
    )(proj, proj, proj, proj, proj,
      lb.astype(F32).reshape(REC_HEADS, 1, LANES), norm_g.astype(F32).reshape(REC_HEADS, 1, LANES))


RET_HEADS_PER_STEP = 2


def _retention_kernel(dl_ref, q_ref, k_ref, v_ref, g_ref, ng_ref, o_ref, const_ref, *scratch):
    S = q_ref.shape[1]
    nc = S // CHUNK
    nh = RET_HEADS_PER_STEP
    acc_refs, st_refs = scratch[:nh], scratch[nh:]
    scale = REC_DIM ** -0.5
    t = lax.broadcasted_iota(jnp.int32, (CHUNK, CHUNK), 0)
    s = lax.broadcasted_iota(jnp.int32, (CHUNK, CHUNK), 1)
    rel = (t - s).astype(F32)
    pos = lax.broadcasted_iota(jnp.int32, (CHUNK, REC_DIM), 0).astype(F32)
    chunk_decay = []
    for hh in range(nh):
        head = pl.program_id(1) * nh + hh

        def log_gamma(d):
            x = jnp.full((1, LANES), dl_ref[d, head], F32)
            return jnp.minimum(x, 0.0) - jnp.log1p(jnp.exp(-jnp.abs(x)))

        lgf, lgb = log_gamma(0), log_gamma(1)
        const_ref[hh, 0] = (jnp.where(t >= s, jnp.exp(lgf * rel), 0.0)
                            + jnp.where(s >= t, jnp.exp(-lgb * rel), 0.0)) * scale
        const_ref[hh, 1] = jnp.exp(lgf * (pos + 1.0))
        const_ref[hh, 2] = jnp.exp(lgf * (CHUNK - 1.0 - pos)) * scale
        const_ref[hh, 3] = jnp.exp(lgb * (CHUNK - pos))
        const_ref[hh, 4] = jnp.exp(lgb * pos) * scale
        chunk_decay.append((jnp.exp(lgf * CHUNK), jnp.exp(lgb * CHUNK)))
    for st in st_refs:
        st[...] = jnp.zeros_like(st)

    def load(ci, hh):
        rows = pl.ds(pl.multiple_of(ci * CHUNK, CHUNK), CHUNK)
        lanes = slice(hh * LANES, (hh + 1) * LANES)
        return rows, lanes, q_ref[0, rows, lanes], k_ref[0, rows, lanes], v_ref[0, rows, lanes].astype(BF16)

    def fwd_part(ci, hh):
        rows, lanes, q, k, vb = load(ci, hh)
        st = st_refs[2 * hh]
        a = lax.dot_general(q.astype(BF16), k.astype(BF16), _NT, preferred_element_type=F32)
        o = jnp.dot((a * const_ref[hh, 0]).astype(BF16), vb, preferred_element_type=F32)
        o = o + lax.dot_general((q * const_ref[hh, 1]).astype(BF16), st[...].astype(BF16), _NT,
                                preferred_element_type=F32)
        st[...] = st[...] * chunk_decay[hh][0] + lax.dot_general(
            vb, (k * const_ref[hh, 2]).astype(BF16), _TN, preferred_element_type=F32)
        return rows, lanes, o

    def bwd_part(ci, hh):
        rows, lanes, q, k, vb = load(ci, hh)
        st = st_refs[2 * hh + 1]
        o = lax.dot_general((q * const_ref[hh, 3]).astype(BF16), st[...].astype(BF16), _NT,
                            preferred_element_type=F32)
        st[...] = st[...] * chunk_decay[hh][1] + lax.dot_general(
            vb, (k * const_ref[hh, 4]).astype(BF16), _TN, preferred_element_type=F32)
        return rows, lanes, o

    def finish(rows, lanes, hh, o):
        o = o + acc_refs[hh][rows, :]
        oc = o - jnp.mean(o, axis=-1, keepdims=True)
        y = oc * lax.rsqrt(jnp.mean(oc * oc, axis=-1, keepdims=True) + HEAD_NORM_EPS)
        o_ref[0, rows, lanes] = (y * ng_ref[hh] * _silu(g_ref[0, rows, lanes])).astype(o_ref.dtype)

    def first_half(j, carry):
        for hh in range(nh):
            rows, _, o = fwd_part(j, hh)
            acc_refs[hh][rows, :] = o
            rows, _, o = bwd_part(nc - 1 - j, hh)
            acc_refs[hh][rows, :] = o
        return carry

    def second_half(j, carry):
        for hh in range(nh):
            rows, lanes, o = fwd_part(j, hh)
            finish(rows, lanes, hh, o)
            rows, lanes, o = bwd_part(nc - 1 - j, hh)
            finish(rows, lanes, hh, o)
        return carry

    lax.fori_loop(0, nc // 2, first_half, 0)
    lax.fori_loop(nc // 2, nc, second_half, 0)


def retention(proj, decay_logit, norm_g):
    B, S, _ = proj.shape
    nh = RET_HEADS_PER_STEP
    assert (S // CHUNK) % 2 == 0 and REC_HEADS % nh == 0
    col = lambda c: pl.BlockSpec((1, S, nh * LANES), lambda b, h: (b, 0, c // nh + h))
    return pl.pallas_call(
        _retention_kernel,
        grid=(B, REC_HEADS // nh),
        in_specs=[pl.BlockSpec(memory_space=pltpu.SMEM),
                  col(COL_RQ), col(COL_RK), col(COL_RV), col(COL_RG),
                  pl.BlockSpec((nh, 1, LANES), lambda b, h: (h, 0, 0))],
        out_specs=pl.BlockSpec((1, S, nh * LANES), lambda b, h: (b, 0, h)),
        out_shape=jax.ShapeDtypeStruct((B, S, W_GROUP), BF16),
        scratch_shapes=([pltpu.VMEM((nh, 5, CHUNK, REC_DIM), F32)]
                        + [pltpu.VMEM((S, REC_DIM), F32)] * nh
                        + [pltpu.VMEM((REC_DIM, REC_DIM), F32)] * (2 * nh)),
        compiler_params=_params(("parallel", "parallel")),
        name="retention",
    )(decay_logit.astype(F32), proj, proj, proj, proj,
      norm_g.astype(F32).reshape(REC_HEADS, 1, LANES))


ROUTE_ROWS = 8

def _first_max(vals):
    best, idx = vals[0], jnp.zeros(vals[0].shape, jnp.int32)
    for i in range(1, len(vals)):
        take = vals[i] > best
        best = jnp.where(take, vals[i], best)
        idx = jnp.where(take, i, idx)
    return best, idx


def _pick(idx, vals):
    out = vals[-1]
    for i in range(len(vals) - 2, -1, -1):
        out = jnp.where(idx == i, vals[i], out)
    return out


def _out_proj_kernel(ya_ref, yc_ref, yg_ref, yr_ref, w_ref, h_ref, g_ref, b_ref, rwh_ref, rwl_ref,
                     rb_ref, h1_ref, h1p_ref, ri_ref, rwt_ref, cnt_ref, carry_ref, ycat_ref,
                     before_ref):
    tm = h_ref.shape[0]

    @pl.when(pl.program_id(0) == 0)
    def _():
        carry_ref[...] = jnp.zeros_like(carry_ref)
        before_ref[...] = (lax.broadcasted_iota(jnp.int32, (tm, tm), 0)
                           < lax.broadcasted_iota(jnp.int32, (tm, tm), 1)).astype(BF16)

    ycat_ref[:, 0:W_GROUP] = ya_ref[...]
    ycat_ref[:, W_GROUP:2 * W_GROUP] = yc_ref[...]
    ycat_ref[:, 2 * W_GROUP:3 * W_GROUP] = yg_ref[...]
    ycat_ref[:, 3 * W_GROUP:4 * W_GROUP] = yr_ref[...]
    mix = jnp.dot(ycat_ref[...], w_ref[...], preferred_element_type=F32)
    h1 = _layer_norm(DEEPNORM_ALPHA * h_ref[...] + mix, g_ref[...], b_ref[...])
    h1_ref[...] = h1
    h1p_ref[...] = _pack_rows(h1)

    h1_hi = h1.astype(BF16)
    h1_lo = (h1 - h1_hi.astype(F32)).astype(BF16)
    logits = (lax.dot_general(rwh_ref[...], h1_hi, _NT, preferred_element_type=F32)
              + lax.dot_general(rwh_ref[...], h1_lo, _NT, preferred_element_type=F32)
              + lax.dot_general(rwl_ref[...], h1_hi, _NT, preferred_element_type=F32)
              + rb_ref[...])
    rows = [logits[e:e + 1, :] for e in range(N_EXPERTS)]
    mx = functools.reduce(jnp.maximum, rows)
    ex = [jnp.exp(r - mx) for r in rows]
    den = functools.reduce(jnp.add, ex)
    pr = [x / den for x in ex]
    scores = []
    for g in range(N_GROUPS):
        a, b, c, d = pr[4 * g:4 * g + 4]
        hi1, lo1, hi2, lo2 = jnp.maximum(a, b), jnp.minimum(a, b), jnp.maximum(c, d), jnp.minimum(c, d)
        scores.append(jnp.maximum(hi1, hi2) + jnp.maximum(jnp.minimum(hi1, hi2), jnp.maximum(lo1, lo2)))
    _, gsel = _first_max(scores)
    cand = [_pick(gsel, [pr[4 * g + i] for g in range(N_GROUPS)]) for i in range(EXPERTS_PER_GROUP)]
    p0, i0 = _first_max(cand)
    p1, i1 = _first_max([jnp.where(i0 == i, -1.0, cand[i]) for i in range(EXPERTS_PER_GROUP)])
    e0 = gsel * EXPERTS_PER_GROUP + i0
    e1 = gsel * EXPERTS_PER_GROUP + i1
    tot = p0 + p1
    w0, w1 = p0 / tot, p1 / tot

    ind = jnp.concatenate([((e0 == e) | (e1 == e)).astype(F32) for e in range(N_EXPERTS)], axis=0)
    rank = jnp.dot(ind.astype(BF16), before_ref[...], preferred_element_type=F32) + carry_ref[...]
    carry_ref[...] = carry_ref[...] + jnp.sum(ind, axis=-1, keepdims=True)
    rk = [rank[e:e + 1, :] for e in range(N_EXPERTS)]
    r0 = _pick(e0, rk).astype(jnp.int32)
    r1 = _pick(e1, rk).astype(jnp.int32)
    zi = jnp.zeros((ROUTE_ROWS - 4, tm), jnp.int32)
    ri_ref[...] = jnp.concatenate([e0, e1, r0, r1, zi], axis=0)
    rwt_ref[...] = jnp.concatenate([w0, w1, jnp.zeros((ROUTE_ROWS - 2, tm), F32)], axis=0)
    cnt_ref[...] = jnp.broadcast_to(carry_ref[...], cnt_ref.shape)


def out_proj_ln_route(ys, w_out_bf16, layer, h, g, b, router_w, router_b, tm=512):
    T, D = h.shape
    tm = min(tm, T)
    part = pl.BlockSpec((tm, W_GROUP), lambda i: (i, 0))
    row = pl.BlockSpec((tm, D), lambda i: (i, 0))
    vec = pl.BlockSpec((1, D), lambda i: (0, 0))
    route = pl.BlockSpec((ROUTE_ROWS, tm), lambda i: (0, i))
    once = pl.Buffered(1)
    rw_t = router_w.astype(F32).T
    rw_hi = rw_t.astype(BF16)
    rw_lo = (rw_t - rw_hi.astype(F32)).astype(BF16)
    return pl.pallas_call(
        _out_proj_kernel,
        grid=(T // tm,),
        in_specs=[part, part, part, part,
                  pl.BlockSpec((None, D, D), lambda i: (layer, 0, 0), pipeline_mode=once),
                  row, vec, vec,
                  pl.BlockSpec((N_EXPERTS, D), lambda i: (0, 0), pipeline_mode=once),
                  pl.BlockSpec((N_EXPERTS, D), lambda i: (0, 0), pipeline_mode=once),
                  pl.BlockSpec((N_EXPERTS, 1), lambda i: (0, 0))],
        out_specs=[row, pl.BlockSpec((tm, D // 2), lambda i: (i, 0)), route, route,
                   pl.BlockSpec((N_EXPERTS, LANES), lambda i: (0, 0))],
        out_shape=[jax.ShapeDtypeStruct((T, D), F32),
                   jax.ShapeDtypeStruct((T, D // 2), jnp.uint32),
                   jax.ShapeDtypeStruct((ROUTE_ROWS, T), jnp.int32),
                   jax.ShapeDtypeStruct((ROUTE_ROWS, T), F32),
                   jax.ShapeDtypeStruct((N_EXPERTS, LANES), F32)],
        scratch_shapes=[pltpu.VMEM((N_EXPERTS, 1), F32), pltpu.VMEM((tm, D), BF16),
                        pltpu.VMEM((tm, tm), BF16)],
        compiler_params=_params(("arbitrary",)),
        name="out_proj_ln_route",
    )(*[y.reshape(T, W_GROUP) for y in ys], w_out_bf16, h, g.reshape(1, D), b.reshape(1, D),
      rw_hi, rw_lo, router_b.astype(F32).reshape(N_EXPERTS, 1))


ROW_DMA_UNROLL = 8


DISPATCH_BUFFERS = 3


def _dispatch_kernel(pad_start_ref, pad_len_ref, nv_ref, pos_ref, h_hbm, x_hbm, hbuf_ref, zero_ref,
                     load_sems, row_sems, zsem):
    tm = hbuf_ref.shape[1]
    tile = zero_ref.shape[0]
    n_tiles = x_hbm.shape[0] // tile
    step = pl.program_id(0)
    n_steps = pl.num_programs(0)

    def load(t):
        return pltpu.make_async_copy(h_hbm.at[pl.ds(pl.multiple_of(t * tm, tm), tm), :],
                                     hbuf_ref.at[t % DISPATCH_BUFFERS],
                                     load_sems.at[t % DISPATCH_BUFFERS])

    def rows_done(t):
        for slot in range(2):
            pltpu.make_async_copy(hbuf_ref.at[t % DISPATCH_BUFFERS], x_hbm.at[pl.ds(0, tm), :],
                                  row_sems.at[t % DISPATCH_BUFFERS, slot]).wait()

    @pl.when(step == 0)
    def _():
        load(0).start()

        @pl.when(n_steps > 1)
        def _():
            load(1).start()

    load(step).wait()
    cur = step % DISPATCH_BUFFERS

    def issue(i, carry):
        for u in range(ROW_DMA_UNROLL):
            r = i * ROW_DMA_UNROLL + u
            for slot in range(2):
                pltpu.make_async_copy(hbuf_ref.at[cur, pl.ds(r, 1), :],
                                      x_hbm.at[pl.ds(pos_ref[0, 0, slot * tm + r], 1), :],
                                      row_sems.at[cur, slot]).start(priority=slot)
        return carry

    lax.fori_loop(0, tm // ROW_DMA_UNROLL, issue, 0)

    @pl.when(step >= 1)
    def _():
        rows_done(step - 1)

    @pl.when(step + 2 < n_steps)
    def _():
        load(step + 2).start()

    @pl.when(step == n_steps - 1)
    def _():
        rows_done(step)

    @pl.when(step == 0)
    def _():
        zero_ref[...] = jnp.zeros_like(zero_ref)

        def tcopy(i):
            return pltpu.make_async_copy(
                zero_ref, x_hbm.at[pl.ds(pl.multiple_of(i * tile, tile), tile), :], zsem)

        def tissue(i, carry):
            tcopy(i).start()
            return carry

        def twait(i, carry):
            tcopy(i).wait()
            return carry

        lax.fori_loop(nv_ref[0], n_tiles, tissue, 0)
        lax.fori_loop(nv_ref[0], n_tiles, twait, 0)
        for e in range(N_EXPERTS):
            def zcopy(i):
                return pltpu.make_async_copy(
                    zero_ref.at[pl.ds(0, 1), :], x_hbm.at[pl.ds(pad_start_ref[e] + i, 1), :], zsem)

            def zissue(i, carry):
                zcopy(i).start()
                return carry

            def zwait(i, carry):
                zcopy(i).wait()
                return carry

            lax.fori_loop(0, pad_len_ref[e], zissue, 0)
            lax.fori_loop(0, pad_len_ref[e], zwait, 0)


def dispatch(h1, pos, pad_start, pad_len, n_valid, n_rows, tile, tm=256):
    T, D = h1.shape
    tm = min(tm, T)
    return pl.pallas_call(
        _dispatch_kernel,
        grid_spec=pltpu.PrefetchScalarGridSpec(
            num_scalar_prefetch=3,
            grid=(T // tm,),
            in_specs=[pl.BlockSpec((1, 1, 2 * tm), lambda i, *_: (i, 0, 0), memory_space=pltpu.SMEM),
                      pl.BlockSpec(memory_space=pl.ANY)],
            out_specs=pl.BlockSpec(memory_space=pl.ANY),
            scratch_shapes=[pltpu.VMEM((DISPATCH_BUFFERS, tm, D), h1.dtype),
                            pltpu.VMEM((tile, D), h1.dtype),
                            pltpu.SemaphoreType.DMA((DISPATCH_BUFFERS,)),
                            pltpu.SemaphoreType.DMA((DISPATCH_BUFFERS, 2)),
                            pltpu.SemaphoreType.DMA]),
        out_shape=jax.ShapeDtypeStruct((n_rows, D), h1.dtype),
        compiler_params=_params(("arbitrary",)),
        name="dispatch",
    )(pad_start, pad_len, n_valid, pos, h1)


def _expert_kernel(layer, te_ref, nv_ref, nxt_ref, x_ref, wg_hbm, wu_hbm, wd_hbm, y_ref,
                   sg_ref, su_ref, sd_ref, wg_ref, wu_ref, wd_ref, sems):
    i = pl.program_id(0)
    valid = i < nv_ref[0]

    def stage(e):
        return [pltpu.make_async_copy(src.at[layer, e], dst, sems.at[k])
                for k, (src, dst) in enumerate(((wg_hbm, sg_ref), (wu_hbm, su_ref), (wd_hbm, sd_ref)))]

    @pl.when(i == 0)
    def _():
        for c in stage(te_ref[0]):
            c.start()

    first_of_expert = valid & ((i == 0) | (te_ref[jnp.maximum(i - 1, 0)] != te_ref[i]))

    @pl.when(first_of_expert)
    def _():
        for c in stage(te_ref[i]):
            c.wait()
        wg_ref[...] = sg_ref[...].astype(BF16)
        wu_ref[...] = su_ref[...].astype(BF16)
        wd_ref[...] = sd_ref[...].astype(BF16)

        @pl.when(nxt_ref[i] >= 0)
        def _():
            for c in stage(nxt_ref[i]):
                c.start()

    @pl.when(valid)
    def _():
        x = _unpack_rows(x_ref[...]).astype(BF16)
        gate = jnp.dot(x, wg_ref[...], preferred_element_type=F32)
        up = jnp.dot(x, wu_ref[...], preferred_element_type=F32)
        hid = (_silu(gate) * up).astype(BF16)
        y_ref[...] = _pack_rows(jnp.dot(hid, wd_ref[...], preferred_element_type=F32))

    @pl.when(jnp.logical_not(valid))
    def _():
        y_ref[...] = jnp.zeros_like(y_ref)


def expert_ffn(x_sorted, tile_expert, n_valid, next_expert, w_gate, w_up, w_down, layer, tm):
    A, DP = x_sorted.shape
    D = 2 * DP
    n_tiles = A // tm
    tile = lambda i, te, nv, nx: (jnp.minimum(i, nv[0] - 1), 0)
    out_tile = lambda i, te, nv, nx: (i, 0)
    anywhere = pl.BlockSpec(memory_space=pl.ANY)
    return pl.pallas_call(
        functools.partial(_expert_kernel, layer),
        grid_spec=pltpu.PrefetchScalarGridSpec(
            num_scalar_prefetch=3,
            grid=(n_tiles,),
            in_specs=[pl.BlockSpec((tm, DP), tile), anywhere, anywhere, anywhere],
            out_specs=pl.BlockSpec((tm, DP), out_tile),
            scratch_shapes=[pltpu.VMEM((D, D_EXPERT), F32), pltpu.VMEM((D, D_EXPERT), F32),
                            pltpu.VMEM((D_EXPERT, D), F32),
                            pltpu.VMEM((D, D_EXPERT), BF16), pltpu.VMEM((D, D_EXPERT), BF16),
                            pltpu.VMEM((D_EXPERT, D), BF16),
                            pltpu.SemaphoreType.DMA((3,))]),
        out_shape=jax.ShapeDtypeStruct((A, DP), jnp.uint32),
        compiler_params=_params(("arbitrary",)),
        name="expert_ffn",
    )(tile_expert, n_valid, next_expert, x_sorted, w_gate, w_up, w_down)


def _combine_kernel(pos_ref, pos_next_ref, h_ref, w_ref, g_ref, b_ref, y_hbm, o_ref, ob_ref,
                    buf_ref, sems):
    tm = h_ref.shape[0]
    step = pl.program_id(0)
    cur = step % 2

    def gather(p_ref, half):
        def issue(i, carry):
            for u in range(ROW_DMA_UNROLL):
                r = i * ROW_DMA_UNROLL + u
                for slot in range(2):
                    pltpu.make_async_copy(y_hbm.at[pl.ds(p_ref[0, 0, slot * tm + r], 1), :],
                                          buf_ref.at[half, slot, pl.ds(r, 1), :],
                                          sems.at[half, slot]).start(priority=slot)
            return carry

        lax.fori_loop(0, tm // ROW_DMA_UNROLL, issue, 0)

    @pl.when(step == 0)
    def _():
        gather(pos_ref, 0)

    @pl.when(step + 1 < pl.num_programs(0))
    def _():
        gather(pos_next_ref, 1 - cur)

    for slot in range(2):
        pltpu.make_async_copy(y_hbm.at[pl.ds(0, tm), :], buf_ref.at[cur, slot],
                              sems.at[cur, slot]).wait()
    w = w_ref[...]
    ffn = (w[:, 0:1] * _unpack_rows(buf_ref[cur, 0]) + w[:, 1:2] * _unpack_rows(buf_ref[cur, 1]))
    h2 = _layer_norm(DEEPNORM_ALPHA * h_ref[...] + ffn, g_ref[...], b_ref[...])
    o_ref[...] = h2
    ob_ref[...] = h2.astype(BF16)


def combine_ln(h1, y_sorted, pos, wcol, g, b, tm=256):
    T, D = h1.shape
    tm = min(tm, T)
    row = pl.BlockSpec((tm, D), lambda i: (i, 0))
    vec = pl.BlockSpec((1, D), lambda i: (0, 0))
    last = T // tm - 1
    return pl.pallas_call(
        _combine_kernel,
        grid=(T // tm,),
        in_specs=[pl.BlockSpec((1, 1, 2 * tm), lambda i: (i, 0, 0), memory_space=pltpu.SMEM),
                  pl.BlockSpec((1, 1, 2 * tm), lambda i: (jnp.minimum(i + 1, last), 0, 0),
                               memory_space=pltpu.SMEM),
                  row,
                  pl.BlockSpec((tm, 2), lambda i: (i, 0)),
                  vec, vec,
                  pl.BlockSpec(memory_space=pl.ANY)],
        out_specs=[row, row],
        out_shape=[jax.ShapeDtypeStruct((T, D), F32), jax.ShapeDtypeStruct((T, D), BF16)],
        scratch_shapes=[pltpu.VMEM((2, 2, tm, D // 2), jnp.uint32),
                        pltpu.SemaphoreType.DMA((2, 2))],
        compiler_params=_params(("arbitrary",)),
        name="combine_ln",
    )(pos, pos, h1, wcol, g.reshape(1, D), b.reshape(1, D), y_sorted)


EXPERT_TILE = 256
ROUTE_TILE = 512


def _routing_plan(route_i, counts, T, tile, route_tile):
    cnt = counts[:, 0].astype(jnp.int32)
    padded = ((cnt + tile - 1) // tile) * tile
    ends = jnp.cumsum(padded)
    offs = ends - padded
    e0, e1, r0, r1 = route_i[0], route_i[1], route_i[2], route_i[3]
    pos0 = offs[e0] + r0
    pos1 = offs[e1] + r1
    nrt = T // route_tile
    pos = jnp.concatenate([pos0.reshape(nrt, 1, route_tile), pos1.reshape(nrt, 1, route_tile)], axis=-1)
    n_tiles = (2 * T) // tile + N_EXPERTS
    tile_ids = jnp.arange(n_tiles, dtype=jnp.int32)
    tile_expert = jnp.minimum(
        jnp.sum((ends[None, :] // tile <= tile_ids[:, None]).astype(jnp.int32), axis=1),
        N_EXPERTS - 1).astype(jnp.int32)
    n_valid = (ends[-1] // tile).astype(jnp.int32).reshape(1)
    ids = jnp.arange(N_EXPERTS, dtype=jnp.int32)
    later = (ids[None, :] > ids[:, None]) & (cnt[None, :] > 0)
    following = jnp.min(jnp.where(later, ids[None, :], N_EXPERTS), axis=1)
    following = jnp.where(following == N_EXPERTS, -1, following).astype(jnp.int32)
    return (pos, tile_expert, n_valid, following[tile_expert],
            (offs + cnt).astype(jnp.int32), (padded - cnt).astype(jnp.int32))


def _hgrn_lower_bounds(hgrn_lb):
    lb = jnp.cumsum(jax.nn.softmax(hgrn_lb.astype(F32), axis=0), axis=0)
    return lb - lb[0:1]


def kernel(x, emb_ln_g, emb_ln_b, w_in, attn_sink, conv_w, hgrn_lb, hgrn_norm_g, ret_decay_logit,
           ret_norm_g, w_out, ln1_g, ln1_b, router_w, router_b, w_gate, w_up, w_down, ln2_g, ln2_b):
    B, S, D = x.shape
    T = B * S
    depth = w_in.shape[0]
    lb_all = _hgrn_lower_bounds(hgrn_lb)
    route_tile = min(ROUTE_TILE, T)
    n_rows = 2 * T + N_EXPERTS * EXPERT_TILE

    w_out_b = w_out.astype(BF16)
    h, hb = embed_ln(x.reshape(T, D), emb_ln_g, emb_ln_b)
    for l in range(depth):
        proj = in_proj(hb, w_in, l).reshape(B, S, D_IN_PROJ)
        ys = [attention(proj, attn_sink[l]),
              short_conv(proj, conv_w[l]),
              hgrn2(proj, lb_all[l], hgrn_norm_g[l]),
              retention(proj, ret_decay_logit[l], ret_norm_g[l])]
        h1, h1_packed, route_i, route_w, counts = out_proj_ln_route(
            ys, w_out_b, l, h, ln1_g[l], ln1_b[l], router_w, router_b)
        pos, tile_expert, n_valid, next_expert, pad_start, pad_len = _routing_plan(
            route_i, counts, T, EXPERT_TILE, route_tile)
        x_sorted = dispatch(h1_packed, pos, pad_start, pad_len, n_valid, n_rows, EXPERT_TILE,
                            tm=route_tile)
        y_sorted = expert_ffn(x_sorted, tile_expert, n_valid, next_expert, w_gate, w_up, w_down, l,
                              EXPERT_TILE)
        h, hb = combine_ln(h1, y_sorted, pos, route_w[0:2].T, ln2_g[l], ln2_b[l], tm=route_tile)
    return h.reshape(B, S, D)
```
